```python
import math
import jax, jax.numpy as jnp
from jax import lax
import numpy as np

D_MODEL = 1024
BATCH = 8
SEQ = 4096
DEPTH = 1

CONV_DIM = 512
CONV_K = 3
N_HEADS = 8
N_KV_HEADS = 2
HEAD_DIM = 64
Q_DIM = N_HEADS * HEAD_DIM
KV_DIM = N_KV_HEADS * HEAD_DIM
GROUP = N_HEADS // N_KV_HEADS
WINDOW = 128
BLOCK = 128
N_BRANCHES = 2
N_BUCKETS = 32
MAX_DISTANCE = 128
PEER_HEADS = 8
N_KEYS = 128
N_EXPERTS = N_KEYS * N_KEYS
PEER_QDIM = 256
PEER_HALF = PEER_QDIM // 2
PEER_TOPK = 16
PEER_CHUNK = 128
EPS = 1e-6
IN_DIM = 3 * CONV_DIM + Q_DIM + 2 * KV_DIM + N_BRANCHES * D_MODEL

kernel_name = "hybrid_conv_swa_sink_peer_block"


def rms_norm(x, g):
    xf = x.astype(jnp.float32)
    y = xf * lax.rsqrt(jnp.mean(xf * xf, axis=-1, keepdims=True) + EPS)
    return (y * g.astype(jnp.float32)).astype(x.dtype)


def t5_causal_bucket(dist):
    max_exact = N_BUCKETS // 2
    d = jnp.maximum(dist, 0)
    df = jnp.maximum(d, 1).astype(jnp.float32)
    large = max_exact + (jnp.log(df / max_exact) / math.log(MAX_DISTANCE / max_exact)
                         * (N_BUCKETS - max_exact)).astype(jnp.int32)
    large = jnp.minimum(large, N_BUCKETS - 1)
    return jnp.where(d < max_exact, d, large)


def band_bias_and_mask(rel_bias, n_blocks):
    q_loc = jnp.arange(BLOCK, dtype=jnp.int32)[:, None]
    k_loc = jnp.arange(2 * BLOCK, dtype=jnp.int32)[None, :]
    dist = q_loc + BLOCK - k_loc
    bias = jnp.take(rel_bias, t5_causal_bucket(dist), axis=0)
    bias = jnp.transpose(bias, (2, 0, 1)).reshape(N_KV_HEADS, GROUP, 1, BLOCK, 2 * BLOCK)
    blk = jnp.arange(n_blocks, dtype=jnp.int32)[:, None, None]
    k_pos = blk * BLOCK - BLOCK + k_loc[None]
    valid = (dist[None] >= 0) & (dist[None] < WINDOW) & (k_pos >= 0)
    return bias.astype(jnp.float32), valid


def short_gated_conv(u, b_gate, c_gate, w_conv):
    h = c_gate * u
    kern = w_conv[:, None, :].astype(h.dtype)
    y = lax.conv_general_dilated(h, kern, window_strides=(1,), padding=[(CONV_K - 1, 0)],
                                 dimension_numbers=("NWC", "WIO", "NWC"),
                                 feature_group_count=CONV_DIM)
    return b_gate * y


def sliding_window_attention(q, k, v, sinks, bias, valid):
    bsz, s = q.shape[0], q.shape[1]
    nb = s // BLOCK
    qb = q.reshape(bsz, nb, BLOCK, N_KV_HEADS, GROUP, HEAD_DIM)
    kp = jnp.pad(k, ((0, 0), (BLOCK, 0), (0, 0), (0, 0)))
    vp = jnp.pad(v, ((0, 0), (BLOCK, 0), (0, 0), (0, 0)))
    kb = jnp.concatenate([kp[:, :s].reshape(bsz, nb, BLOCK, N_KV_HEADS, HEAD_DIM),
                          k.reshape(bsz, nb, BLOCK, N_KV_HEADS, HEAD_DIM)], axis=2)
    vb = jnp.concatenate([vp[:, :s].reshape(bsz, nb, BLOCK, N_KV_HEADS, HEAD_DIM),
                          v.reshape(bsz, nb, BLOCK, N_KV_HEADS, HEAD_DIM)], axis=2)
    logits = jnp.einsum("bnqkgd,bnskd->bkgnqs", qb, kb).astype(jnp.float32)
    logits = jnp.where(valid, logits + bias, -jnp.inf)
    sink = sinks.astype(jnp.float32).reshape(N_KV_HEADS, GROUP, 1, 1, 1)
    m = jnp.maximum(jnp.max(logits, axis=-1, keepdims=True), sink)
    p = jnp.exp(logits - m)
    denom = jnp.sum(p, axis=-1, keepdims=True) + jnp.exp(sink - m)
    p = (p / denom).astype(v.dtype)
    out = jnp.einsum("bkgnqs,bnskd->bnqkgd", p, vb)
    return out.reshape(bsz, s, Q_DIM)


def peer_ffn(x, w_query, sub_keys, expert_u, expert_v):
    bsz, s, d = x.shape
    t = bsz * s
    xt = x.reshape(t, d)
    q = (xt @ w_query).reshape(t, PEER_HEADS, 2, PEER_HALF)
    scores = jnp.einsum("thpd,hpnd->thpn", q, sub_keys).astype(jnp.float32)
    s_half, i_half = lax.top_k(scores, PEER_TOPK)
    cand_s = s_half[:, :, 0, :, None] + s_half[:, :, 1, None, :]
    cand_i = i_half[:, :, 0, :, None] * N_KEYS + i_half[:, :, 1, None, :]
    top_s, top_pos = lax.top_k(cand_s.reshape(t, PEER_HEADS, PEER_TOPK * PEER_TOPK), PEER_TOPK)
    idx = jnp.take_along_axis(cand_i.reshape(t, PEER_HEADS, PEER_TOPK * PEER_TOPK), top_pos, axis=-1)
    gates = jax.nn.softmax(top_s, axis=-1).astype(x.dtype)
    nc = t // PEER_CHUNK
    n_sel = PEER_HEADS * PEER_TOPK

    def chunk_fn(args):
        xc, ic, gc = args
        u = jnp.take(expert_u, ic, axis=0)
        a = jax.nn.gelu(jnp.einsum("cd,ced->ce", xc, u), approximate=False)
        vv = jnp.take(expert_v, ic, axis=0)
        return jnp.einsum("ce,ced->cd", (gc * a).astype(vv.dtype), vv)

    out = lax.map(chunk_fn, (xt.reshape(nc, PEER_CHUNK, d),
                             idx.reshape(nc, PEER_CHUNK, n_sel),
                             gates.reshape(nc, PEER_CHUNK, n_sel)))
    return out.reshape(bsz, s, d)


def setup_inputs(seed: int = 0) -> dict:
    key = jax.random.key(seed)
    ks = jax.random.split(key, 18)
    f32 = jnp.float32
    nrm = lambda k, shape, scale: jax.random.normal(k, shape, f32) * scale
    return {
        "x": nrm(ks[0], (BATCH, SEQ, D_MODEL), 1.0),
        "norm_mix": 1.0 + nrm(ks[1], (DEPTH, D_MODEL), 0.02),
        "norm_ffn": 1.0 + nrm(ks[2], (DEPTH, D_MODEL), 0.02),
        "w_in": nrm(ks[3], (DEPTH, D_MODEL, IN_DIM), D_MODEL ** -0.5),
        "b_gate": nrm(ks[4], (DEPTH, N_BRANCHES * D_MODEL), 0.02),
        "w_conv": nrm(ks[5], (DEPTH, CONV_K, CONV_DIM), CONV_K ** -0.5),
        "q_norm": 1.0 + nrm(ks[6], (DEPTH, HEAD_DIM), 0.02),
        "k_norm": 1.0 + nrm(ks[7], (DEPTH, HEAD_DIM), 0.02),
        "sinks": nrm(ks[8], (DEPTH, N_HEADS), 0.5),
        "rel_bias": nrm(ks[9], (N_BUCKETS, N_HEADS), 0.5),
        "w_conv_out": nrm(ks[10], (DEPTH, CONV_DIM, D_MODEL), CONV_DIM ** -0.5),
        "w_attn_out": nrm(ks[11], (DEPTH, Q_DIM, D_MODEL), Q_DIM ** -0.5),
        "w_out": nrm(ks[12], (DEPTH, D_MODEL, D_MODEL), D_MODEL ** -0.5),
        "w_query": nrm(ks[13], (DEPTH, D_MODEL, PEER_HEADS * PEER_QDIM), D_MODEL ** -0.5),
        "sub_keys": nrm(ks[14], (DEPTH, PEER_HEADS, 2, N_KEYS, PEER_HALF), PEER_HALF ** -0.5),
        "expert_u": nrm(ks[15], (DEPTH, N_EXPERTS, D_MODEL), D_MODEL ** -0.5),
        "expert_v": nrm(ks[16], (DEPTH, N_EXPERTS, D_MODEL), PEER_HEADS ** -0.5),
    }


def reference(x, norm_mix, norm_ffn, w_in, b_gate, w_conv, q_norm, k_norm, sinks, rel_bias,
              w_conv_out, w_attn_out, w_out, w_query, sub_keys, expert_u, expert_v):
    bsz, s, _ = x.shape
    bias, valid = band_bias_and_mask(rel_bias, s // BLOCK)
    offs = np.cumsum([CONV_DIM, CONV_DIM, CONV_DIM, Q_DIM, KV_DIM, KV_DIM, D_MODEL]).tolist()
    for l in range(DEPTH):
        h = rms_norm(x, norm_mix[l])
        proj = h @ w_in[l]
        u_c, b_c, c_c, q, k, v, g_c, g_a = jnp.split(proj, offs, axis=-1)
        bg_c, bg_a = jnp.split(b_gate[l], [D_MODEL])
        gate_conv = jax.nn.sigmoid(g_c + bg_c)
        gate_attn = jax.nn.sigmoid(g_a + bg_a)
        y_conv = short_gated_conv(u_c, b_c, c_c, w_conv[l]) @ w_conv_out[l]
        qh = rms_norm(q.reshape(bsz, s, N_HEADS, HEAD_DIM), q_norm[l]) * (HEAD_DIM ** -0.5)
        kh = rms_norm(k.reshape(bsz, s, N_KV_HEADS, HEAD_DIM), k_norm[l])
        vh = v.reshape(bsz, s, N_KV_HEADS, HEAD_DIM)
        y_attn = sliding_window_attention(qh, kh, vh, sinks[l], bias, valid) @ w_attn_out[l]
        mixed = gate_conv * y_conv + gate_attn * y_attn
        x = x + mixed @ w_out[l]
        h2 = rms_norm(x, norm_ffn[l])
        x = x + peer_ffn(h2, w_query[l], sub_keys[l], expert_u[l], expert_v[l])
    return x
```

```python
import functools
import math

import jax
import jax.numpy as jnp
import numpy as np
from jax import lax
from jax.experimental import pallas as pl
from jax.experimental.pallas import tpu as pltpu

F32 = jnp.float32
BF16 = jnp.bfloat16

D_MODEL = 1024
CONV_DIM = 512
CONV_K = 3
N_HEADS = 8
N_KV_HEADS = 2
HEAD_DIM = 64
Q_DIM = N_HEADS * HEAD_DIM
KV_DIM = N_KV_HEADS * HEAD_DIM
GROUP = N_HEADS // N_KV_HEADS
WINDOW = 128
BLOCK = 128
N_BUCKETS = 32
MAX_DISTANCE = 128
PEER_HEADS = 8
N_KEYS = 128
N_EXPERTS = N_KEYS * N_KEYS
PEER_QDIM = 256
PEER_HALF = PEER_QDIM // 2
PEER_TOPK = 16
N_SEL = PEER_HEADS * PEER_TOPK
EPS = 1e-6
NEG_BIG = -1e30

SUBLANES = 8
LANES = 128

_OFFS = np.cumsum([0, CONV_DIM, CONV_DIM, CONV_DIM, Q_DIM, KV_DIM, KV_DIM, D_MODEL, D_MODEL]).tolist()

_NT = (((1,), (1,)), ((), ()))


def _split_bf16(a):
    hi = a.astype(BF16)
    lo = (a - hi.astype(F32)).astype(BF16)
    return hi, lo


def _proj_kernel(x_ref, g_ref, w_ref, bg_ref, qg_ref, kg_ref, avq_ref, avk_ref, exp_ref,
                 hc_ref, bc_ref, qn_ref, kx_ref, vx_ref, gc_ref, ga_ref):
    x = x_ref[...]
    ms = jnp.mean(x * x, axis=-1, keepdims=True)
    h = ((x * lax.rsqrt(ms + EPS)) * g_ref[...]).astype(BF16)

    def seg(i):
        return jnp.dot(h, w_ref[:, _OFFS[i]:_OFFS[i + 1]], preferred_element_type=F32)

    u = seg(0)
    hc_ref[...] = seg(2) * u
    bc_ref[...] = seg(1)

    def head_rms(a, av_ref):
        hi, lo = _split_bf16(a * a)
        return (jnp.dot(hi, av_ref[...], preferred_element_type=F32)
                + jnp.dot(lo, av_ref[...], preferred_element_type=F32))

    q = seg(3)
    qn = (q * lax.rsqrt(head_rms(q, avq_ref) + EPS)) * qg_ref[...] * (HEAD_DIM ** -0.5)
    qn_ref[...] = qn.astype(BF16)

    k = seg(4)
    kn = ((k * lax.rsqrt(head_rms(k, avk_ref) + EPS)) * kg_ref[...]).astype(BF16)
    kx_ref[...] = jnp.dot(kn, exp_ref[...], preferred_element_type=F32).astype(BF16)
    v = seg(5).astype(BF16)
    vx_ref[...] = jnp.dot(v, exp_ref[...], preferred_element_type=F32).astype(BF16)

    bg = bg_ref[...]
    gc_ref[...] = jax.nn.sigmoid(seg(6) + bg[:, :D_MODEL]).astype(BF16)
    ga_ref[...] = jax.nn.sigmoid(seg(7) + bg[:, D_MODEL:]).astype(BF16)


def _proj_call(x2, norm_mix, w_in, b_gate, q_norm, k_norm, tm):
    t = x2.shape[0]
    in_dim = w_in.shape[1]
    avq = jnp.kron(jnp.eye(N_HEADS, dtype=F32), jnp.full((HEAD_DIM, HEAD_DIM), 1.0 / HEAD_DIM, F32)).astype(BF16)
    avk = jnp.kron(jnp.eye(N_KV_HEADS, dtype=F32), jnp.full((HEAD_DIM, HEAD_DIM), 1.0 / HEAD_DIM, F32)).astype(BF16)
    expand = jnp.kron(jnp.kron(jnp.eye(N_KV_HEADS, dtype=F32), jnp.ones((1, GROUP), F32)),
                      jnp.eye(HEAD_DIM, dtype=F32)).astype(BF16)
    qg = jnp.tile(q_norm.astype(F32), N_HEADS).reshape(1, Q_DIM)
    kg = jnp.tile(k_norm.astype(F32), N_KV_HEADS).reshape(1, KV_DIM)
    full = lambda shape: pl.BlockSpec(shape, lambda i: (0,) * len(shape))
    row = lambda c: pl.BlockSpec((tm, c), lambda i: (i, 0))
    return pl.pallas_call(
        _proj_kernel,
        grid=(t // tm,),
        in_specs=[row(D_MODEL), full((1, D_MODEL)), full((D_MODEL, in_dim)), full((1, 2 * D_MODEL)),
                  full((1, Q_DIM)), full((1, KV_DIM)), full((Q_DIM, Q_DIM)), full((KV_DIM, KV_DIM)),
                  full((KV_DIM, Q_DIM))],
        out_specs=[row(CONV_DIM), row(CONV_DIM), row(Q_DIM), row(Q_DIM), row(Q_DIM), row(D_MODEL), row(D_MODEL)],
        out_shape=[jax.ShapeDtypeStruct((t, CONV_DIM), F32), jax.ShapeDtypeStruct((t, CONV_DIM), F32),
                   jax.ShapeDtypeStruct((t, Q_DIM), BF16), jax.ShapeDtypeStruct((t, Q_DIM), BF16),
                   jax.ShapeDtypeStruct((t, Q_DIM), BF16), jax.ShapeDtypeStruct((t, D_MODEL), BF16),
                   jax.ShapeDtypeStruct((t, D_MODEL), BF16)],
        compiler_params=pltpu.CompilerParams(dimension_semantics=("arbitrary",),
                                             vmem_limit_bytes=56 * 1024 * 1024),
        name="peer_block_proj",
    )(x2, norm_mix.reshape(1, D_MODEL).astype(F32), w_in.astype(BF16), b_gate.reshape(1, 2 * D_MODEL).astype(F32),
      qg, kg, avq, avk, expand)


def _mixer_kernel(x_ref, hc_ref, hcp_ref, bc_ref, qn_ref, kx_ref, kxp_ref, vx_ref, vxp_ref, gc_ref, ga_ref,
                  wc_ref, bias_ref, sink_ref, wco_ref, wao_ref, wo_ref, x1_ref, att_ref, *, ts):
    j = pl.program_id(1)
    first = j == 0

    hc = hc_ref[...]
    prev = jnp.where(first, 0.0, hcp_ref[...])
    rows = lax.broadcasted_iota(jnp.int32, hc.shape, 0)
    s1 = pltpu.roll(hc, 1, axis=0)
    s1 = jnp.where(rows == 0, prev[SUBLANES - 1:SUBLANES, :], s1)
    s2 = pltpu.roll(hc, 2, axis=0)
    s2 = jnp.where(rows == 0, prev[SUBLANES - 2:SUBLANES - 1, :], s2)
    s2 = jnp.where(rows == 1, prev[SUBLANES - 1:SUBLANES, :], s2)
    wc = wc_ref[...]
    conv = s2 * wc[0:1, :] + s1 * wc[1:2, :] + hc * wc[2:3, :]
    yc = (bc_ref[...] * conv).astype(BF16)
    y_conv = jnp.dot(yc, wco_ref[...], preferred_element_type=F32)

    kfull = jnp.concatenate([kxp_ref[...], kx_ref[...]], axis=0)
    vfull = jnp.concatenate([vxp_ref[...], vx_ref[...]], axis=0)
    lane_head = lax.broadcasted_iota(jnp.int32, (BLOCK, Q_DIM), 1) // HEAD_DIM
    col = lax.broadcasted_iota(jnp.int32, (BLOCK, 2 * BLOCK), 1)
    pen0 = jnp.where(jnp.logical_and(first, col < BLOCK), NEG_BIG, 0.0)
    for r in range(ts // BLOCK):
        qb = qn_ref[r * BLOCK:(r + 1) * BLOCK, :]
        k2 = kfull[r * BLOCK:(r + 2) * BLOCK, :]
        v2 = vfull[r * BLOCK:(r + 2) * BLOCK, :]
        o = jnp.zeros((BLOCK, Q_DIM), F32)
        for hd in range(N_HEADS):
            hm = lane_head == hd
            qh = jnp.where(hm, qb, jnp.zeros_like(qb))
            lg = lax.dot_general(qh, k2, _NT, preferred_element_type=F32) + bias_ref[hd]
            if r == 0:
                lg = lg + pen0
            sink = sink_ref[hd]
            m = jnp.maximum(jnp.max(lg, axis=-1, keepdims=True), sink)
            p = jnp.exp(lg - m)
            denom = jnp.sum(p, axis=-1, keepdims=True) + jnp.exp(sink - m)
            pv = jnp.dot(p.astype(BF16), v2, preferred_element_type=F32)
            o = o + jnp.where(hm, pv / denom, 0.0)
        att_ref[r * BLOCK:(r + 1) * BLOCK, :] = o.astype(BF16)
    y_attn = jnp.dot(att_ref[...], wao_ref[...], preferred_element_type=F32)

    mixed = (gc_ref[...].astype(F32) * y_conv + ga_ref[...].astype(F32) * y_attn).astype(BF16)
    x1_ref[...] = x_ref[...] + jnp.dot(mixed, wo_ref[...], preferred_element_type=F32)


def _mixer_call(x2, hc, bc, qn, kx, vx, gc, ga, w_conv, biasm, sinks, w_conv_out, w_attn_out, w_out, bsz, s, ts):
    t = bsz * s
    nj = s // ts
    row = lambda c: pl.BlockSpec((ts, c), lambda b, j: (b * nj + j, 0))
    prev_blk = lambda c: pl.BlockSpec((BLOCK, c), lambda b, j: (jnp.maximum((b * nj + j) * (ts // BLOCK) - 1, 0), 0))
    prev8 = pl.BlockSpec((SUBLANES, CONV_DIM),
                         lambda b, j: (jnp.maximum((b * nj + j) * (ts // SUBLANES) - 1, 0), 0))
    full = lambda shape: pl.BlockSpec(shape, lambda b, j: (0,) * len(shape))
    return pl.pallas_call(
        functools.partial(_mixer_kernel, ts=ts),
        grid=(bsz, nj),
        in_specs=[row(D_MODEL), row(CONV_DIM), prev8, row(CONV_DIM), row(Q_DIM),
                  row(Q_DIM), prev_blk(Q_DIM), row(Q_DIM), prev_blk(Q_DIM), row(D_MODEL), row(D_MODEL),
                  full((CONV_K, CONV_DIM)), full((N_HEADS, BLOCK, 2 * BLOCK)),
                  pl.BlockSpec(memory_space=pltpu.SMEM),
                  full((CONV_DIM, D_MODEL)), full((Q_DIM, D_MODEL)), full((D_MODEL, D_MODEL))],
        out_specs=row(D_MODEL),
        out_shape=jax.ShapeDtypeStruct((t, D_MODEL), F32),
        scratch_shapes=[pltpu.VMEM((ts, Q_DIM), BF16)],
        compiler_params=pltpu.CompilerParams(dimension_semantics=("arbitrary", "arbitrary"),
                                             vmem_limit_bytes=56 * 1024 * 1024),
        name="peer_block_mixer",
    )(x2, hc, hc, bc, qn, kx, kx, vx, vx, gc, ga, w_conv.astype(F32), biasm, sinks.astype(F32),
      w_conv_out.astype(BF16), w_attn_out.astype(BF16), w_out.astype(BF16))


def _band_bias(rel_bias):
    q_loc = jnp.arange(BLOCK, dtype=jnp.int32)[:, None]
    k_loc = jnp.arange(2 * BLOCK, dtype=jnp.int32)[None, :]
    dist = q_loc + BLOCK - k_loc
    max_exact = N_BUCKETS // 2
    d = jnp.maximum(dist, 0)
    df = jnp.maximum(d, 1).astype(F32)
    large = max_exact + (jnp.log(df / max_exact) / math.log(MAX_DISTANCE / max_exact)
                         * (N_BUCKETS - max_exact)).astype(jnp.int32)
    large = jnp.minimum(large, N_BUCKETS - 1)
    bucket = jnp.where(d < max_exact, d, large)
    bias = jnp.transpose(jnp.take(rel_bias.astype(F32), bucket, axis=0), (2, 0, 1))
    valid = (dist >= 0) & (dist < WINDOW)
    return jnp.where(valid[None], bias, NEG_BIG)


def _top16(vals, n_rows):
    shape = vals.shape
    rid = lax.broadcasted_iota(jnp.int32, shape, 0).astype(F32)
    tv, ti = [], []
    for _ in range(PEER_TOPK):
        m = jnp.max(vals, axis=0, keepdims=True)
        idx = jnp.min(jnp.where(vals == m, rid, float(n_rows)), axis=0, keepdims=True)
        tv.append(m)
        ti.append(idx)
        vals = jnp.where(rid == idx, -jnp.inf, vals)
    return jnp.concatenate(tv, axis=0), jnp.concatenate(ti, axis=0)


def _pick(table, sel):
    out = jnp.zeros_like(sel)
    for a in range(PEER_TOPK):
        out = jnp.where(sel == float(a), table[a:a + 1, :], out)
    return out


def _route_kernel(x1_ref, g_ref, wqh_ref, wql_ref, kh_ref, kl_ref,
                  h2_ref, i0_ref, i1_ref, gt_ref, i0s, i1s, gs, *, tr):
    x = x1_ref[...]
    ms = jnp.mean(x * x, axis=-1, keepdims=True)
    h2 = (x * lax.rsqrt(ms + EPS)) * g_ref[...]
    h2_ref[...] = h2.astype(BF16)
    hh, hl = _split_bf16(h2)
    q = (jnp.dot(hh, wqh_ref[...], preferred_element_type=F32)
         + jnp.dot(hh, wql_ref[...], preferred_element_type=F32)
         + jnp.dot(hl, wqh_ref[...], preferred_element_type=F32))

    for hd in range(PEER_HEADS):
        tops = []
        for part in range(2):
            hp = hd * 2 + part
            qhp = q[:, hp * PEER_HALF:(hp + 1) * PEER_HALF]
            qh_, ql_ = _split_bf16(qhp)
            kh, kl = kh_ref[hp], kl_ref[hp]
            st = (lax.dot_general(kh, qh_, _NT, preferred_element_type=F32)
                  + lax.dot_general(kh, ql_, _NT, preferred_element_type=F32)
                  + lax.dot_general(kl, qh_, _NT, preferred_element_type=F32))
            parts = [_top16(st[:, c * LANES:(c + 1) * LANES], N_KEYS) for c in range(tr // LANES)]
            tops.append((jnp.concatenate([p[0] for p in parts], axis=1),
                         jnp.concatenate([p[1] for p in parts], axis=1)))
        (v0, j0), (v1, j1) = tops
        for c in range(tr // LANES):
            sl = slice(c * LANES, (c + 1) * LANES)
            v0c, v1c = v0[:, sl], v1[:, sl]
            cand = jnp.concatenate([v0c[a:a + 1, :] + v1c for a in range(PEER_TOPK)], axis=0)
            ts_, pos = _top16(cand, PEER_TOPK * PEER_TOPK)
            pa = jnp.floor(pos * (1.0 / PEER_TOPK))
            pb = pos - pa * PEER_TOPK
            e = jnp.exp(ts_ - ts_[0:1, :])
            gate = e / jnp.sum(e, axis=0, keepdims=True)
            rs = slice(hd * PEER_TOPK, (hd + 1) * PEER_TOPK)
            i0s[rs, sl] = _pick(j0[:, sl], pa)
            i1s[rs, sl] = _pick(j1[:, sl], pb)
            gs[rs, sl] = gate
    i0_ref[...] = i0s[...].T
    i1_ref[...] = i1s[...].T
    gt_ref[...] = gs[...].T


def _route_call(x1, norm_ffn, w_query, sub_keys, tr):
    t = x1.shape[0]
    qd = w_query.shape[1]
    wqh, wql = _split_bf16(w_query.astype(F32))
    keys = sub_keys.astype(F32).reshape(PEER_HEADS * 2, N_KEYS, PEER_HALF)
    kh, kl = _split_bf16(keys)
    full = lambda shape: pl.BlockSpec(shape, lambda i: (0,) * len(shape))
    row = lambda c: pl.BlockSpec((tr, c), lambda i: (i, 0))
    return pl.pallas_call(
        functools.partial(_route_kernel, tr=tr),
        grid=(t // tr,),
        in_specs=[row(D_MODEL), full((1, D_MODEL)), full((D_MODEL, qd)), full((D_MODEL, qd)),
                  full((PEER_HEADS * 2, N_KEYS, PEER_HALF)), full((PEER_HEADS * 2, N_KEYS, PEER_HALF))],
        out_specs=[row(D_MODEL), row(N_SEL), row(N_SEL), row(N_SEL)],
        out_shape=[jax.ShapeDtypeStruct((t, D_MODEL), BF16), jax.ShapeDtypeStruct((t, N_SEL), F32),
                   jax.ShapeDtypeStruct((t, N_SEL), F32), jax.ShapeDtypeStruct((t, N_SEL), F32)],
        scratch_shapes=[pltpu.VMEM((N_SEL, tr), F32)] * 3,
        compiler_params=pltpu.CompilerParams(dimension_semantics=("arbitrary",),
                                             vmem_limit_bytes=56 * 1024 * 1024),
        name="peer_block_route",
    )(x1, norm_ffn.reshape(1, D_MODEL).astype(F32), wqh, wql, kh, kl)


def _expert_kernel(x1_ref, h2_ref, i0_ref, i1_ref, gt_ref, u_ref, v_ref, out_ref, gd_ref, acc_ref, *, tb, ec):
    c = pl.program_id(1)
    ngrp = ec // N_KEYS

    @pl.when(c == 0)
    def _build_gate_matrix():
        rid = lax.broadcasted_iota(jnp.int32, (N_KEYS, N_SEL), 0).astype(F32)

        def body(t, carry):
            i0r = i0_ref[pl.ds(t, 1), :]
            i1r = i1_ref[pl.ds(t, 1), :]
            gr = gt_ref[pl.ds(t, 1), :]
            rt = jnp.where(rid == i0r, gr, 0.0)
            rh, rl = _split_bf16(rt)
            ct = jnp.where(rid == i1r, 1.0, 0.0).astype(BF16)
            gd = (lax.dot_general(rh, ct, _NT, preferred_element_type=F32)
                  + lax.dot_general(rl, ct, _NT, preferred_element_type=F32))
            gd_ref[t // SUBLANES, pl.ds(t % SUBLANES, N_KEYS, stride=SUBLANES), :] = gd
            return carry

        lax.fori_loop(0, tb, body, 0)
        acc_ref[...] = jnp.zeros_like(acc_ref)

    a = lax.dot_general(h2_ref[...], u_ref[...], _NT, preferred_element_type=F32)
    base = pl.multiple_of(c * (ngrp * SUBLANES), SUBLANES)
    gd = jnp.concatenate(
        [gd_ref[:, pl.ds(base + g * SUBLANES, SUBLANES), :].reshape(tb, N_KEYS) for g in range(ngrp)], axis=1)
    gelu = 0.5 * a * (1.0 + lax.erf(a * (2.0 ** -0.5)))
    w = (gd * gelu).astype(BF16)
    acc_ref[...] += jnp.dot(w, v_ref[...], preferred_element_type=F32)

    @pl.when(c == pl.num_programs(1) - 1)
    def _finish():
        out_ref[...] = x1_ref[...] + acc_ref[...]


def _expert_call(x1, h2, i0, i1, gt, expert_u, expert_v, tb, ec):
    t = x1.shape[0]
    row = lambda c_: pl.BlockSpec((tb, c_), lambda i, c: (i, 0))
    chunk = pl.BlockSpec((ec, D_MODEL), lambda i, c: (c, 0))
    return pl.pallas_call(
        functools.partial(_expert_kernel, tb=tb, ec=ec),
        grid=(t // tb, N_EXPERTS // ec),
        in_specs=[row(D_MODEL), row(D_MODEL), row(N_SEL), row(N_SEL), row(N_SEL), chunk, chunk],
        out_specs=row(D_MODEL),
        out_shape=jax.ShapeDtypeStruct((t, D_MODEL), F32),
        scratch_shapes=[pltpu.VMEM((tb // SUBLANES, N_KEYS * SUBLANES, N_KEYS), F32),
                        pltpu.VMEM((tb, D_MODEL), F32)],
        compiler_params=pltpu.CompilerParams(dimension_semantics=("arbitrary", "arbitrary"),
                                             vmem_limit_bytes=56 * 1024 * 1024),
        name="peer_block_experts",
    )(x1, h2, i0, i1, gt, expert_u.astype(BF16), expert_v.astype(BF16))


def _tile_sizes(bsz, s):
    t = bsz * s
    tm = math.gcd(t, 512)
    ts = math.gcd(s, 512)
    tr = math.gcd(t, 256)
    tb = math.gcd(t, 256)
    return tm, ts, tr, tb


def kernel(x, norm_mix, norm_ffn, w_in, b_gate, w_conv, q_norm, k_norm, sinks, rel_bias, w_conv_out, w_attn_out,
           w_out, w_query, sub_keys, expert_u, expert_v):
    bsz, s, d = x.shape
    assert d == D_MODEL and s % BLOCK == 0
    tm, ts, tr, tb = _tile_sizes(bsz, s)
    biasm = _band_bias(rel_bias)
    x2 = x.reshape(bsz * s, d)
    for l in range(norm_mix.shape[0]):
        hc, bc, qn, kx, vx, gc, ga = _proj_call(x2, norm_mix[l], w_in[l], b_gate[l], q_norm[l], k_norm[l], tm)
        x1 = _mixer_call(x2, hc, bc, qn, kx, vx, gc, ga, w_conv[l], biasm, sinks[l], w_conv_out[l],
                         w_attn_out[l], w_out[l], bsz, s, ts)
        h2, i0, i1, gt = _route_call(x1, norm_ffn[l], w_query[l], sub_keys[l], tr)
        x2 = _expert_call(x1, h2, i0, i1, gt, expert_u[l], expert_v[l], tb, 512)
    return x2.reshape(bsz, s, d)
```

```python
import functools
import math

import jax
import jax.numpy as jnp
import numpy as np
from jax import lax
from jax.experimental import pallas as pl
from jax.experimental.pallas import tpu as pltpu

F32 = jnp.float32
BF16 = jnp.bfloat16

D_MODEL = 1024
CONV_DIM = 512
CONV_K = 3
N_HEADS = 8
N_KV_HEADS = 2
HEAD_DIM = 64
Q_DIM = N_HEADS * HEAD_DIM
KV_DIM = N_KV_HEADS * HEAD_DIM
GROUP = N_HEADS // N_KV_HEADS
WINDOW = 128
BLOCK = 128
N_BUCKETS = 32
MAX_DISTANCE = 128
PEER_HEADS = 8
N_KEYS = 128
N_EXPERTS = N_KEYS * N_KEYS
PEER_QDIM = 256
PEER_HALF = PEER_QDIM // 2
PEER_TOPK = 16
N_SEL = PEER_HEADS * PEER_TOPK
EPS = 1e-6
NEG_BIG = -1e30
EXPERT_CHUNK = 1024
GATE_GROUPS_PER_ITER = 2

SUBLANES = 8
LANES = 128

_OFFS = np.cumsum([0, CONV_DIM, CONV_DIM, CONV_DIM, Q_DIM, KV_DIM, KV_DIM, D_MODEL, D_MODEL]).tolist()

_NT = (((1,), (1,)), ((), ()))


def _split_bf16(a):
    hi = a.astype(BF16)
    lo = (a - hi.astype(F32)).astype(BF16)
    return hi, lo


def _proj_kernel(x_ref, g_ref, w_ref, bg_ref, qg_ref, kg_ref, avq_ref, avk_ref, exp_ref,
                 hc_ref, bc_ref, qn_ref, kx_ref, vx_ref, gc_ref, ga_ref):
    x = x_ref[...]
    ms = jnp.mean(x * x, axis=-1, keepdims=True)
    h = ((x * lax.rsqrt(ms + EPS)) * g_ref[...]).astype(BF16)

    def seg(i):
        return jnp.dot(h, w_ref[:, _OFFS[i]:_OFFS[i + 1]], preferred_element_type=F32)

    u = seg(0)
    hc_ref[...] = seg(2) * u
    bc_ref[...] = seg(1)

    def head_rms(a, av_ref):
        hi, lo = _split_bf16(a * a)
        return (jnp.dot(hi, av_ref[...], preferred_element_type=F32)
                + jnp.dot(lo, av_ref[...], preferred_element_type=F32))

    q = seg(3)
    qn = (q * lax.rsqrt(head_rms(q, avq_ref) + EPS)) * qg_ref[...] * (HEAD_DIM ** -0.5)
    qn_ref[...] = qn.astype(BF16)

    k = seg(4)
    kn = ((k * lax.rsqrt(head_rms(k, avk_ref) + EPS)) * kg_ref[...]).astype(BF16)
    kx_ref[...] = jnp.dot(kn, exp_ref[...], preferred_element_type=F32).astype(BF16)
    v = seg(5).astype(BF16)
    vx_ref[...] = jnp.dot(v, exp_ref[...], preferred_element_type=F32).astype(BF16)

    bg = bg_ref[...]
    gc_ref[...] = jax.nn.sigmoid(seg(6) + bg[:, :D_MODEL]).astype(BF16)
    ga_ref[...] = jax.nn.sigmoid(seg(7) + bg[:, D_MODEL:]).astype(BF16)


def _proj_call(x2, norm_mix, w_in, b_gate, q_norm, k_norm, tm):
    t = x2.shape[0]
    in_dim = w_in.shape[1]
    avq = jnp.kron(jnp.eye(N_HEADS, dtype=F32), jnp.full((HEAD_DIM, HEAD_DIM), 1.0 / HEAD_DIM, F32)).astype(BF16)
    avk = jnp.kron(jnp.eye(N_KV_HEADS, dtype=F32), jnp.full((HEAD_DIM, HEAD_DIM), 1.0 / HEAD_DIM, F32)).astype(BF16)
    expand = jnp.kron(jnp.kron(jnp.eye(N_KV_HEADS, dtype=F32), jnp.ones((1, GROUP), F32)),
                      jnp.eye(HEAD_DIM, dtype=F32)).astype(BF16)
    qg = jnp.tile(q_norm.astype(F32), N_HEADS).reshape(1, Q_DIM)
    kg = jnp.tile(k_norm.astype(F32), N_KV_HEADS).reshape(1, KV_DIM)
    full = lambda shape: pl.BlockSpec(shape, lambda i: (0,) * len(shape))
    row = lambda c: pl.BlockSpec((tm, c), lambda i: (i, 0))
    return pl.pallas_call(
        _proj_kernel,
        grid=(t // tm,),
        in_specs=[row(D_MODEL), full((1, D_MODEL)), full((D_MODEL, in_dim)), full((1, 2 * D_MODEL)),
                  full((1, Q_DIM)), full((1, KV_DIM)), full((Q_DIM, Q_DIM)), full((KV_DIM, KV_DIM)),
                  full((KV_DIM, Q_DIM))],
        out_specs=[row(CONV_DIM), row(CONV_DIM), row(Q_DIM), row(Q_DIM), row(Q_DIM), row(D_MODEL), row(D_MODEL)],
        out_shape=[jax.ShapeDtypeStruct((t, CONV_DIM), F32), jax.ShapeDtypeStruct((t, CONV_DIM), F32),
                   jax.ShapeDtypeStruct((t, Q_DIM), BF16), jax.ShapeDtypeStruct((t, Q_DIM), BF16),
                   jax.ShapeDtypeStruct((t, Q_DIM), BF16), jax.ShapeDtypeStruct((t, D_MODEL), BF16),
                   jax.ShapeDtypeStruct((t, D_MODEL), BF16)],
        compiler_params=pltpu.CompilerParams(dimension_semantics=("arbitrary",),
                                             vmem_limit_bytes=56 * 1024 * 1024),
        name="peer_block_proj",
    )(x2, norm_mix.reshape(1, D_MODEL).astype(F32), w_in.astype(BF16), b_gate.reshape(1, 2 * D_MODEL).astype(F32),
      qg, kg, avq, avk, expand)


def _mixer_kernel(x_ref, hc_ref, hcp_ref, bc_ref, qn_ref, kx_ref, kxp_ref, vx_ref, vxp_ref, gc_ref, ga_ref,
                  wc_ref, bias_ref, sink_ref, wco_ref, wao_ref, wo_ref, x1_ref, att_ref, *, ts):
    j = pl.program_id(1)
    first = j == 0

    hc = hc_ref[...]
    prev = jnp.where(first, 0.0, hcp_ref[...])
    rows = lax.broadcasted_iota(jnp.int32, hc.shape, 0)
    s1 = pltpu.roll(hc, 1, axis=0)
    s1 = jnp.where(rows == 0, prev[SUBLANES - 1:SUBLANES, :], s1)
    s2 = pltpu.roll(hc, 2, axis=0)
    s2 = jnp.where(rows == 0, prev[SUBLANES - 2:SUBLANES - 1, :], s2)
    s2 = jnp.where(rows == 1, prev[SUBLANES - 1:SUBLANES, :], s2)
    wc = wc_ref[...]
    conv = s2 * wc[0:1, :] + s1 * wc[1:2, :] + hc * wc[2:3, :]
    yc = (bc_ref[...] * conv).astype(BF16)
    y_conv = jnp.dot(yc, wco_ref[...], preferred_element_type=F32)

    kfull = jnp.concatenate([kxp_ref[...], kx_ref[...]], axis=0)
    vfull = jnp.concatenate([vxp_ref[...], vx_ref[...]], axis=0)
    lane_head = lax.broadcasted_iota(jnp.int32, (BLOCK, Q_DIM), 1) // HEAD_DIM
    col = lax.broadcasted_iota(jnp.int32, (BLOCK, 2 * BLOCK), 1)
    pen0 = jnp.where(jnp.logical_and(first, col < BLOCK), NEG_BIG, 0.0)
    for r in range(ts // BLOCK):
        qb = qn_ref[r * BLOCK:(r + 1) * BLOCK, :]
        k2 = kfull[r * BLOCK:(r + 2) * BLOCK, :]
        v2 = vfull[r * BLOCK:(r + 2) * BLOCK, :]
        o = jnp.zeros((BLOCK, Q_DIM), F32)
        for hd in range(N_HEADS):
            hm = lane_head == hd
            qh = jnp.where(hm, qb, jnp.zeros_like(qb))
            lg = lax.dot_general(qh, k2, _NT, preferred_element_type=F32) + bias_ref[hd]
            if r == 0:
                lg = lg + pen0
            sink = sink_ref[hd]
            m = jnp.maximum(jnp.max(lg, axis=-1, keepdims=True), sink)
            p = jnp.exp(lg - m)
            denom = jnp.sum(p, axis=-1, keepdims=True) + jnp.exp(sink - m)
            pv = jnp.dot(p.astype(BF16), v2, preferred_element_type=F32)
            o = o + jnp.where(hm, pv / denom, 0.0)
        att_ref[r * BLOCK:(r + 1) * BLOCK, :] = o.astype(BF16)
    y_attn = jnp.dot(att_ref[...], wao_ref[...], preferred_element_type=F32)

    mixed = (gc_ref[...].astype(F32) * y_conv + ga_ref[...].astype(F32) * y_attn).astype(BF16)
    x1_ref[...] = x_ref[...] + jnp.dot(mixed, wo_ref[...], preferred_element_type=F32)


def _mixer_call(x2, hc, bc, qn, kx, vx, gc, ga, w_conv, biasm, sinks, w_conv_out, w_attn_out, w_out, bsz, s, ts):
    t = bsz * s
    nj = s // ts
    row = lambda c: pl.BlockSpec((ts, c), lambda b, j: (b * nj + j, 0))
    prev_blk = lambda c: pl.BlockSpec((BLOCK, c), lambda b, j: (jnp.maximum((b * nj + j) * (ts // BLOCK) - 1, 0), 0))
    prev8 = pl.BlockSpec((SUBLANES, CONV_DIM),
                         lambda b, j: (jnp.maximum((b * nj + j) * (ts // SUBLANES) - 1, 0), 0))
    full = lambda shape: pl.BlockSpec(shape, lambda b, j: (0,) * len(shape))
    return pl.pallas_call(
        functools.partial(_mixer_kernel, ts=ts),
        grid=(bsz, nj),
        in_specs=[row(D_MODEL), row(CONV_DIM), prev8, row(CONV_DIM), row(Q_DIM),
                  row(Q_DIM), prev_blk(Q_DIM), row(Q_DIM), prev_blk(Q_DIM), row(D_MODEL), row(D_MODEL),
                  full((CONV_K, CONV_DIM)), full((N_HEADS, BLOCK, 2 * BLOCK)),
                  pl.BlockSpec(memory_space=pltpu.SMEM),
                  full((CONV_DIM, D_MODEL)), full((Q_DIM, D_MODEL)), full((D_MODEL, D_MODEL))],
        out_specs=row(D_MODEL),
        out_shape=jax.ShapeDtypeStruct((t, D_MODEL), F32),
        scratch_shapes=[pltpu.VMEM((ts, Q_DIM), BF16)],
        compiler_params=pltpu.CompilerParams(dimension_semantics=("arbitrary", "arbitrary"),
                                             vmem_limit_bytes=56 * 1024 * 1024),
        name="peer_block_mixer",
    )(x2, hc, hc, bc, qn, kx, kx, vx, vx, gc, ga, w_conv.astype(F32), biasm, sinks.astype(F32),
      w_conv_out.astype(BF16), w_attn_out.astype(BF16), w_out.astype(BF16))


def _band_bias(rel_bias):
    q_loc = jnp.arange(BLOCK, dtype=jnp.int32)[:, None]
    k_loc = jnp.arange(2 * BLOCK, dtype=jnp.int32)[None, :]
    dist = q_loc + BLOCK - k_loc
    max_exact = N_BUCKETS // 2
    d = jnp.maximum(dist, 0)
    df = jnp.maximum(d, 1).astype(F32)
    large = max_exact + (jnp.log(df / max_exact) / math.log(MAX_DISTANCE / max_exact)
                         * (N_BUCKETS - max_exact)).astype(jnp.int32)
    large = jnp.minimum(large, N_BUCKETS - 1)
    bucket = jnp.where(d < max_exact, d, large)
    onehot = (bucket[None] == jnp.arange(N_BUCKETS, dtype=jnp.int32)[:, None, None]).astype(F32)
    bias = jnp.einsum("bh,bqk->hqk", rel_bias.astype(F32), onehot, precision=lax.Precision.HIGHEST)
    valid = (dist >= 0) & (dist < WINDOW)
    return jnp.where(valid[None], bias, NEG_BIG)


def _top16(vals, n_rows):
    shape = vals.shape
    rid = lax.broadcasted_iota(jnp.int32, shape, 0).astype(F32)
    tv, ti = [], []
    for _ in range(PEER_TOPK):
        m = jnp.max(vals, axis=0, keepdims=True)
        idx = jnp.min(jnp.where(vals == m, rid, float(n_rows)), axis=0, keepdims=True)
        tv.append(m)
        ti.append(idx)
        vals = jnp.where(rid == idx, -jnp.inf, vals)
    return jnp.concatenate(tv, axis=0), jnp.concatenate(ti, axis=0)


def _pick(table, sel):
    out = jnp.zeros_like(sel)
    for a in range(PEER_TOPK):
        out = jnp.where(sel == float(a), table[a:a + 1, :], out)
    return out


def _route_kernel(x1_ref, g_ref, wqh_ref, wql_ref, kh_ref, kl_ref,
                  h2_ref, i0_ref, i1_ref, gt_ref, i0s, i1s, gs, *, tr):
    x = x1_ref[...]
    ms = jnp.mean(x * x, axis=-1, keepdims=True)
    h2 = (x * lax.rsqrt(ms + EPS)) * g_ref[...]
    h2_ref[...] = h2.astype(BF16)
    hh, hl = _split_bf16(h2)
    q = (jnp.dot(hh, wqh_ref[...], preferred_element_type=F32)
         + jnp.dot(hh, wql_ref[...], preferred_element_type=F32)
         + jnp.dot(hl, wqh_ref[...], preferred_element_type=F32))

    for hd in range(PEER_HEADS):
        tops = []
        for part in range(2):
            hp = hd * 2 + part
            qhp = q[:, hp * PEER_HALF:(hp + 1) * PEER_HALF]
            qh_, ql_ = _split_bf16(qhp)
            kh, kl = kh_ref[hp], kl_ref[hp]
            st = (lax.dot_general(kh, qh_, _NT, preferred_element_type=F32)
                  + lax.dot_general(kh, ql_, _NT, preferred_element_type=F32)
                  + lax.dot_general(kl, qh_, _NT, preferred_element_type=F32))
            parts = [_top16(st[:, c * LANES:(c + 1) * LANES], N_KEYS) for c in range(tr // LANES)]
            tops.append((jnp.concatenate([p[0] for p in parts], axis=1),
                         jnp.concatenate([p[1] for p in parts], axis=1)))
        (v0, j0), (v1, j1) = tops
        for c in range(tr // LANES):
            sl = slice(c * LANES, (c + 1) * LANES)
            v0c, v1c = v0[:, sl], v1[:, sl]
            cand = jnp.concatenate([v0c[a:a + 1, :] + v1c for a in range(PEER_TOPK)], axis=0)
            ts_, pos = _top16(cand, PEER_TOPK * PEER_TOPK)
            pa = jnp.floor(pos * (1.0 / PEER_TOPK))
            pb = pos - pa * PEER_TOPK
            e = jnp.exp(ts_ - ts_[0:1, :])
            gate = e / jnp.sum(e, axis=0, keepdims=True)
            rs = slice(hd * PEER_TOPK, (hd + 1) * PEER_TOPK)
            i0s[rs, sl] = _pick(j0[:, sl], pa)
            i1s[rs, sl] = _pick(j1[:, sl], pb)
            gs[rs, sl] = gate
    i0_ref[...] = i0s[...].T
    i1_ref[...] = i1s[...].T
    gt_ref[...] = gs[...].T


def _route_call(x1, norm_ffn, w_query, sub_keys, tr):
    t = x1.shape[0]
    qd = w_query.shape[1]
    wqh, wql = _split_bf16(w_query.astype(F32))
    keys = sub_keys.astype(F32).reshape(PEER_HEADS * 2, N_KEYS, PEER_HALF)
    kh, kl = _split_bf16(keys)
    full = lambda shape: pl.BlockSpec(shape, lambda i: (0,) * len(shape))
    row = lambda c: pl.BlockSpec((tr, c), lambda i: (i, 0))
    return pl.pallas_call(
        functools.partial(_route_kernel, tr=tr),
        grid=(t // tr,),
        in_specs=[row(D_MODEL), full((1, D_MODEL)), full((D_MODEL, qd)), full((D_MODEL, qd)),
                  full((PEER_HEADS * 2, N_KEYS, PEER_HALF)), full((PEER_HEADS * 2, N_KEYS, PEER_HALF))],
        out_specs=[row(D_MODEL), row(N_SEL), row(N_SEL), row(N_SEL)],
        out_shape=[jax.ShapeDtypeStruct((t, D_MODEL), BF16), jax.ShapeDtypeStruct((t, N_SEL), F32),
                   jax.ShapeDtypeStruct((t, N_SEL), F32), jax.ShapeDtypeStruct((t, N_SEL), F32)],
        scratch_shapes=[pltpu.VMEM((N_SEL, tr), F32)] * 3,
        compiler_params=pltpu.CompilerParams(dimension_semantics=("arbitrary",),
                                             vmem_limit_bytes=56 * 1024 * 1024),
        name="peer_block_route",
    )(x1, norm_ffn.reshape(1, D_MODEL).astype(F32), wqh, wql, kh, kl)


def _gate_tiles(gd_ref, tg, i0b, i1b, gh, gl, rid):
    zero = jnp.zeros_like(rid)
    one = jnp.ones_like(rid)
    for s in range(SUBLANES):
        m0 = rid == i0b[s:s + 1, :]
        rh = jnp.where(m0, gh[s:s + 1, :], zero)
        rl = jnp.where(m0, gl[s:s + 1, :], zero)
        ct = jnp.where(rid == i1b[s:s + 1, :], one, zero)
        gd = (lax.dot_general(rh, ct, _NT, preferred_element_type=F32)
              + lax.dot_general(rl, ct, _NT, preferred_element_type=F32))
        gd_ref[tg, pl.ds(s, N_KEYS, stride=SUBLANES), :] = gd


def _expert_kernel(x1_ref, h2_ref, i0_ref, i1_ref, gt_ref, u_ref, v_ref, out_ref, gd_ref, *, tb, ec):
    c = pl.program_id(1)
    ngrp = ec // N_KEYS

    @pl.when(c == 0)
    def _build_gate_matrix():
        rid = lax.broadcasted_iota(jnp.int32, (N_KEYS, N_SEL), 0).astype(F32).astype(BF16)

        def body(it, carry):
            for k in range(GATE_GROUPS_PER_ITER):
                tg = it * GATE_GROUPS_PER_ITER + k
                r0 = pl.multiple_of(tg * SUBLANES, SUBLANES)
                gh, gl = _split_bf16(gt_ref[pl.ds(r0, SUBLANES), :])
                _gate_tiles(gd_ref, tg, i0_ref[pl.ds(r0, SUBLANES), :].astype(BF16),
                            i1_ref[pl.ds(r0, SUBLANES), :].astype(BF16), gh, gl, rid)
            return carry

        lax.fori_loop(0, tb // (SUBLANES * GATE_GROUPS_PER_ITER), body, 0)
        out_ref[...] = x1_ref[...]

    a = lax.dot_general(h2_ref[...], u_ref[...], _NT, preferred_element_type=F32)
    base = pl.multiple_of(c * (ngrp * SUBLANES), SUBLANES)
    gd = jnp.concatenate(
        [gd_ref[:, pl.ds(base + g * SUBLANES, SUBLANES), :].reshape(tb, N_KEYS) for g in range(ngrp)], axis=1)
    gelu = 0.5 * a * (1.0 + lax.erf(a * (2.0 ** -0.5)))
    w = (gd * gelu).astype(BF16)
    out_ref[...] += jnp.dot(w, v_ref[...], preferred_element_type=F32)


def _expert_call(x1, h2, i0, i1, gt, expert_u, expert_v, tb, ec):
    t = x1.shape[0]
    row = lambda c_: pl.BlockSpec((tb, c_), lambda i, c: (i, 0))
    chunk = pl.BlockSpec((ec, D_MODEL), lambda i, c: (c, 0))
    return pl.pallas_call(
        functools.partial(_expert_kernel, tb=tb, ec=ec),
        grid=(t // tb, N_EXPERTS // ec),
        in_specs=[row(D_MODEL), row(D_MODEL), row(N_SEL), row(N_SEL), row(N_SEL), chunk, chunk],
        out_specs=row(D_MODEL),
        out_shape=jax.ShapeDtypeStruct((t, D_MODEL), F32),
        scratch_shapes=[pltpu.VMEM((tb // SUBLANES, N_KEYS * SUBLANES, N_KEYS), F32)],
        compiler_params=pltpu.CompilerParams(dimension_semantics=("arbitrary", "arbitrary"),
                                             vmem_limit_bytes=60 * 1024 * 1024),
        name="peer_block_experts",
    )(x1, h2, i0, i1, gt, expert_u.astype(BF16), expert_v.astype(BF16))


def _tile_sizes(bsz, s):
    t = bsz * s
    tm = math.gcd(t, 512)
    ts = math.gcd(s, 512)
    tr = math.gcd(t, 256)
    tb = math.gcd(t, 512)
    return tm, ts, tr, tb


def kernel(x, norm_mix, norm_ffn, w_in, b_gate, w_conv, q_norm, k_norm, sinks, rel_bias, w_conv_out, w_attn_out,
           w_out, w_query, sub_keys, expert_u, expert_v):
    bsz, s, d = x.shape
    assert d == D_MODEL and s % BLOCK == 0
    tm, ts, tr, tb = _tile_sizes(bsz, s)
    biasm = _band_bias(rel_bias)
    x2 = x.reshape(bsz * s, d)
    for l in range(norm_mix.shape[0]):
        hc, bc, qn, kx, vx, gc, ga = _proj_call(x2, norm_mix[l], w_in[l], b_gate[l], q_norm[l], k_norm[l], tm)
        x1 = _mixer_call(x2, hc, bc, qn, kx, vx, gc, ga, w_conv[l], biasm, sinks[l], w_conv_out[l],
                         w_attn_out[l], w_out[l], bsz, s, ts)
        h2, i0, i1, gt = _route_call(x1, norm_ffn[l], w_query[l], sub_keys[l], tr)
        x2 = _expert_call(x1, h2, i0, i1, gt, expert_u[l], expert_v[l], tb, EXPERT_CHUNK)
    return x2.reshape(bsz, s, d)
```

```python
import functools
import math

import jax
import jax.numpy as jnp
import numpy as np
from jax import lax
from jax.experimental import pallas as pl
from jax.experimental.pallas import tpu as pltpu

F32 = jnp.float32
BF16 = jnp.bfloat16

D_MODEL = 1024
CONV_DIM = 512
CONV_K = 3
N_HEADS = 8
N_KV_HEADS = 2
HEAD_DIM = 64
Q_DIM = N_HEADS * HEAD_DIM
KV_DIM = N_KV_HEADS * HEAD_DIM
GROUP = N_HEADS // N_KV_HEADS
WINDOW = 128
BLOCK = 128
N_BUCKETS = 32
MAX_DISTANCE = 128
PEER_HEADS = 8
N_KEYS = 128
N_EXPERTS = N_KEYS * N_KEYS
PEER_QDIM = 256
PEER_HALF = PEER_QDIM // 2
PEER_TOPK = 16
N_SEL = PEER_HEADS * PEER_TOPK
EPS = 1e-6
NEG_BIG = -1e30
EXPERT_CHUNK = 1024
GATE_GROUPS_PER_ITER = 2

SUBLANES = 8
LANES = 128

_OFFS = np.cumsum([0, CONV_DIM, CONV_DIM, CONV_DIM, Q_DIM, KV_DIM, KV_DIM, D_MODEL, D_MODEL]).tolist()

_NT = (((1,), (1,)), ((), ()))


def _split_bf16(a):
    hi = a.astype(BF16)
    lo = (a - hi.astype(F32)).astype(BF16)
    return hi, lo


def _proj_kernel(x_ref, g_ref, w_ref, bg_ref, qg_ref, kg_ref, avq_ref, avk_ref, exp_ref,
                 hc_ref, bc_ref, qn_ref, kx_ref, vx_ref, gc_ref, ga_ref):
    x = x_ref[...]
    ms = jnp.mean(x * x, axis=-1, keepdims=True)
    h = ((x * lax.rsqrt(ms + EPS)) * g_ref[...]).astype(BF16)

    def seg(i):
        return jnp.dot(h, w_ref[:, _OFFS[i]:_OFFS[i + 1]], preferred_element_type=F32)

    u = seg(0)
    hc_ref[...] = seg(2) * u
    bc_ref[...] = seg(1)

    def head_rms(a, av_ref):
        hi, lo = _split_bf16(a * a)
        return (jnp.dot(hi, av_ref[...], preferred_element_type=F32)
                + jnp.dot(lo, av_ref[...], preferred_element_type=F32))

    q = seg(3)
    qn = (q * lax.rsqrt(head_rms(q, avq_ref) + EPS)) * qg_ref[...] * (HEAD_DIM ** -0.5)
    qn_ref[...] = qn.astype(BF16)

    k = seg(4)
    kn = ((k * lax.rsqrt(head_rms(k, avk_ref) + EPS)) * kg_ref[...]).astype(BF16)
    kx_ref[...] = jnp.dot(kn, exp_ref[...], preferred_element_type=F32).astype(BF16)
    v = seg(5).astype(BF16)
    vx_ref[...] = jnp.dot(v, exp_ref[...], preferred_element_type=F32).astype(BF16)

    bg = bg_ref[...]
    gc_ref[...] = jax.nn.sigmoid(seg(6) + bg[:, :D_MODEL]).astype(BF16)
    ga_ref[...] = jax.nn.sigmoid(seg(7) + bg[:, D_MODEL:]).astype(BF16)


def _proj_call(x2, norm_mix, w_in, b_gate, q_norm, k_norm, tm):
    t = x2.shape[0]
    in_dim = w_in.shape[1]
    avq = jnp.kron(jnp.eye(N_HEADS, dtype=F32), jnp.full((HEAD_DIM, HEAD_DIM), 1.0 / HEAD_DIM, F32)).astype(BF16)
    avk = jnp.kron(jnp.eye(N_KV_HEADS, dtype=F32), jnp.full((HEAD_DIM, HEAD_DIM), 1.0 / HEAD_DIM, F32)).astype(BF16)
    expand = jnp.kron(jnp.kron(jnp.eye(N_KV_HEADS, dtype=F32), jnp.ones((1, GROUP), F32)),
                      jnp.eye(HEAD_DIM, dtype=F32)).astype(BF16)
    qg = jnp.tile(q_norm.astype(F32), N_HEADS).reshape(1, Q_DIM)
    kg = jnp.tile(k_norm.astype(F32), N_KV_HEADS).reshape(1, KV_DIM)
    full = lambda shape: pl.BlockSpec(shape, lambda i: (0,) * len(shape))
    row = lambda c: pl.BlockSpec((tm, c), lambda i: (i, 0))
    return pl.pallas_call(
        _proj_kernel,
        grid=(t // tm,),
        in_specs=[row(D_MODEL), full((1, D_MODEL)), full((D_MODEL, in_dim)), full((1, 2 * D_MODEL)),
                  full((1, Q_DIM)), full((1, KV_DIM)), full((Q_DIM, Q_DIM)), full((KV_DIM, KV_DIM)),
                  full((KV_DIM, Q_DIM))],
        out_specs=[row(CONV_DIM), row(CONV_DIM), row(Q_DIM), row(Q_DIM), row(Q_DIM), row(D_MODEL), row(D_MODEL)],
        out_shape=[jax.ShapeDtypeStruct((t, CONV_DIM), F32), jax.ShapeDtypeStruct((t, CONV_DIM), F32),
                   jax.ShapeDtypeStruct((t, Q_DIM), BF16), jax.ShapeDtypeStruct((t, Q_DIM), BF16),
                   jax.ShapeDtypeStruct((t, Q_DIM), BF16), jax.ShapeDtypeStruct((t, D_MODEL), BF16),
                   jax.ShapeDtypeStruct((t, D_MODEL), BF16)],
        compiler_params=pltpu.CompilerParams(dimension_semantics=("arbitrary",),
                                             vmem_limit_bytes=56 * 1024 * 1024),
        name="peer_block_proj",
    )(x2, norm_mix.reshape(1, D_MODEL).astype(F32), w_in.astype(BF16), b_gate.reshape(1, 2 * D_MODEL).astype(F32),
      qg, kg, avq, avk, expand)


def _mixer_kernel(x_ref, hc_ref, hcp_ref, bc_ref, qn_ref, kx_ref, kxp_ref, vx_ref, vxp_ref, gc_ref, ga_ref,
                  wc_ref, bias_ref, sink_ref, wco_ref, wao_ref, wo_ref, x1_ref, att_ref, *, ts):
    j = pl.program_id(1)
    first = j == 0

    hc = hc_ref[...]
    prev = jnp.where(first, 0.0, hcp_ref[...])
    rows = lax.broadcasted_iota(jnp.int32, hc.shape, 0)
    s1 = pltpu.roll(hc, 1, axis=0)
    s1 = jnp.where(rows == 0, prev[SUBLANES - 1:SUBLANES, :], s1)
    s2 = pltpu.roll(hc, 2, axis=0)
    s2 = jnp.where(rows == 0, prev[SUBLANES - 2:SUBLANES - 1, :], s2)
    s2 = jnp.where(rows == 1, prev[SUBLANES - 1:SUBLANES, :], s2)
    wc = wc_ref[...]
    conv = s2 * wc[0:1, :] + s1 * wc[1:2, :] + hc * wc[2:3, :]
    yc = (bc_ref[...] * conv).astype(BF16)
    y_conv = jnp.dot(yc, wco_ref[...], preferred_element_type=F32)

    kfull = jnp.concatenate([kxp_ref[...], kx_ref[...]], axis=0)
    vfull = jnp.concatenate([vxp_ref[...], vx_ref[...]], axis=0)
    lane_head = lax.broadcasted_iota(jnp.int32, (BLOCK, Q_DIM), 1) // HEAD_DIM
    col = lax.broadcasted_iota(jnp.int32, (BLOCK, 2 * BLOCK), 1)
    pen0 = jnp.where(jnp.logical_and(first, col < BLOCK), NEG_BIG, 0.0)
    for r in range(ts // BLOCK):
        qb = qn_ref[r * BLOCK:(r + 1) * BLOCK, :]
        k2 = kfull[r * BLOCK:(r + 2) * BLOCK, :]
        v2 = vfull[r * BLOCK:(r + 2) * BLOCK, :]
        o = jnp.zeros((BLOCK, Q_DIM), F32)
        for hd in range(N_HEADS):
            hm = lane_head == hd
            qh = jnp.where(hm, qb, jnp.zeros_like(qb))
            lg = lax.dot_general(qh, k2, _NT, preferred_element_type=F32) + bias_ref[hd]
            if r == 0:
                lg = lg + pen0
            sink = sink_ref[hd]
            m = jnp.maximum(jnp.max(lg, axis=-1, keepdims=True), sink)
            p = jnp.exp(lg - m)
            denom = jnp.sum(p, axis=-1, keepdims=True) + jnp.exp(sink - m)
            pv = jnp.dot(p.astype(BF16), v2, preferred_element_type=F32)
            o = o + jnp.where(hm, pv / denom, 0.0)
        att_ref[r * BLOCK:(r + 1) * BLOCK, :] = o.astype(BF16)
    y_attn = jnp.dot(att_ref[...], wao_ref[...], preferred_element_type=F32)

    mixed = (gc_ref[...].astype(F32) * y_conv + ga_ref[...].astype(F32) * y_attn).astype(BF16)
    x1_ref[...] = x_ref[...] + jnp.dot(mixed, wo_ref[...], preferred_element_type=F32)


def _mixer_call(x2, hc, bc, qn, kx, vx, gc, ga, w_conv, biasm, sinks, w_conv_out, w_attn_out, w_out, bsz, s, ts):
    t = bsz * s
    nj = s // ts
    row = lambda c: pl.BlockSpec((ts, c), lambda b, j: (b * nj + j, 0))
    prev_blk = lambda c: pl.BlockSpec((BLOCK, c), lambda b, j: (jnp.maximum((b * nj + j) * (ts // BLOCK) - 1, 0), 0))
    prev8 = pl.BlockSpec((SUBLANES, CONV_DIM),
                         lambda b, j: (jnp.maximum((b * nj + j) * (ts // SUBLANES) - 1, 0), 0))
    full = lambda shape: pl.BlockSpec(shape, lambda b, j: (0,) * len(shape))
    return pl.pallas_call(
        functools.partial(_mixer_kernel, ts=ts),
        grid=(bsz, nj),
        in_specs=[row(D_MODEL), row(CONV_DIM), prev8, row(CONV_DIM), row(Q_DIM),
                  row(Q_DIM), prev_blk(Q_DIM), row(Q_DIM), prev_blk(Q_DIM), row(D_MODEL), row(D_MODEL),
                  full((CONV_K, CONV_DIM)), full((N_HEADS, BLOCK, 2 * BLOCK)),
                  pl.BlockSpec(memory_space=pltpu.SMEM),
                  full((CONV_DIM, D_MODEL)), full((Q_DIM, D_MODEL)), full((D_MODEL, D_MODEL))],
        out_specs=row(D_MODEL),
        out_shape=jax.ShapeDtypeStruct((t, D_MODEL), F32),
        scratch_shapes=[pltpu.VMEM((ts, Q_DIM), BF16)],
        compiler_params=pltpu.CompilerParams(dimension_semantics=("arbitrary", "arbitrary"),
                                             vmem_limit_bytes=56 * 1024 * 1024),
        name="peer_block_mixer",
    )(x2, hc, hc, bc, qn, kx, kx, vx, vx, gc, ga, w_conv.astype(F32), biasm, sinks.astype(F32),
      w_conv_out.astype(BF16), w_attn_out.astype(BF16), w_out.astype(BF16))


def _band_bias(rel_bias):
    q_loc = jnp.arange(BLOCK, dtype=jnp.int32)[:, None]
    k_loc = jnp.arange(2 * BLOCK, dtype=jnp.int32)[None, :]
    dist = q_loc + BLOCK - k_loc
    max_exact = N_BUCKETS // 2
    d = jnp.maximum(dist, 0)
    df = jnp.maximum(d, 1).astype(F32)
    large = max_exact + (jnp.log(df / max_exact) / math.log(MAX_DISTANCE / max_exact)
                         * (N_BUCKETS - max_exact)).astype(jnp.int32)
    large = jnp.minimum(large, N_BUCKETS - 1)
    bucket = jnp.where(d < max_exact, d, large)
    onehot = (bucket[None] == jnp.arange(N_BUCKETS, dtype=jnp.int32)[:, None, None]).astype(F32)
    bias = jnp.einsum("bh,bqk->hqk", rel_bias.astype(F32), onehot, precision=lax.Precision.HIGHEST)
    valid = (dist >= 0) & (dist < WINDOW)
    return jnp.where(valid[None], bias, NEG_BIG)


def _top16(vals, rid, big):
    tv, ti = [], []
    for _ in range(PEER_TOPK):
        m = jnp.max(vals, axis=0, keepdims=True)
        idx = jnp.min(jnp.where(vals == m, rid, big), axis=0, keepdims=True)
        tv.append(m)
        ti.append(idx)
        vals = jnp.where(rid == idx, -jnp.inf, vals)
    return jnp.concatenate(tv, axis=0), jnp.concatenate(ti, axis=0)


def _row_iota(n_rows):
    return lax.broadcasted_iota(jnp.int32, (n_rows, LANES), 0).astype(F32)


def _pair_candidates(v0, v1):
    r8 = _row_iota(SUBLANES)
    r16 = _row_iota(PEER_TOPK)
    vals = [v0[0:1, :] + v1]
    pos = [r16]
    for a in (1, 2, 3):
        vals.append(v0[a:a + 1, :] + v1[0:SUBLANES, :])
        pos.append(r8 + float(a * PEER_TOPK))
    low = r8 < 4.0
    v1dup = jnp.where(low, v1[0:SUBLANES, :], pltpu.roll(v1[0:SUBLANES, :], 4, axis=0))
    bdup = jnp.where(low, r8, r8 - 4.0)
    for a in (4, 6):
        vals.append(jnp.where(low, v0[a:a + 1, :], v0[a + 1:a + 2, :]) + v1dup)
        pos.append(jnp.where(low, float(a * PEER_TOPK), float((a + 1) * PEER_TOPK)) + bdup)
    vals.append(v0[SUBLANES:, :] + v1[0:1, :])
    pos.append((r8 + float(SUBLANES)) * float(PEER_TOPK))
    return jnp.concatenate(vals, axis=0), jnp.concatenate(pos, axis=0)


def _pick(table, sel):
    out = jnp.zeros_like(sel)
    for a in range(PEER_TOPK):
        out = jnp.where(sel == float(a), table[a:a + 1, :], out)
    return out


def _route_kernel(x1_ref, g_ref, wqh_ref, wql_ref, kh_ref, kl_ref,
                  h2_ref, i0_ref, i1_ref, gt_ref, i0s, i1s, gs, *, tr):
    x = x1_ref[...]
    ms = jnp.mean(x * x, axis=-1, keepdims=True)
    h2 = (x * lax.rsqrt(ms + EPS)) * g_ref[...]
    h2_ref[...] = h2.astype(BF16)
    hh, hl = _split_bf16(h2)
    q = (jnp.dot(hh, wqh_ref[...], preferred_element_type=F32)
         + jnp.dot(hh, wql_ref[...], preferred_element_type=F32)
         + jnp.dot(hl, wqh_ref[...], preferred_element_type=F32))

    for hd in range(PEER_HEADS):
        tops = []
        for part in range(2):
            hp = hd * 2 + part
            qhp = q[:, hp * PEER_HALF:(hp + 1) * PEER_HALF]
            qh_, ql_ = _split_bf16(qhp)
            kh, kl = kh_ref[hp], kl_ref[hp]
            st = (lax.dot_general(kh, qh_, _NT, preferred_element_type=F32)
                  + lax.dot_general(kh, ql_, _NT, preferred_element_type=F32)
                  + lax.dot_general(kl, qh_, _NT, preferred_element_type=F32))
            parts = [_top16(st[:, c * LANES:(c + 1) * LANES], _row_iota(N_KEYS), float(N_KEYS))
                     for c in range(tr // LANES)]
            tops.append((jnp.concatenate([p[0] for p in parts], axis=1),
                         jnp.concatenate([p[1] for p in parts], axis=1)))
        (v0, j0), (v1, j1) = tops
        for c in range(tr // LANES):
            sl = slice(c * LANES, (c + 1) * LANES)
            cand, cpos = _pair_candidates(v0[:, sl], v1[:, sl])
            ts_, pos = _top16(cand, cpos, float(PEER_TOPK * PEER_TOPK))
            pa = jnp.floor(pos * (1.0 / PEER_TOPK))
            pb = pos - pa * PEER_TOPK
            e = jnp.exp(ts_ - ts_[0:1, :])
            gate = e / jnp.sum(e, axis=0, keepdims=True)
            rs = slice(hd * PEER_TOPK, (hd + 1) * PEER_TOPK)
            i0s[rs, sl] = _pick(j0[:, sl], pa)
            i1s[rs, sl] = _pick(j1[:, sl], pb)
            gs[rs, sl] = gate
    i0_ref[...] = i0s[...].T
    i1_ref[...] = i1s[...].T
    gt_ref[...] = gs[...].T


def _route_call(x1, norm_ffn, w_query, sub_keys, tr):
    t = x1.shape[0]
    qd = w_query.shape[1]
    wqh, wql = _split_bf16(w_query.astype(F32))
    keys = sub_keys.astype(F32).reshape(PEER_HEADS * 2, N_KEYS, PEER_HALF)
    kh, kl = _split_bf16(keys)
    full = lambda shape: pl.BlockSpec(shape, lambda i: (0,) * len(shape))
    row = lambda c: pl.BlockSpec((tr, c), lambda i: (i, 0))
    return pl.pallas_call(
        functools.partial(_route_kernel, tr=tr),
        grid=(t // tr,),
        in_specs=[row(D_MODEL), full((1, D_MODEL)), full((D_MODEL, qd)), full((D_MODEL, qd)),
                  full((PEER_HEADS * 2, N_KEYS, PEER_HALF)), full((PEER_HEADS * 2, N_KEYS, PEER_HALF))],
        out_specs=[row(D_MODEL), row(N_SEL), row(N_SEL), row(N_SEL)],
        out_shape=[jax.ShapeDtypeStruct((t, D_MODEL), BF16), jax.ShapeDtypeStruct((t, N_SEL), F32),
                   jax.ShapeDtypeStruct((t, N_SEL), F32), jax.ShapeDtypeStruct((t, N_SEL), F32)],
        scratch_shapes=[pltpu.VMEM((N_SEL, tr), F32)] * 3,
        compiler_params=pltpu.CompilerParams(dimension_semantics=("arbitrary",),
                                             vmem_limit_bytes=56 * 1024 * 1024),
        name="peer_block_route",
    )(x1, norm_ffn.reshape(1, D_MODEL).astype(F32), wqh, wql, kh, kl)


def _gate_tiles(gd_ref, tg, i0b, i1b, gh, gl, rid):
    zero = jnp.zeros_like(rid)
    one = jnp.ones_like(rid)
    for s in range(SUBLANES):
        m0 = rid == i0b[s:s + 1, :]
        rh = jnp.where(m0, gh[s:s + 1, :], zero)
        rl = jnp.where(m0, gl[s:s + 1, :], zero)
        ct = jnp.where(rid == i1b[s:s + 1, :], one, zero)
        gd2 = lax.dot_general(jnp.concatenate([rh, rl], axis=0), ct, _NT, preferred_element_type=F32)
        gd = gd2[:N_KEYS] + gd2[N_KEYS:]
        gd_ref[tg, pl.ds(s, N_KEYS, stride=SUBLANES), :] = gd


def _expert_kernel(x1_ref, h2_ref, i0_ref, i1_ref, gt_ref, u_ref, v_ref, out_ref, gd_ref, *, tb, ec):
    c = pl.program_id(1)
    ngrp = ec // N_KEYS

    @pl.when(c == 0)
    def _build_gate_matrix():
        rid = lax.broadcasted_iota(jnp.int32, (N_KEYS, N_SEL), 0).astype(F32).astype(BF16)

        def body(it, carry):
            for k in range(GATE_GROUPS_PER_ITER):
                tg = it * GATE_GROUPS_PER_ITER + k
                r0 = pl.multiple_of(tg * SUBLANES, SUBLANES)
                gh, gl = _split_bf16(gt_ref[pl.ds(r0, SUBLANES), :])
                _gate_tiles(gd_ref, tg, i0_ref[pl.ds(r0, SUBLANES), :].astype(BF16),
                            i1_ref[pl.ds(r0, SUBLANES), :].astype(BF16), gh, gl, rid)
            return carry

        lax.fori_loop(0, tb // (SUBLANES * GATE_GROUPS_PER_ITER), body, 0)
        out_ref[...] = x1_ref[...]

    a = lax.dot_general(h2_ref[...], u_ref[...], _NT, preferred_element_type=F32)
    base = pl.multiple_of(c * (ngrp * SUBLANES), SUBLANES)
    gd = jnp.concatenate(
        [gd_ref[:, pl.ds(base + g * SUBLANES, SUBLANES), :].reshape(tb, N_KEYS) for g in range(ngrp)], axis=1)
    gelu = 0.5 * a * (1.0 + lax.erf(a * (2.0 ** -0.5)))
    w = (gd * gelu).astype(BF16)
    out_ref[...] += jnp.dot(w, v_ref[...], preferred_element_type=F32)


def _expert_call(x1, h2, i0, i1, gt, expert_u, expert_v, tb, ec):
    t = x1.shape[0]
    row = lambda c_: pl.BlockSpec((tb, c_), lambda i, c: (i, 0))
    chunk = pl.BlockSpec((ec, D_MODEL), lambda i, c: (c, 0))
    return pl.pallas_call(
        functools.partial(_expert_kernel, tb=tb, ec=ec),
        grid=(t // tb, N_EXPERTS // ec),
        in_specs=[row(D_MODEL), row(D_MODEL), row(N_SEL), row(N_SEL), row(N_SEL), chunk, chunk],
        out_specs=row(D_MODEL),
        out_shape=jax.ShapeDtypeStruct((t, D_MODEL), F32),
        scratch_shapes=[pltpu.VMEM((tb // SUBLANES, N_KEYS * SUBLANES, N_KEYS), F32)],
        compiler_params=pltpu.CompilerParams(dimension_semantics=("arbitrary", "arbitrary"),
                                             vmem_limit_bytes=60 * 1024 * 1024),
        name="peer_block_experts",
    )(x1, h2, i0, i1, gt, expert_u.astype(BF16), expert_v.astype(BF16))


def _tile_sizes(bsz, s):
    t = bsz * s
    tm = math.gcd(t, 512)
    ts = math.gcd(s, 512)
    tr = math.gcd(t, 256)
    tb = math.gcd(t, 512)
    return tm, ts, tr, tb


def kernel(x, norm_mix, norm_ffn, w_in, b_gate, w_conv, q_norm, k_norm, sinks, rel_bias, w_conv_out, w_attn_out,
           w_out, w_query, sub_keys, expert_u, expert_v):
    bsz, s, d = x.shape
    assert d == D_MODEL and s % BLOCK == 0
    tm, ts, tr, tb = _tile_sizes(bsz, s)
    biasm = _band_bias(rel_bias)
    x2 = x.reshape(bsz * s, d)
    for l in range(norm_mix.shape[0]):
        hc, bc, qn, kx, vx, gc, ga = _proj_call(x2, norm_mix[l], w_in[l], b_gate[l], q_norm[l], k_norm[l], tm)
        x1 = _mixer_call(x2, hc, bc, qn, kx, vx, gc, ga, w_conv[l], biasm, sinks[l], w_conv_out[l],
                         w_attn_out[l], w_out[l], bsz, s, ts)
        h2, i0, i1, gt = _route_call(x1, norm_ffn[l], w_query[l], sub_keys[l], tr)
        x2 = _expert_call(x1, h2, i0, i1, gt, expert_u[l], expert_v[l], tb, EXPERT_CHUNK)
    return x2.reshape(bsz, s, d)
```

```python
import functools
import math

import jax
import jax.numpy as jnp
import numpy as np
from jax import lax
from jax.experimental import pallas as pl
from jax.experimental.pallas import tpu as pltpu

F32 = jnp.float32
BF16 = jnp.bfloat16

D_MODEL = 1024
CONV_DIM = 512
CONV_K = 3
N_HEADS = 8
N_KV_HEADS = 2
HEAD_DIM = 64
Q_DIM = N_HEADS * HEAD_DIM
KV_DIM = N_KV_HEADS * HEAD_DIM
GROUP = N_HEADS // N_KV_HEADS
WINDOW = 128
BLOCK = 128
N_BUCKETS = 32
MAX_DISTANCE = 128
PEER_HEADS = 8
N_KEYS = 128
N_EXPERTS = N_KEYS * N_KEYS
PEER_QDIM = 256
PEER_HALF = PEER_QDIM // 2
PEER_TOPK = 16
N_SEL = PEER_HEADS * PEER_TOPK
EPS = 1e-6
NEG_BIG = -1e30
PAIR_W = 2 * HEAD_DIM
KV_WIDE = N_KV_HEADS * PAIR_W

SUBLANES = 8
LANES = 128

EXPERT_CHUNK = N_EXPERTS // (2 * PEER_HEADS)
EXPERT_PIECE = 2 * N_KEYS
GATE_GROUPS_PER_ITER = 2
PEER_VMEM_LIMIT = 60 * 1024 * 1024
DENSE_VMEM_LIMIT = 56 * 1024 * 1024

_OFFS = np.cumsum([0, CONV_DIM, CONV_DIM, CONV_DIM, Q_DIM, KV_DIM, KV_DIM, D_MODEL, D_MODEL]).tolist()

_NT = (((1,), (1,)), ((), ()))


def _split_bf16(a):
    hi = a.astype(BF16)
    lo = (a - hi.astype(F32)).astype(BF16)
    return hi, lo


def _proj_kernel(x_ref, g_ref, w_ref, bg_ref, qg_ref, kg_ref, avq_ref, avk_ref, exp_ref,
                 hc_ref, bc_ref, qn_ref, kv_ref, gc_ref, ga_ref):
    x = x_ref[...]
    ms = jnp.mean(x * x, axis=-1, keepdims=True)
    h = ((x * lax.rsqrt(ms + EPS)) * g_ref[...]).astype(BF16)

    def seg(i):
        return jnp.dot(h, w_ref[:, _OFFS[i]:_OFFS[i + 1]], preferred_element_type=F32)

    u = seg(0)
    hc_ref[...] = seg(2) * u
    bc_ref[...] = seg(1)

    def head_rms(a, av_ref):
        hi, lo = _split_bf16(a * a)
        return (jnp.dot(hi, av_ref[...], preferred_element_type=F32)
                + jnp.dot(lo, av_ref[...], preferred_element_type=F32))

    q = seg(3)
    qn = (q * lax.rsqrt(head_rms(q, avq_ref) + EPS)) * qg_ref[...] * (HEAD_DIM ** -0.5)
    qn_ref[...] = qn.astype(BF16)

    k = seg(4)
    kn = ((k * lax.rsqrt(head_rms(k, avk_ref) + EPS)) * kg_ref[...]).astype(BF16)
    v = seg(5).astype(BF16)
    kv_ref[...] = jnp.dot(jnp.concatenate([kn, v], axis=1), exp_ref[...], preferred_element_type=F32).astype(BF16)

    bg = bg_ref[...]
    gc_ref[...] = jax.nn.sigmoid(seg(6) + bg[:, :D_MODEL]).astype(BF16)
    ga_ref[...] = jax.nn.sigmoid(seg(7) + bg[:, D_MODEL:]).astype(BF16)


def _kv_layout():
    e = np.zeros((2 * KV_DIM, 3 * KV_WIDE), np.float32)
    for g in range(N_KV_HEADS):
        for d in range(HEAD_DIM):
            e[g * HEAD_DIM + d, g * PAIR_W + d] = 1.0
            e[g * HEAD_DIM + d, KV_WIDE + g * PAIR_W + HEAD_DIM + d] = 1.0
            e[KV_DIM + g * HEAD_DIM + d, 2 * KV_WIDE + g * PAIR_W + d] = 1.0
            e[KV_DIM + g * HEAD_DIM + d, 2 * KV_WIDE + g * PAIR_W + HEAD_DIM + d] = 1.0
    return jnp.asarray(e, BF16)


def _proj_call(x2, norm_mix, w_in, b_gate, q_norm, k_norm, tm):
    t = x2.shape[0]
    in_dim = w_in.shape[1]
    avq = jnp.kron(jnp.eye(N_HEADS, dtype=F32), jnp.full((HEAD_DIM, HEAD_DIM), 1.0 / HEAD_DIM, F32)).astype(BF16)
    avk = jnp.kron(jnp.eye(N_KV_HEADS, dtype=F32), jnp.full((HEAD_DIM, HEAD_DIM), 1.0 / HEAD_DIM, F32)).astype(BF16)
    qg = jnp.tile(q_norm.astype(F32), N_HEADS).reshape(1, Q_DIM)
    kg = jnp.tile(k_norm.astype(F32), N_KV_HEADS).reshape(1, KV_DIM)
    full = lambda shape: pl.BlockSpec(shape, lambda i: (0,) * len(shape))
    row = lambda c: pl.BlockSpec((tm, c), lambda i: (i, 0))
    return pl.pallas_call(
        _proj_kernel,
        grid=(t // tm,),
        in_specs=[row(D_MODEL), full((1, D_MODEL)), full((D_MODEL, in_dim)), full((1, 2 * D_MODEL)),
                  full((1, Q_DIM)), full((1, KV_DIM)), full((Q_DIM, Q_DIM)), full((KV_DIM, KV_DIM)),
                  full((2 * KV_DIM, 3 * KV_WIDE))],
        out_specs=[row(CONV_DIM), row(CONV_DIM), row(Q_DIM), row(3 * KV_WIDE), row(D_MODEL), row(D_MODEL)],
        out_shape=[jax.ShapeDtypeStruct((t, CONV_DIM), F32), jax.ShapeDtypeStruct((t, CONV_DIM), F32),
                   jax.ShapeDtypeStruct((t, Q_DIM), BF16), jax.ShapeDtypeStruct((t, 3 * KV_WIDE), BF16),
                   jax.ShapeDtypeStruct((t, D_MODEL), BF16), jax.ShapeDtypeStruct((t, D_MODEL), BF16)],
        compiler_params=pltpu.CompilerParams(dimension_semantics=("arbitrary",),
                                             vmem_limit_bytes=DENSE_VMEM_LIMIT),
        name="peer_block_proj",
    )(x2, norm_mix.reshape(1, D_MODEL).astype(F32), w_in.astype(BF16), b_gate.reshape(1, 2 * D_MODEL).astype(F32),
      qg, kg, avq, avk, _kv_layout())


def _mixer_kernel(x_ref, hc_ref, hcp_ref, bc_ref, qn_ref, kv_ref, kvp_ref, gc_ref, ga_ref,
                  wc_ref, bias_ref, sink_ref, wco_ref, wao_ref, wo_ref, g2_ref, wqh_ref, wql_ref,
                  x1_ref, h2_ref, pq_ref, att_ref, *, ts):
    j = pl.program_id(1)
    first = j == 0

    hc = hc_ref[...]
    prev = jnp.where(first, 0.0, hcp_ref[...])
    rows = lax.broadcasted_iota(jnp.int32, hc.shape, 0)
    s1 = pltpu.roll(hc, 1, axis=0)
    s1 = jnp.where(rows == 0, prev[SUBLANES - 1:SUBLANES, :], s1)
    s2 = pltpu.roll(hc, 2, axis=0)
    s2 = jnp.where(rows == 0, prev[SUBLANES - 2:SUBLANES - 1, :], s2)
    s2 = jnp.where(rows == 1, prev[SUBLANES - 1:SUBLANES, :], s2)
    wc = wc_ref[...]
    conv = s2 * wc[0:1, :] + s1 * wc[1:2, :] + hc * wc[2:3, :]
    yc = (bc_ref[...] * conv).astype(BF16)
    y_conv = jnp.dot(yc, wco_ref[...], preferred_element_type=F32)

    kvfull = jnp.concatenate([kvp_ref[...], kv_ref[...]], axis=0)
    col = lax.broadcasted_iota(jnp.int32, (BLOCK, 2 * BLOCK), 1)
    pen0 = jnp.where(jnp.logical_and(first, col < BLOCK), NEG_BIG, 0.0)
    even_lanes = lax.broadcasted_iota(jnp.int32, (BLOCK, PAIR_W), 1) < HEAD_DIM
    for r in range(ts // BLOCK):
        kvb = kvfull[r * BLOCK:(r + 2) * BLOCK, :]
        for pair in range(N_HEADS // 2):
            g = (2 * pair) // GROUP
            qs = qn_ref[r * BLOCK:(r + 1) * BLOCK, pair * PAIR_W:(pair + 1) * PAIR_W]
            vsel = kvb[:, 2 * KV_WIDE + g * PAIR_W:2 * KV_WIDE + (g + 1) * PAIR_W]
            outs = []
            for par in range(2):
                hd = 2 * pair + par
                ksel = kvb[:, par * KV_WIDE + g * PAIR_W:par * KV_WIDE + (g + 1) * PAIR_W]
                lg = lax.dot_general(qs, ksel, _NT, preferred_element_type=F32) + bias_ref[hd]
                if r == 0:
                    lg = lg + pen0
                sink = sink_ref[hd]
                m = jnp.maximum(jnp.max(lg, axis=-1, keepdims=True), sink)
                p = jnp.exp(lg - m)
                denom = jnp.sum(p, axis=-1, keepdims=True) + jnp.exp(sink - m)
                pv = jnp.dot(p.astype(BF16), vsel, preferred_element_type=F32)
                outs.append(pv / denom)
            att_ref[r * BLOCK:(r + 1) * BLOCK, pair * PAIR_W:(pair + 1) * PAIR_W] = (
                jnp.where(even_lanes, outs[0], outs[1]).astype(BF16))
    y_attn = jnp.dot(att_ref[...], wao_ref[...], preferred_element_type=F32)

    mixed = (gc_ref[...].astype(F32) * y_conv + ga_ref[...].astype(F32) * y_attn).astype(BF16)
    x1 = x_ref[...] + jnp.dot(mixed, wo_ref[...], preferred_element_type=F32)
    x1_ref[...] = x1

    ms = jnp.mean(x1 * x1, axis=-1, keepdims=True)
    h2 = (x1 * lax.rsqrt(ms + EPS)) * g2_ref[...]
    h2_ref[...] = h2.astype(BF16)
    hh, hl = _split_bf16(h2)
    pq_ref[...] = (jnp.dot(hh, wqh_ref[...], preferred_element_type=F32)
                   + jnp.dot(hh, wql_ref[...], preferred_element_type=F32)
                   + jnp.dot(hl, wqh_ref[...], preferred_element_type=F32))


def _mixer_call(x2, hc, bc, qn, kv, gc, ga, w_conv, biasm, sinks, w_conv_out, w_attn_out, w_out,
                norm_ffn, w_query, bsz, s, ts):
    t = bsz * s
    qd = w_query.shape[1]
    wqh, wql = _split_bf16(w_query.astype(F32))
    nj = s // ts
    row = lambda c: pl.BlockSpec((ts, c), lambda b, j: (b * nj + j, 0))
    prev_blk = lambda c: pl.BlockSpec((BLOCK, c), lambda b, j: (jnp.maximum((b * nj + j) * (ts // BLOCK) - 1, 0), 0))
    prev8 = pl.BlockSpec((SUBLANES, CONV_DIM),
                         lambda b, j: (jnp.maximum((b * nj + j) * (ts // SUBLANES) - 1, 0), 0))
    full = lambda shape: pl.BlockSpec(shape, lambda b, j: (0,) * len(shape))
    return pl.pallas_call(
        functools.partial(_mixer_kernel, ts=ts),
        grid=(bsz, nj),
        in_specs=[row(D_MODEL), row(CONV_DIM), prev8, row(CONV_DIM), row(Q_DIM),
                  row(3 * KV_WIDE), prev_blk(3 * KV_WIDE), row(D_MODEL), row(D_MODEL),
                  full((CONV_K, CONV_DIM)), full((N_HEADS, BLOCK, 2 * BLOCK)),
                  pl.BlockSpec(memory_space=pltpu.SMEM),
                  full((CONV_DIM, D_MODEL)), full((Q_DIM, D_MODEL)), full((D_MODEL, D_MODEL)),
                  full((1, D_MODEL)), full((D_MODEL, qd)), full((D_MODEL, qd))],
        out_specs=[row(D_MODEL), row(D_MODEL), row(qd)],
        out_shape=[jax.ShapeDtypeStruct((t, D_MODEL), F32), jax.ShapeDtypeStruct((t, D_MODEL), BF16),
                   jax.ShapeDtypeStruct((t, qd), F32)],
        scratch_shapes=[pltpu.VMEM((ts, Q_DIM), BF16)],
        compiler_params=pltpu.CompilerParams(dimension_semantics=("arbitrary", "arbitrary"),
                                             vmem_limit_bytes=DENSE_VMEM_LIMIT),
        name="peer_block_mixer",
    )(x2, hc, hc, bc, qn, kv, kv, gc, ga, w_conv.astype(F32), biasm, sinks.astype(F32),
      w_conv_out.astype(BF16), w_attn_out.astype(BF16), w_out.astype(BF16),
      norm_ffn.reshape(1, D_MODEL).astype(F32), wqh, wql)


def _band_bias(rel_bias):
    q_loc = jnp.arange(BLOCK, dtype=jnp.int32)[:, None]
    k_loc = jnp.arange(2 * BLOCK, dtype=jnp.int32)[None, :]
    dist = q_loc + BLOCK - k_loc
    max_exact = N_BUCKETS // 2
    d = jnp.maximum(dist, 0)
    df = jnp.maximum(d, 1).astype(F32)
    large = max_exact + (jnp.log(df / max_exact) / math.log(MAX_DISTANCE / max_exact)
                         * (N_BUCKETS - max_exact)).astype(jnp.int32)
    large = jnp.minimum(large, N_BUCKETS - 1)
    bucket = jnp.where(d < max_exact, d, large)
    onehot = (bucket[None] == jnp.arange(N_BUCKETS, dtype=jnp.int32)[:, None, None]).astype(F32)
    bias = jnp.einsum("bh,bqk->hqk", rel_bias.astype(F32), onehot, precision=lax.Precision.HIGHEST)
    valid = (dist >= 0) & (dist < WINDOW)
    return jnp.where(valid[None], bias, NEG_BIG)


def _top16(vals, rid, big):
    tv, ti = [], []
    for _ in range(PEER_TOPK):
        m = jnp.max(vals, axis=0, keepdims=True)
        idx = jnp.min(jnp.where(vals == m, rid, big), axis=0, keepdims=True)
        tv.append(m)
        ti.append(idx)
        vals = jnp.where(rid == idx, -jnp.inf, vals)
    return jnp.concatenate(tv, axis=0), jnp.concatenate(ti, axis=0)


def _row_iota(n_rows):
    return lax.broadcasted_iota(jnp.int32, (n_rows, LANES), 0).astype(F32)


def _pair_candidates(v0, v1):
    r8 = _row_iota(SUBLANES)
    r16 = _row_iota(PEER_TOPK)
    vals = [v0[0:1, :] + v1]
    pos = [r16]
    for a in (1, 2, 3):
        vals.append(v0[a:a + 1, :] + v1[0:SUBLANES, :])
        pos.append(r8 + float(a * PEER_TOPK))
    low = r8 < 4.0
    v1dup = jnp.where(low, v1[0:SUBLANES, :], pltpu.roll(v1[0:SUBLANES, :], 4, axis=0))
    bdup = jnp.where(low, r8, r8 - 4.0)
    for a in (4, 6):
        vals.append(jnp.where(low, v0[a:a + 1, :], v0[a + 1:a + 2, :]) + v1dup)
        pos.append(jnp.where(low, float(a * PEER_TOPK), float((a + 1) * PEER_TOPK)) + bdup)
    vals.append(v0[SUBLANES:, :] + v1[0:1, :])
    pos.append((r8 + float(SUBLANES)) * float(PEER_TOPK))
    return jnp.concatenate(vals, axis=0), jnp.concatenate(pos, axis=0)


def _pick(table, sel):
    out = jnp.zeros_like(sel)
    for a in range(PEER_TOPK):
        out = jnp.where(sel == float(a), table[a:a + 1, :], out)
    return out


def _select_pairs(top0, top1):
    (v0, j0), (v1, j1) = top0, top1
    cand, cpos = _pair_candidates(v0, v1)
    ts_, pos = _top16(cand, cpos, float(PEER_TOPK * PEER_TOPK))
    pa = jnp.floor(pos * (1.0 / PEER_TOPK))
    pb = pos - pa * PEER_TOPK
    e = jnp.exp(ts_ - ts_[0:1, :])
    return _pick(j0, pa), _pick(j1, pb), e / jnp.sum(e, axis=0, keepdims=True)


def _gate_tiles(gd_ref, tg, i0b, i1b, gh, gl, rid):
    zero = jnp.zeros_like(rid)
    one = jnp.ones_like(rid)
    for s in range(SUBLANES):
        m0 = rid == i0b[s:s + 1, :]
        rh = jnp.where(m0, gh[s:s + 1, :], zero)
        rl = jnp.where(m0, gl[s:s + 1, :], zero)
        ct = jnp.where(rid == i1b[s:s + 1, :], one, zero)
        gd2 = lax.dot_general(jnp.concatenate([rh, rl], axis=0), ct, _NT, preferred_element_type=F32)
        gd = gd2[:N_KEYS] + gd2[N_KEYS:]
        gd_ref[tg, pl.ds(s, N_KEYS, stride=SUBLANES), :] = gd


def _peer_kernel(x1_ref, h2_ref, q_ref, kh_ref, kl_ref, u_ref, v_ref, out_ref, gd_ref, rs_ref, tok_ref, *, tb, ec):
    ib = pl.program_id(0)
    c = pl.program_id(1)
    half_tok = tb // 2
    n_lg = half_tok // LANES
    n_pieces = ec // EXPERT_PIECE
    grp_per_piece = EXPERT_PIECE // N_KEYS
    w_slot = ib % 2
    r_slot = 1 - w_slot

    @pl.when(jnp.logical_and(ib == 0, c == 0))
    def _no_block_before_the_first():
        tok_ref[r_slot] = jnp.zeros(tok_ref.shape[1:], F32)

    @pl.when(c == 0)
    def _build_gate_matrix():
        rid = lax.broadcasted_iota(jnp.int32, (N_KEYS, N_SEL), 0).astype(F32).astype(BF16)

        def body(it, carry):
            for k in range(GATE_GROUPS_PER_ITER):
                tg = it * GATE_GROUPS_PER_ITER + k
                r0 = pl.multiple_of(tg * SUBLANES, SUBLANES)
                gh, gl = _split_bf16(0.5 * tok_ref[r_slot, 2, pl.ds(r0, SUBLANES), :])
                _gate_tiles(gd_ref, tg, tok_ref[r_slot, 0, pl.ds(r0, SUBLANES), :].astype(BF16),
                            tok_ref[r_slot, 1, pl.ds(r0, SUBLANES), :].astype(BF16), gh, gl, rid)
            return carry

        lax.fori_loop(0, tb // (SUBLANES * GATE_GROUPS_PER_ITER), body, 0)
        out_ref[...] = x1_ref[...]

    hd = c // 2
    half = c % 2
    base = pl.multiple_of(c * (ec // N_KEYS * SUBLANES), SUBLANES)
    h2 = h2_ref[...]

    def scores(part):
        qh_, ql_ = _split_bf16(q_ref[:, part * PEER_HALF:(part + 1) * PEER_HALF])
        kh, kl = kh_ref[hd * 2 + part], kl_ref[hd * 2 + part]
        return (lax.dot_general(kh, qh_, _NT, preferred_element_type=F32)
                + lax.dot_general(kh, ql_, _NT, preferred_element_type=F32)
                + lax.dot_general(kl, qh_, _NT, preferred_element_type=F32))

    def weights(j):
        a = lax.dot_general(h2, u_ref[j * EXPERT_PIECE:(j + 1) * EXPERT_PIECE, :], _NT, preferred_element_type=F32)
        gd = jnp.concatenate(
            [gd_ref[:, pl.ds(base + (grp_per_piece * j + g) * SUBLANES, SUBLANES), :].reshape(tb, N_KEYS)
             for g in range(grp_per_piece)], axis=1)
        return (gd * (a * (1.0 + lax.erf(a * (2.0 ** -0.5))))).astype(BF16)

    tops = [[None] * n_lg, [None] * n_lg]
    w_pieces = []
    for part in range(2):
        st = scores(part)
        for lg in range(n_lg):
            tops[part][lg] = _top16(st[:, lg * LANES:(lg + 1) * LANES], _row_iota(N_KEYS), float(N_KEYS))
            w_pieces.extend(weights(j) for j in range(len(w_pieces), min(len(w_pieces) + 2, n_pieces)))
    w_pieces.extend(weights(j) for j in range(len(w_pieces), n_pieces))
    sel = [_select_pairs(tops[0][lg], tops[1][lg]) for lg in range(n_lg)]
    rows = pl.ds(pl.multiple_of(hd * PEER_TOPK, PEER_TOPK), PEER_TOPK)
    for k in range(3):
        rs_ref[w_slot, half, k, rows, :] = jnp.concatenate([s_[k] for s_ in sel], axis=1)
    out_ref[...] += jnp.dot(jnp.concatenate(w_pieces, axis=1), v_ref[...], preferred_element_type=F32)

    @pl.when(c == pl.num_programs(1) - 1)
    def _routing_to_token_major():
        for hf in range(2):
            for k in range(3):
                tok_ref[w_slot, k, hf * half_tok:(hf + 1) * half_tok, :] = rs_ref[w_slot, hf, k].T


def _peer_call(x1, h2, pq, sub_keys, expert_u, expert_v, tb, ec):
    t = x1.shape[0]
    nb = t // tb
    nc = N_EXPERTS // ec
    assert nc == 2 * PEER_HEADS and tb % (2 * LANES) == 0 and ec % EXPERT_PIECE == 0
    keys = sub_keys.astype(F32).reshape(PEER_HEADS * 2, N_KEYS, PEER_HALF)
    kh, kl = _split_bf16(keys)
    prev_row = lambda c_: pl.BlockSpec((tb, c_), lambda i, c: (jnp.maximum(i - 1, 0), 0))
    chunk = pl.BlockSpec((ec, D_MODEL), lambda i, c: (c, 0))
    q_spec = pl.BlockSpec((tb // 2, 2 * PEER_HALF), lambda i, c: (jnp.minimum(i, nb - 1) * 2 + c % 2, c // 2))
    full3 = pl.BlockSpec((PEER_HEADS * 2, N_KEYS, PEER_HALF), lambda i, c: (0, 0, 0))
    return pl.pallas_call(
        functools.partial(_peer_kernel, tb=tb, ec=ec),
        grid=(nb + 1, nc),
        in_specs=[prev_row(D_MODEL), prev_row(D_MODEL), q_spec, full3, full3, chunk, chunk],
        out_specs=prev_row(D_MODEL),
        out_shape=jax.ShapeDtypeStruct((t, D_MODEL), F32),
        scratch_shapes=[pltpu.VMEM((tb // SUBLANES, N_KEYS * SUBLANES, N_KEYS), F32),
                        pltpu.VMEM((2, 2, 3, N_SEL, tb // 2), F32),
                        pltpu.VMEM((2, 3, tb, N_SEL), F32)],
        compiler_params=pltpu.CompilerParams(dimension_semantics=("arbitrary", "arbitrary"),
                                             vmem_limit_bytes=PEER_VMEM_LIMIT),
        name="peer_block_experts",
    )(x1, h2, pq, kh, kl, expert_u.astype(BF16), expert_v.astype(BF16))


def _tile_sizes(bsz, s):
    t = bsz * s
    tm = math.gcd(t, 512)
    ts = math.gcd(s, 512)
    tb = math.gcd(t, 512)
    return tm, ts, tb


def kernel(x, norm_mix, norm_ffn, w_in, b_gate, w_conv, q_norm, k_norm, sinks, rel_bias, w_conv_out, w_attn_out,
           w_out, w_query, sub_keys, expert_u, expert_v):
    bsz, s, d = x.shape
    assert d == D_MODEL and s % BLOCK == 0
    tm, ts, tb = _tile_sizes(bsz, s)
    biasm = _band_bias(rel_bias)
    x2 = x.reshape(bsz * s, d)
    for l in range(norm_mix.shape[0]):
        hc, bc, qn, kv, gc, ga = _proj_call(x2, norm_mix[l], w_in[l], b_gate[l], q_norm[l], k_norm[l], tm)
        x1, h2, pq = _mixer_call(x2, hc, bc, qn, kv, gc, ga, w_conv[l], biasm, sinks[l], w_conv_out[l],
                                 w_attn_out[l], w_out[l], norm_ffn[l], w_query[l], bsz, s, ts)
        x2 = _peer_call(x1, h2, pq, sub_keys[l], expert_u[l], expert_v[l], tb, EXPERT_CHUNK)
    return x2.reshape(bsz, s, d)
```

```python
import functools
import math

import jax
import jax.numpy as jnp
import numpy as np
from jax import lax
from jax.experimental import pallas as pl
from jax.experimental.pallas import tpu as pltpu

F32 = jnp.float32
BF16 = jnp.bfloat16

D_MODEL = 1024
CONV_DIM = 512
CONV_K = 3
N_HEADS = 8
N_KV_HEADS = 2
HEAD_DIM = 64
Q_DIM = N_HEADS * HEAD_DIM
KV_DIM = N_KV_HEADS * HEAD_DIM
GROUP = N_HEADS // N_KV_HEADS
WINDOW = 128
BLOCK = 128
N_BUCKETS = 32
MAX_DISTANCE = 128
PEER_HEADS = 8
N_KEYS = 128
N_EXPERTS = N_KEYS * N_KEYS
PEER_QDIM = 256
PEER_HALF = PEER_QDIM // 2
PEER_TOPK = 16
N_SEL = PEER_HEADS * PEER_TOPK
EPS = 1e-6
NEG_BIG = -1e30
PAIR_W = 2 * HEAD_DIM
KV_WIDE = N_KV_HEADS * PAIR_W

SUBLANES = 8
LANES = 128

EXPERT_CHUNK = N_EXPERTS // (2 * PEER_HEADS)
EXPERT_PIECE = 2 * N_KEYS
GATE_GROUPS_PER_ITER = 2
PEER_VMEM_LIMIT = 60 * 1024 * 1024
DENSE_VMEM_LIMIT = 56 * 1024 * 1024

_OFFS = np.cumsum([0, CONV_DIM, CONV_DIM, CONV_DIM, Q_DIM, KV_DIM, KV_DIM, D_MODEL, D_MODEL]).tolist()

_NT = (((1,), (1,)), ((), ()))


def _split_bf16(a):
    hi = a.astype(BF16)
    lo = (a - hi.astype(F32)).astype(BF16)
    return hi, lo


def _proj_kernel(x_ref, g_ref, w_ref, bg_ref, qg_ref, kg_ref, avq_ref, avk_ref, exp_ref,
                 hc_ref, bc_ref, qn_ref, kv_ref, gc_ref, ga_ref):
    x = x_ref[...]
    ms = jnp.mean(x * x, axis=-1, keepdims=True)
    h = ((x * lax.rsqrt(ms + EPS)) * g_ref[...]).astype(BF16)

    def seg(i):
        return jnp.dot(h, w_ref[:, _OFFS[i]:_OFFS[i + 1]], preferred_element_type=F32)

    u = seg(0)
    hc_ref[...] = seg(2) * u
    bc_ref[...] = seg(1)

    def head_rms(a, av_ref):
        hi, lo = _split_bf16(a * a)
        return (jnp.dot(hi, av_ref[...], preferred_element_type=F32)
                + jnp.dot(lo, av_ref[...], preferred_element_type=F32))

    q = seg(3)
    qn = (q * lax.rsqrt(head_rms(q, avq_ref) + EPS)) * qg_ref[...] * (HEAD_DIM ** -0.5)
    qn_ref[...] = qn.astype(BF16)

    k = seg(4)
    kn = ((k * lax.rsqrt(head_rms(k, avk_ref) + EPS)) * kg_ref[...]).astype(BF16)
    v = seg(5).astype(BF16)
    kv_ref[...] = jnp.dot(jnp.concatenate([kn, v], axis=1), exp_ref[...], preferred_element_type=F32).astype(BF16)

    bg = bg_ref[...]
    gc_ref[...] = jax.nn.sigmoid(seg(6) + bg[:, :D_MODEL]).astype(BF16)
    ga_ref[...] = jax.nn.sigmoid(seg(7) + bg[:, D_MODEL:]).astype(BF16)


def _kv_layout():
    e = np.zeros((2 * KV_DIM, 3 * KV_WIDE), np.float32)
    for g in range(N_KV_HEADS):
        for d in range(HEAD_DIM):
            e[g * HEAD_DIM + d, g * PAIR_W + d] = 1.0
            e[g * HEAD_DIM + d, KV_WIDE + g * PAIR_W + HEAD_DIM + d] = 1.0
            e[KV_DIM + g * HEAD_DIM + d, 2 * KV_WIDE + g * PAIR_W + d] = 1.0
            e[KV_DIM + g * HEAD_DIM + d, 2 * KV_WIDE + g * PAIR_W + HEAD_DIM + d] = 1.0
    return jnp.asarray(e, BF16)


def _proj_call(x2, norm_mix, w_in, b_gate, q_norm, k_norm, tm):
    t = x2.shape[0]
    in_dim = w_in.shape[1]
    avq = jnp.kron(jnp.eye(N_HEADS, dtype=F32), jnp.full((HEAD_DIM, HEAD_DIM), 1.0 / HEAD_DIM, F32)).astype(BF16)
    avk = jnp.kron(jnp.eye(N_KV_HEADS, dtype=F32), jnp.full((HEAD_DIM, HEAD_DIM), 1.0 / HEAD_DIM, F32)).astype(BF16)
    qg = jnp.tile(q_norm.astype(F32), N_HEADS).reshape(1, Q_DIM)
    kg = jnp.tile(k_norm.astype(F32), N_KV_HEADS).reshape(1, KV_DIM)
    full = lambda shape: pl.BlockSpec(shape, lambda i: (0,) * len(shape))
    row = lambda c: pl.BlockSpec((tm, c), lambda i: (i, 0))
    return pl.pallas_call(
        _proj_kernel,
        grid=(t // tm,),
        in_specs=[row(D_MODEL), full((1, D_MODEL)), full((D_MODEL, in_dim)), full((1, 2 * D_MODEL)),
                  full((1, Q_DIM)), full((1, KV_DIM)), full((Q_DIM, Q_DIM)), full((KV_DIM, KV_DIM)),
                  full((2 * KV_DIM, 3 * KV_WIDE))],
        out_specs=[row(CONV_DIM), row(CONV_DIM), row(Q_DIM), row(3 * KV_WIDE), row(D_MODEL), row(D_MODEL)],
        out_shape=[jax.ShapeDtypeStruct((t, CONV_DIM), F32), jax.ShapeDtypeStruct((t, CONV_DIM), F32),
                   jax.ShapeDtypeStruct((t, Q_DIM), BF16), jax.ShapeDtypeStruct((t, 3 * KV_WIDE), BF16),
                   jax.ShapeDtypeStruct((t, D_MODEL), BF16), jax.ShapeDtypeStruct((t, D_MODEL), BF16)],
        compiler_params=pltpu.CompilerParams(dimension_semantics=("arbitrary",),
                                             vmem_limit_bytes=DENSE_VMEM_LIMIT),
        name="peer_block_proj",
    )(x2, norm_mix.reshape(1, D_MODEL).astype(F32), w_in.astype(BF16), b_gate.reshape(1, 2 * D_MODEL).astype(F32),
      qg, kg, avq, avk, _kv_layout())


def _mixer_kernel(x_ref, hc_ref, hcp_ref, bc_ref, qn_ref, kv_ref, kvp_ref, gc_ref, ga_ref,
                  wc_ref, bias_ref, sink_ref, wco_ref, wao_ref, wo_ref, g2_ref, wq_ref,
                  x1_ref, h2_ref, pq_ref, att_ref, *, ts):
    j = pl.program_id(1)
    first = j == 0

    hc = hc_ref[...]
    prev = jnp.where(first, 0.0, hcp_ref[...])
    rows = lax.broadcasted_iota(jnp.int32, hc.shape, 0)
    s1 = pltpu.roll(hc, 1, axis=0)
    s1 = jnp.where(rows == 0, prev[SUBLANES - 1:SUBLANES, :], s1)
    s2 = pltpu.roll(hc, 2, axis=0)
    s2 = jnp.where(rows == 0, prev[SUBLANES - 2:SUBLANES - 1, :], s2)
    s2 = jnp.where(rows == 1, prev[SUBLANES - 1:SUBLANES, :], s2)
    wc = wc_ref[...]
    conv = s2 * wc[0:1, :] + s1 * wc[1:2, :] + hc * wc[2:3, :]
    yc = (bc_ref[...] * conv).astype(BF16)
    y_conv = jnp.dot(yc, wco_ref[...], preferred_element_type=F32)

    kvfull = jnp.concatenate([kvp_ref[...], kv_ref[...]], axis=0)
    col = lax.broadcasted_iota(jnp.int32, (BLOCK, 2 * BLOCK), 1)
    pen0 = jnp.where(jnp.logical_and(first, col < BLOCK), NEG_BIG, 0.0)
    even_lanes = lax.broadcasted_iota(jnp.int32, (BLOCK, PAIR_W), 1) < HEAD_DIM
    for r in range(ts // BLOCK):
        kvb = kvfull[r * BLOCK:(r + 2) * BLOCK, :]
        for pair in range(N_HEADS // 2):
            g = (2 * pair) // GROUP
            qs = qn_ref[r * BLOCK:(r + 1) * BLOCK, pair * PAIR_W:(pair + 1) * PAIR_W]
            vsel = kvb[:, 2 * KV_WIDE + g * PAIR_W:2 * KV_WIDE + (g + 1) * PAIR_W]
            outs = []
            for par in range(2):
                hd = 2 * pair + par
                ksel = kvb[:, par * KV_WIDE + g * PAIR_W:par * KV_WIDE + (g + 1) * PAIR_W]
                lg = lax.dot_general(qs, ksel, _NT, preferred_element_type=F32) + bias_ref[hd]
                if r == 0:
                    lg = lg + pen0
                sink = sink_ref[hd]
                m = jnp.maximum(jnp.max(lg, axis=-1, keepdims=True), sink)
                p = jnp.exp(lg - m)
                denom = jnp.sum(p, axis=-1, keepdims=True) + jnp.exp(sink - m)
                pv = jnp.dot(p.astype(BF16), vsel, preferred_element_type=F32)
                outs.append(pv / denom)
            att_ref[r * BLOCK:(r + 1) * BLOCK, pair * PAIR_W:(pair + 1) * PAIR_W] = (
                jnp.where(even_lanes, outs[0], outs[1]).astype(BF16))
    y_attn = jnp.dot(att_ref[...], wao_ref[...], preferred_element_type=F32)

    mixed = (gc_ref[...].astype(F32) * y_conv + ga_ref[...].astype(F32) * y_attn).astype(BF16)
    x1 = x_ref[...] + jnp.dot(mixed, wo_ref[...], preferred_element_type=F32)
    x1_ref[...] = x1

    ms = jnp.mean(x1 * x1, axis=-1, keepdims=True)
    h2 = ((x1 * lax.rsqrt(ms + EPS)) * g2_ref[...]).astype(BF16)
    h2_ref[...] = h2
    pq_ref[...] = jnp.dot(h2, wq_ref[...], preferred_element_type=F32)


def _mixer_call(x2, hc, bc, qn, kv, gc, ga, w_conv, biasm, sinks, w_conv_out, w_attn_out, w_out,
                norm_ffn, w_query, bsz, s, ts):
    t = bsz * s
    qd = w_query.shape[1]
    nj = s // ts
    row = lambda c: pl.BlockSpec((ts, c), lambda b, j: (b * nj + j, 0))
    prev_blk = lambda c: pl.BlockSpec((BLOCK, c), lambda b, j: (jnp.maximum((b * nj + j) * (ts // BLOCK) - 1, 0), 0))
    prev8 = pl.BlockSpec((SUBLANES, CONV_DIM),
                         lambda b, j: (jnp.maximum((b * nj + j) * (ts // SUBLANES) - 1, 0), 0))
    full = lambda shape: pl.BlockSpec(shape, lambda b, j: (0,) * len(shape))
    return pl.pallas_call(
        functools.partial(_mixer_kernel, ts=ts),
        grid=(bsz, nj),
        in_specs=[row(D_MODEL), row(CONV_DIM), prev8, row(CONV_DIM), row(Q_DIM),
                  row(3 * KV_WIDE), prev_blk(3 * KV_WIDE), row(D_MODEL), row(D_MODEL),
                  full((CONV_K, CONV_DIM)), full((N_HEADS, BLOCK, 2 * BLOCK)),
                  pl.BlockSpec(memory_space=pltpu.SMEM),
                  full((CONV_DIM, D_MODEL)), full((Q_DIM, D_MODEL)), full((D_MODEL, D_MODEL)),
                  full((1, D_MODEL)), full((D_MODEL, qd))],
        out_specs=[row(D_MODEL), row(D_MODEL), row(qd)],
        out_shape=[jax.ShapeDtypeStruct((t, D_MODEL), F32), jax.ShapeDtypeStruct((t, D_MODEL), BF16),
                   jax.ShapeDtypeStruct((t, qd), F32)],
        scratch_shapes=[pltpu.VMEM((ts, Q_DIM), BF16)],
        compiler_params=pltpu.CompilerParams(dimension_semantics=("arbitrary", "arbitrary"),
                                             vmem_limit_bytes=DENSE_VMEM_LIMIT),
        name="peer_block_mixer",
    )(x2, hc, hc, bc, qn, kv, kv, gc, ga, w_conv.astype(F32), biasm, sinks.astype(F32),
      w_conv_out.astype(BF16), w_attn_out.astype(BF16), w_out.astype(BF16),
      norm_ffn.reshape(1, D_MODEL).astype(F32), w_query.astype(BF16))


def _band_bias(rel_bias):
    q_loc = jnp.arange(BLOCK, dtype=jnp.int32)[:, None]
    k_loc = jnp.arange(2 * BLOCK, dtype=jnp.int32)[None, :]
    dist = q_loc + BLOCK - k_loc
    max_exact = N_BUCKETS // 2
    d = jnp.maximum(dist, 0)
    df = jnp.maximum(d, 1).astype(F32)
    large = max_exact + (jnp.log(df / max_exact) / math.log(MAX_DISTANCE / max_exact)
                         * (N_BUCKETS - max_exact)).astype(jnp.int32)
    large = jnp.minimum(large, N_BUCKETS - 1)
    bucket = jnp.where(d < max_exact, d, large)
    onehot = (bucket[None] == jnp.arange(N_BUCKETS, dtype=jnp.int32)[:, None, None]).astype(F32)
    bias = jnp.einsum("bh,bqk->hqk", rel_bias.astype(F32), onehot, precision=lax.Precision.HIGHEST)
    valid = (dist >= 0) & (dist < WINDOW)
    return jnp.where(valid[None], bias, NEG_BIG)


def _top16(vals, rid, big):
    tv, ti = [], []
    for _ in range(PEER_TOPK):
        m = jnp.max(vals, axis=0, keepdims=True)
        idx = jnp.min(jnp.where(vals == m, rid, big), axis=0, keepdims=True)
        tv.append(m)
        ti.append(idx)
        vals = jnp.where(rid == idx, -jnp.inf, vals)
    return jnp.concatenate(tv, axis=0), jnp.concatenate(ti, axis=0)


def _row_iota(n_rows):
    return lax.broadcasted_iota(jnp.int32, (n_rows, LANES), 0).astype(F32)


def _pair_candidates(v0, v1):
    r8 = _row_iota(SUBLANES)
    r16 = _row_iota(PEER_TOPK)
    vals = [v0[0:1, :] + v1]
    pos = [r16]
    for a in (1, 2, 3):
        vals.append(v0[a:a + 1, :] + v1[0:SUBLANES, :])
        pos.append(r8 + float(a * PEER_TOPK))
    low = r8 < 4.0
    v1dup = jnp.where(low, v1[0:SUBLANES, :], pltpu.roll(v1[0:SUBLANES, :], 4, axis=0))
    bdup = jnp.where(low, r8, r8 - 4.0)
    for a in (4, 6):
        vals.append(jnp.where(low, v0[a:a + 1, :], v0[a + 1:a + 2, :]) + v1dup)
        pos.append(jnp.where(low, float(a * PEER_TOPK), float((a + 1) * PEER_TOPK)) + bdup)
    vals.append(v0[SUBLANES:, :] + v1[0:1, :])
    pos.append((r8 + float(SUBLANES)) * float(PEER_TOPK))
    return jnp.concatenate(vals, axis=0), jnp.concatenate(pos, axis=0)


def _pick(table, sel):
    out = jnp.zeros_like(sel)
    for a in range(PEER_TOPK):
        out = jnp.where(sel == float(a), table[a:a + 1, :], out)
    return out


def _select_pairs(top0, top1):
    (v0, j0), (v1, j1) = top0, top1
    cand, cpos = _pair_candidates(v0, v1)
    ts_, pos = _top16(cand, cpos, float(PEER_TOPK * PEER_TOPK))
    pa = jnp.floor(pos * (1.0 / PEER_TOPK))
    pb = pos - pa * PEER_TOPK
    e = jnp.exp(ts_ - ts_[0:1, :])
    return _pick(j0, pa), _pick(j1, pb), e / jnp.sum(e, axis=0, keepdims=True)


def _gate_tiles(gd_ref, tg, i0b, i1b, gh, gl, rid):
    zero = jnp.zeros_like(rid)
    one = jnp.ones_like(rid)
    for s in range(SUBLANES):
        m0 = rid == i0b[s:s + 1, :]
        rh = jnp.where(m0, gh[s:s + 1, :], zero)
        rl = jnp.where(m0, gl[s:s + 1, :], zero)
        ct = jnp.where(rid == i1b[s:s + 1, :], one, zero)
        gd2 = lax.dot_general(jnp.concatenate([rh, rl], axis=0), ct, _NT, preferred_element_type=F32)
        gd = gd2[:N_KEYS] + gd2[N_KEYS:]
        gd_ref[tg, pl.ds(s, N_KEYS, stride=SUBLANES), :] = gd


def _peer_kernel(x1_ref, h2_ref, q_ref, kh_ref, kl_ref, u_ref, v_ref, out_ref, gd_ref, rs_ref, tok_ref, *, tb, ec):
    ib = pl.program_id(0)
    c = pl.program_id(1)
    half_tok = tb // 2
    n_lg = half_tok // LANES
    n_pieces = ec // EXPERT_PIECE
    grp_per_piece = EXPERT_PIECE // N_KEYS
    w_slot = ib % 2
    r_slot = 1 - w_slot

    @pl.when(jnp.logical_and(ib == 0, c == 0))
    def _no_block_before_the_first():
        tok_ref[r_slot] = jnp.zeros(tok_ref.shape[1:], F32)

    @pl.when(c == 0)
    def _build_gate_matrix():
        rid = lax.broadcasted_iota(jnp.int32, (N_KEYS, N_SEL), 0).astype(F32).astype(BF16)

        def body(it, carry):
            for k in range(GATE_GROUPS_PER_ITER):
                tg = it * GATE_GROUPS_PER_ITER + k
                r0 = pl.multiple_of(tg * SUBLANES, SUBLANES)
                gh, gl = _split_bf16(0.5 * tok_ref[r_slot, 2, pl.ds(r0, SUBLANES), :])
                _gate_tiles(gd_ref, tg, tok_ref[r_slot, 0, pl.ds(r0, SUBLANES), :].astype(BF16),
                            tok_ref[r_slot, 1, pl.ds(r0, SUBLANES), :].astype(BF16), gh, gl, rid)
            return carry

        lax.fori_loop(0, tb // (SUBLANES * GATE_GROUPS_PER_ITER), body, 0)
        out_ref[...] = x1_ref[...]

    hd = c // 2
    half = c % 2
    base = pl.multiple_of(c * (ec // N_KEYS * SUBLANES), SUBLANES)
    h2 = h2_ref[...]

    def scores(part):
        qh_, ql_ = _split_bf16(q_ref[:, part * PEER_HALF:(part + 1) * PEER_HALF])
        kh, kl = kh_ref[hd * 2 + part], kl_ref[hd * 2 + part]
        return (lax.dot_general(kh, qh_, _NT, preferred_element_type=F32)
                + lax.dot_general(kh, ql_, _NT, preferred_element_type=F32)
                + lax.dot_general(kl, qh_, _NT, preferred_element_type=F32))

    def weights(j):
        a = lax.dot_general(h2, u_ref[j * EXPERT_PIECE:(j + 1) * EXPERT_PIECE, :], _NT, preferred_element_type=F32)
        gd = jnp.concatenate(
            [gd_ref[:, pl.ds(base + (grp_per_piece * j + g) * SUBLANES, SUBLANES), :].reshape(tb, N_KEYS)
             for g in range(grp_per_piece)], axis=1)
        return (gd * (a * (1.0 + lax.erf(a * (2.0 ** -0.5))))).astype(BF16)

    tops = [[None] * n_lg, [None] * n_lg]
    w_pieces = []
    for part in range(2):
        st = scores(part)
        for lg in range(n_lg):
            tops[part][lg] = _top16(st[:, lg * LANES:(lg + 1) * LANES], _row_iota(N_KEYS), float(N_KEYS))
            w_pieces.extend(weights(j) for j in range(len(w_pieces), min(len(w_pieces) + 2, n_pieces)))
    w_pieces.extend(weights(j) for j in range(len(w_pieces), n_pieces))
    sel = [_select_pairs(tops[0][lg], tops[1][lg]) for lg in range(n_lg)]
    rows = pl.ds(pl.multiple_of(hd * PEER_TOPK, PEER_TOPK), PEER_TOPK)
    for k in range(3):
        rs_ref[w_slot, half, k, rows, :] = jnp.concatenate([s_[k] for s_ in sel], axis=1)
    out_ref[...] += jnp.dot(jnp.concatenate(w_pieces, axis=1), v_ref[...], preferred_element_type=F32)

    @pl.when(c == pl.num_programs(1) - 1)
    def _routing_to_token_major():
        for hf in range(2):
            for k in range(3):
                tok_ref[w_slot, k, hf * half_tok:(hf + 1) * half_tok, :] = rs_ref[w_slot, hf, k].T


def _peer_call(x1, h2, pq, sub_keys, expert_u, expert_v, tb, ec):
    t = x1.shape[0]
    nb = t // tb
    nc = N_EXPERTS // ec
    assert nc == 2 * PEER_HEADS and tb % (2 * LANES) == 0 and ec % EXPERT_PIECE == 0
    keys = sub_keys.astype(F32).reshape(PEER_HEADS * 2, N_KEYS, PEER_HALF)
    kh, kl = _split_bf16(keys)
    prev_row = lambda c_: pl.BlockSpec((tb, c_), lambda i, c: (jnp.maximum(i - 1, 0), 0))
    chunk = pl.BlockSpec((ec, D_MODEL), lambda i, c: (c, 0))
    q_spec = pl.BlockSpec((tb // 2, 2 * PEER_HALF), lambda i, c: (jnp.minimum(i, nb - 1) * 2 + c % 2, c // 2))
    full3 = pl.BlockSpec((PEER_HEADS * 2, N_KEYS, PEER_HALF), lambda i, c: (0, 0, 0))
    return pl.pallas_call(
        functools.partial(_peer_kernel, tb=tb, ec=ec),
        grid=(nb + 1, nc),
        in_specs=[prev_row(D_MODEL), prev_row(D_MODEL), q_spec, full3, full3, chunk, chunk],
        out_specs=prev_row(D_MODEL),
        out_shape=jax.ShapeDtypeStruct((t, D_MODEL), F32),
        scratch_shapes=[pltpu.VMEM((tb // SUBLANES, N_KEYS * SUBLANES, N_KEYS), F32),
                        pltpu.VMEM((2, 2, 3, N_SEL, tb // 2), F32),
                        pltpu.VMEM((2, 3, tb, N_SEL), F32)],
        compiler_params=pltpu.CompilerParams(dimension_semantics=("arbitrary", "arbitrary"),
                                             vmem_limit_bytes=PEER_VMEM_LIMIT),
        name="peer_block_experts",
    )(x1, h2, pq, kh, kl, expert_u.astype(BF16), expert_v.astype(BF16))


def _tile_sizes(bsz, s):
    t = bsz * s
    tm = math.gcd(t, 512)
    ts = math.gcd(s, 512)
    tb = math.gcd(t, 512)
    return tm, ts, tb


def kernel(x, norm_mix, norm_ffn, w_in, b_gate, w_conv, q_norm, k_norm, sinks, rel_bias, w_conv_out, w_attn_out,
           w_out, w_query, sub_keys, expert_u, expert_v):
    bsz, s, d = x.shape
    assert d == D_MODEL and s % BLOCK == 0
    tm, ts, tb = _tile_sizes(bsz, s)
    biasm = _band_bias(rel_bias)
    x2 = x.reshape(bsz * s, d)
    for l in range(norm_mix.shape[0]):
        hc, bc, qn, kv, gc, ga = _proj_call(x2, norm_mix[l], w_in[l], b_gate[l], q_norm[l], k_norm[l], tm)
        x1, h2, pq = _mixer_call(x2, hc, bc, qn, kv, gc, ga, w_conv[l], biasm, sinks[l], w_conv_out[l],
                                 w_attn_out[l], w_out[l], norm_ffn[l], w_query[l], bsz, s, ts)
        x2 = _peer_call(x1, h2, pq, sub_keys[l], expert_u[l], expert_v[l], tb, EXPERT_CHUNK)
    return x2.reshape(bsz, s, d)
```

```python
import functools
import math

import jax
import jax.numpy as jnp
import numpy as np
from jax import lax
from jax.experimental import pallas as pl
from jax.experimental.pallas import tpu as pltpu

F32 = jnp.float32
BF16 = jnp.bfloat16

D_MODEL = 1024
CONV_DIM = 512
CONV_K = 3
N_HEADS = 8
N_KV_HEADS = 2
HEAD_DIM = 64
Q_DIM = N_HEADS * HEAD_DIM
KV_DIM = N_KV_HEADS * HEAD_DIM
GROUP = N_HEADS // N_KV_HEADS
WINDOW = 128
BLOCK = 128
N_BUCKETS = 32
MAX_DISTANCE = 128
PEER_HEADS = 8
N_KEYS = 128
N_EXPERTS = N_KEYS * N_KEYS
PEER_QDIM = 256
PEER_HALF = PEER_QDIM // 2
PEER_TOPK = 16
N_SEL = PEER_HEADS * PEER_TOPK
EPS = 1e-6
NEG_BIG = -1e30
PAIR_W = 2 * HEAD_DIM
KV_WIDE = N_KV_HEADS * PAIR_W

SUBLANES = 8
LANES = 128

EXPERT_CHUNK = N_EXPERTS // (2 * PEER_HEADS)
EXPERT_PIECE = 2 * N_KEYS
TOKENS_PER_GATE_GROUP = 2 * SUBLANES
GATE_GROUPS_PER_ITER = 4
PEER_VMEM_LIMIT = 60 * 1024 * 1024
DENSE_VMEM_LIMIT = 56 * 1024 * 1024

_OFFS = np.cumsum([0, CONV_DIM, CONV_DIM, CONV_DIM, Q_DIM, KV_DIM, KV_DIM, D_MODEL, D_MODEL]).tolist()

_NT = (((1,), (1,)), ((), ()))


def _split_bf16(a):
    hi = a.astype(BF16)
    lo = (a - hi.astype(F32)).astype(BF16)
    return hi, lo


def _proj_kernel(x_ref, g_ref, w_ref, bg_ref, qg_ref, kg_ref, avq_ref, avk_ref, exp_ref,
                 hc_ref, bc_ref, qn_ref, kv_ref, gc_ref, ga_ref):
    x = x_ref[...]
    ms = jnp.mean(x * x, axis=-1, keepdims=True)
    h = ((x * lax.rsqrt(ms + EPS)) * g_ref[...]).astype(BF16)

    def seg(i):
        return jnp.dot(h, w_ref[:, _OFFS[i]:_OFFS[i + 1]], preferred_element_type=F32)

    u = seg(0)
    hc_ref[...] = seg(2) * u
    bc_ref[...] = seg(1)

    def head_rms(a, av_ref):
        hi, lo = _split_bf16(a * a)
        return (jnp.dot(hi, av_ref[...], preferred_element_type=F32)
                + jnp.dot(lo, av_ref[...], preferred_element_type=F32))

    q = seg(3)
    qn = (q * lax.rsqrt(head_rms(q, avq_ref) + EPS)) * qg_ref[...] * (HEAD_DIM ** -0.5)
    qn_ref[...] = qn.astype(BF16)

    k = seg(4)
    kn = ((k * lax.rsqrt(head_rms(k, avk_ref) + EPS)) * kg_ref[...]).astype(BF16)
    v = seg(5).astype(BF16)
    kv_ref[...] = jnp.dot(jnp.concatenate([kn, v], axis=1), exp_ref[...], preferred_element_type=F32).astype(BF16)

    bg = bg_ref[...]
    gc_ref[...] = jax.nn.sigmoid(seg(6) + bg[:, :D_MODEL]).astype(BF16)
    ga_ref[...] = jax.nn.sigmoid(seg(7) + bg[:, D_MODEL:]).astype(BF16)


def _kv_layout():
    e = np.zeros((2 * KV_DIM, 3 * KV_WIDE), np.float32)
    for g in range(N_KV_HEADS):
        for d in range(HEAD_DIM):
            e[g * HEAD_DIM + d, g * PAIR_W + d] = 1.0
            e[g * HEAD_DIM + d, KV_WIDE + g * PAIR_W + HEAD_DIM + d] = 1.0
            e[KV_DIM + g * HEAD_DIM + d, 2 * KV_WIDE + g * PAIR_W + d] = 1.0
            e[KV_DIM + g * HEAD_DIM + d, 2 * KV_WIDE + g * PAIR_W + HEAD_DIM + d] = 1.0
    return jnp.asarray(e, BF16)


def _proj_call(x2, norm_mix, w_in, b_gate, q_norm, k_norm, tm):
    t = x2.shape[0]
    in_dim = w_in.shape[1]
    avq = jnp.kron(jnp.eye(N_HEADS, dtype=F32), jnp.full((HEAD_DIM, HEAD_DIM), 1.0 / HEAD_DIM, F32)).astype(BF16)
    avk = jnp.kron(jnp.eye(N_KV_HEADS, dtype=F32), jnp.full((HEAD_DIM, HEAD_DIM), 1.0 / HEAD_DIM, F32)).astype(BF16)
    qg = jnp.tile(q_norm.astype(F32), N_HEADS).reshape(1, Q_DIM)
    kg = jnp.tile(k_norm.astype(F32), N_KV_HEADS).reshape(1, KV_DIM)
    full = lambda shape: pl.BlockSpec(shape, lambda i: (0,) * len(shape))
    row = lambda c: pl.BlockSpec((tm, c), lambda i: (i, 0))
    return pl.pallas_call(
        _proj_kernel,
        grid=(t // tm,),
        in_specs=[row(D_MODEL), full((1, D_MODEL)), full((D_MODEL, in_dim)), full((1, 2 * D_MODEL)),
                  full((1, Q_DIM)), full((1, KV_DIM)), full((Q_DIM, Q_DIM)), full((KV_DIM, KV_DIM)),
                  full((2 * KV_DIM, 3 * KV_WIDE))],
        out_specs=[row(CONV_DIM), row(CONV_DIM), row(Q_DIM), row(3 * KV_WIDE), row(D_MODEL), row(D_MODEL)],
        out_shape=[jax.ShapeDtypeStruct((t, CONV_DIM), F32), jax.ShapeDtypeStruct((t, CONV_DIM), F32),
                   jax.ShapeDtypeStruct((t, Q_DIM), BF16), jax.ShapeDtypeStruct((t, 3 * KV_WIDE), BF16),
                   jax.ShapeDtypeStruct((t, D_MODEL), BF16), jax.ShapeDtypeStruct((t, D_MODEL), BF16)],
        compiler_params=pltpu.CompilerParams(dimension_semantics=("arbitrary",),
                                             vmem_limit_bytes=DENSE_VMEM_LIMIT),
        name="peer_block_proj",
    )(x2, norm_mix.reshape(1, D_MODEL).astype(F32), w_in.astype(BF16), b_gate.reshape(1, 2 * D_MODEL).astype(F32),
      qg, kg, avq, avk, _kv_layout())


def _mixer_kernel(x_ref, hc_ref, hcp_ref, bc_ref, qn_ref, kv_ref, kvp_ref, gc_ref, ga_ref,
                  wc_ref, bias_ref, sink_ref, wco_ref, wao_ref, wo_ref, g2_ref, wq_ref,
                  x1_ref, h2_ref, pq_ref, att_ref, *, ts):
    j = pl.program_id(1)
    first = j == 0

    hc = hc_ref[...]
    prev = jnp.where(first, 0.0, hcp_ref[...])
    rows = lax.broadcasted_iota(jnp.int32, hc.shape, 0)
    s1 = pltpu.roll(hc, 1, axis=0)
    s1 = jnp.where(rows == 0, prev[SUBLANES - 1:SUBLANES, :], s1)
    s2 = pltpu.roll(hc, 2, axis=0)
    s2 = jnp.where(rows == 0, prev[SUBLANES - 2:SUBLANES - 1, :], s2)
    s2 = jnp.where(rows == 1, prev[SUBLANES - 1:SUBLANES, :], s2)
    wc = wc_ref[...]
    conv = s2 * wc[0:1, :] + s1 * wc[1:2, :] + hc * wc[2:3, :]
    yc = (bc_ref[...] * conv).astype(BF16)
    y_conv = jnp.dot(yc, wco_ref[...], preferred_element_type=F32)

    kvfull = jnp.concatenate([kvp_ref[...], kv_ref[...]], axis=0)
    col = lax.broadcasted_iota(jnp.int32, (BLOCK, 2 * BLOCK), 1)
    pen0 = jnp.where(jnp.logical_and(first, col < BLOCK), NEG_BIG, 0.0)
    even_lanes = lax.broadcasted_iota(jnp.int32, (BLOCK, PAIR_W), 1) < HEAD_DIM
    for r in range(ts // BLOCK):
        kvb = kvfull[r * BLOCK:(r + 2) * BLOCK, :]
        for pair in range(N_HEADS // 2):
            g = (2 * pair) // GROUP
            qs = qn_ref[r * BLOCK:(r + 1) * BLOCK, pair * PAIR_W:(pair + 1) * PAIR_W]
            vsel = kvb[:, 2 * KV_WIDE + g * PAIR_W:2 * KV_WIDE + (g + 1) * PAIR_W]
            outs = []
            for par in range(2):
                hd = 2 * pair + par
                ksel = kvb[:, par * KV_WIDE + g * PAIR_W:par * KV_WIDE + (g + 1) * PAIR_W]
                lg = lax.dot_general(qs, ksel, _NT, preferred_element_type=F32) + bias_ref[hd]
                if r == 0:
                    lg = lg + pen0
                sink = sink_ref[hd]
                m = jnp.maximum(jnp.max(lg, axis=-1, keepdims=True), sink)
                p = jnp.exp(lg - m)
                denom = jnp.sum(p, axis=-1, keepdims=True) + jnp.exp(sink - m)
                pv = jnp.dot(p.astype(BF16), vsel, preferred_element_type=F32)
                outs.append(pv / denom)
            att_ref[r * BLOCK:(r + 1) * BLOCK, pair * PAIR_W:(pair + 1) * PAIR_W] = (
                jnp.where(even_lanes, outs[0], outs[1]).astype(BF16))
    y_attn = jnp.dot(att_ref[...], wao_ref[...], preferred_element_type=F32)

    mixed = (gc_ref[...].astype(F32) * y_conv + ga_ref[...].astype(F32) * y_attn).astype(BF16)
    x1 = x_ref[...] + jnp.dot(mixed, wo_ref[...], preferred_element_type=F32)
    x1_ref[...] = x1

    ms = jnp.mean(x1 * x1, axis=-1, keepdims=True)
    h2 = ((x1 * lax.rsqrt(ms + EPS)) * g2_ref[...]).astype(BF16)
    h2_ref[...] = h2
    pq_ref[...] = jnp.dot(h2, wq_ref[...], preferred_element_type=F32)


def _mixer_call(x2, hc, bc, qn, kv, gc, ga, w_conv, biasm, sinks, w_conv_out, w_attn_out, w_out,
                norm_ffn, w_query, bsz, s, ts):
    t = bsz * s
    qd = w_query.shape[1]
    nj = s // ts
    row = lambda c: pl.BlockSpec((ts, c), lambda b, j: (b * nj + j, 0))
    prev_blk = lambda c: pl.BlockSpec((BLOCK, c), lambda b, j: (jnp.maximum((b * nj + j) * (ts // BLOCK) - 1, 0), 0))
    prev8 = pl.BlockSpec((SUBLANES, CONV_DIM),
                         lambda b, j: (jnp.maximum((b * nj + j) * (ts // SUBLANES) - 1, 0), 0))
    full = lambda shape: pl.BlockSpec(shape, lambda b, j: (0,) * len(shape))
    return pl.pallas_call(
        functools.partial(_mixer_kernel, ts=ts),
        grid=(bsz, nj),
        in_specs=[row(D_MODEL), row(CONV_DIM), prev8, row(CONV_DIM), row(Q_DIM),
                  row(3 * KV_WIDE), prev_blk(3 * KV_WIDE), row(D_MODEL), row(D_MODEL),
                  full((CONV_K, CONV_DIM)), full((N_HEADS, BLOCK, 2 * BLOCK)),
                  pl.BlockSpec(memory_space=pltpu.SMEM),
                  full((CONV_DIM, D_MODEL)), full((Q_DIM, D_MODEL)), full((D_MODEL, D_MODEL)),
                  full((1, D_MODEL)), full((D_MODEL, qd))],
        out_specs=[row(D_MODEL), row(D_MODEL), row(qd)],
        out_shape=[jax.ShapeDtypeStruct((t, D_MODEL), F32), jax.ShapeDtypeStruct((t, D_MODEL), BF16),
                   jax.ShapeDtypeStruct((t, qd), F32)],
        scratch_shapes=[pltpu.VMEM((ts, Q_DIM), BF16)],
        compiler_params=pltpu.CompilerParams(dimension_semantics=("arbitrary", "arbitrary"),
                                             vmem_limit_bytes=DENSE_VMEM_LIMIT),
        name="peer_block_mixer",
    )(x2, hc, hc, bc, qn, kv, kv, gc, ga, w_conv.astype(F32), biasm, sinks.astype(F32),
      w_conv_out.astype(BF16), w_attn_out.astype(BF16), w_out.astype(BF16),
      norm_ffn.reshape(1, D_MODEL).astype(F32), w_query.astype(BF16))


def _band_bias(rel_bias):
    q_loc = jnp.arange(BLOCK, dtype=jnp.int32)[:, None]
    k_loc = jnp.arange(2 * BLOCK, dtype=jnp.int32)[None, :]
    dist = q_loc + BLOCK - k_loc
    max_exact = N_BUCKETS // 2
    d = jnp.maximum(dist, 0)
    df = jnp.maximum(d, 1).astype(F32)
    large = max_exact + (jnp.log(df / max_exact) / math.log(MAX_DISTANCE / max_exact)
                         * (N_BUCKETS - max_exact)).astype(jnp.int32)
    large = jnp.minimum(large, N_BUCKETS - 1)
    bucket = jnp.where(d < max_exact, d, large)
    onehot = (bucket[None] == jnp.arange(N_BUCKETS, dtype=jnp.int32)[:, None, None]).astype(F32)
    bias = jnp.einsum("bh,bqk->hqk", rel_bias.astype(F32), onehot, precision=lax.Precision.HIGHEST)
    valid = (dist >= 0) & (dist < WINDOW)
    return jnp.where(valid[None], bias, NEG_BIG)


def _top16(vals, rid, big):
    tv, ti = [], []
    for _ in range(PEER_TOPK):
        m = jnp.max(vals, axis=0, keepdims=True)
        idx = jnp.min(jnp.where(vals == m, rid, big), axis=0, keepdims=True)
        tv.append(m)
        ti.append(idx)
        vals = jnp.where(rid == idx, -jnp.inf, vals)
    return jnp.concatenate(tv, axis=0), jnp.concatenate(ti, axis=0)


def _row_iota(n_rows):
    return lax.broadcasted_iota(jnp.int32, (n_rows, LANES), 0).astype(F32)


def _pair_candidates(v0, v1):
    r8 = _row_iota(SUBLANES)
    r16 = _row_iota(PEER_TOPK)
    vals = [v0[0:1, :] + v1]
    pos = [r16]
    for a in (1, 2, 3):
        vals.append(v0[a:a + 1, :] + v1[0:SUBLANES, :])
        pos.append(r8 + float(a * PEER_TOPK))
    low = r8 < 4.0
    v1dup = jnp.where(low, v1[0:SUBLANES, :], pltpu.roll(v1[0:SUBLANES, :], 4, axis=0))
    bdup = jnp.where(low, r8, r8 - 4.0)
    for a in (4, 6):
        vals.append(jnp.where(low, v0[a:a + 1, :], v0[a + 1:a + 2, :]) + v1dup)
        pos.append(jnp.where(low, float(a * PEER_TOPK), float((a + 1) * PEER_TOPK)) + bdup)
    vals.append(v0[SUBLANES:, :] + v1[0:1, :])
    pos.append((r8 + float(SUBLANES)) * float(PEER_TOPK))
    return jnp.concatenate(vals, axis=0), jnp.concatenate(pos, axis=0)


def _pick(table, sel):
    out = jnp.zeros_like(sel)
    for a in range(PEER_TOPK):
        out = jnp.where(sel == float(a), table[a:a + 1, :], out)
    return out


def _select_pairs(top0, top1):
    (v0, j0), (v1, j1) = top0, top1
    cand, cpos = _pair_candidates(v0, v1)
    ts_, pos = _top16(cand, cpos, float(PEER_TOPK * PEER_TOPK))
    pa = jnp.floor(pos * (1.0 / PEER_TOPK))
    pb = pos - pa * PEER_TOPK
    e = jnp.exp(ts_ - ts_[0:1, :])
    return _pick(j0, pa), _pick(j1, pb), e / jnp.sum(e, axis=0, keepdims=True)


def _gate_tiles(gd_ref, tg, i0b, i1b, gb, rid):
    zero = jnp.zeros_like(rid)
    one = jnp.ones_like(rid)
    for p in range(SUBLANES):
        tiles = []
        for t in (2 * p, 2 * p + 1):
            rt = jnp.where(rid == i0b[t:t + 1, :], gb[t:t + 1, :], zero)
            ct = jnp.where(rid == i1b[t:t + 1, :], one, zero)
            tiles.append(lax.dot_general(rt, ct, _NT, preferred_element_type=F32))
        lo_, hi_ = (pltpu.bitcast(t_.astype(BF16).astype(F32), jnp.uint32) for t_ in tiles)
        gd_ref[tg, pl.ds(p, N_KEYS, stride=SUBLANES), :] = (
            lax.shift_right_logical(lo_, jnp.uint32(16)) | (hi_ & jnp.uint32(0xFFFF0000)))


def _peer_kernel(x1_ref, h2_ref, q_ref, kh_ref, kl_ref, u_ref, v_ref, out_ref, gd_ref, rs_ref, tok_ref, *, tb, ec):
    ib = pl.program_id(0)
    c = pl.program_id(1)
    half_tok = tb // 2
    n_lg = half_tok // LANES
    n_pieces = ec // EXPERT_PIECE
    grp_per_piece = EXPERT_PIECE // N_KEYS
    w_slot = ib % 2
    r_slot = 1 - w_slot

    @pl.when(jnp.logical_and(ib == 0, c == 0))
    def _no_block_before_the_first():
        tok_ref[r_slot] = jnp.zeros(tok_ref.shape[1:], F32)

    @pl.when(c == 0)
    def _build_gate_matrix():
        rid = lax.broadcasted_iota(jnp.int32, (N_KEYS, N_SEL), 0).astype(F32).astype(BF16)

        def body(it, carry):
            for k in range(GATE_GROUPS_PER_ITER):
                tg = it * GATE_GROUPS_PER_ITER + k
                r0 = pl.multiple_of(tg * TOKENS_PER_GATE_GROUP, TOKENS_PER_GATE_GROUP)
                rows = pl.ds(r0, TOKENS_PER_GATE_GROUP)
                _gate_tiles(gd_ref, tg, tok_ref[r_slot, 0, rows, :].astype(BF16),
                            tok_ref[r_slot, 1, rows, :].astype(BF16),
                            (0.5 * tok_ref[r_slot, 2, rows, :]).astype(BF16), rid)
            return carry

        lax.fori_loop(0, tb // (TOKENS_PER_GATE_GROUP * GATE_GROUPS_PER_ITER), body, 0)
        out_ref[...] = x1_ref[...]

    hd = c // 2
    half = c % 2
    base = pl.multiple_of(c * (ec // N_KEYS * SUBLANES), SUBLANES)
    h2 = h2_ref[...]

    def scores(part):
        qh_, ql_ = _split_bf16(q_ref[:, part * PEER_HALF:(part + 1) * PEER_HALF])
        kh, kl = kh_ref[hd * 2 + part], kl_ref[hd * 2 + part]
        return (lax.dot_general(kh, qh_, _NT, preferred_element_type=F32)
                + lax.dot_general(kh, ql_, _NT, preferred_element_type=F32)
                + lax.dot_general(kl, qh_, _NT, preferred_element_type=F32))

    def weights(j):
        a = lax.dot_general(h2, u_ref[j * EXPERT_PIECE:(j + 1) * EXPERT_PIECE, :], _NT, preferred_element_type=F32)
        gd = jnp.concatenate(
            [pltpu.bitcast(gd_ref[:, pl.ds(base + (grp_per_piece * j + g) * SUBLANES, SUBLANES), :]
                           .reshape(tb // 2, N_KEYS), BF16) for g in range(grp_per_piece)], axis=1)
        return gd * (a * (1.0 + lax.erf(a * (2.0 ** -0.5)))).astype(BF16)

    tops = [[None] * n_lg, [None] * n_lg]
    w_pieces = []
    for part in range(2):
        st = scores(part)
        for lg in range(n_lg):
            tops[part][lg] = _top16(st[:, lg * LANES:(lg + 1) * LANES], _row_iota(N_KEYS), float(N_KEYS))
            w_pieces.extend(weights(j) for j in range(len(w_pieces), min(len(w_pieces) + 2, n_pieces)))
    w_pieces.extend(weights(j) for j in range(len(w_pieces), n_pieces))
    sel = [_select_pairs(tops[0][lg], tops[1][lg]) for lg in range(n_lg)]
    rows = pl.ds(pl.multiple_of(hd * PEER_TOPK, PEER_TOPK), PEER_TOPK)
    for k in range(3):
        rs_ref[w_slot, half, k, rows, :] = jnp.concatenate([s_[k] for s_ in sel], axis=1)
    out_ref[...] += jnp.dot(jnp.concatenate(w_pieces, axis=1), v_ref[...], preferred_element_type=F32)

    @pl.when(c == pl.num_programs(1) - 1)
    def _routing_to_token_major():
        for hf in range(2):
            for k in range(3):
                tok_ref[w_slot, k, hf * half_tok:(hf + 1) * half_tok, :] = rs_ref[w_slot, hf, k].T


def _peer_call(x1, h2, pq, sub_keys, expert_u, expert_v, tb, ec):
    t = x1.shape[0]
    nb = t // tb
    nc = N_EXPERTS // ec
    assert nc == 2 * PEER_HEADS and tb % (2 * LANES) == 0 and ec % EXPERT_PIECE == 0
    keys = sub_keys.astype(F32).reshape(PEER_HEADS * 2, N_KEYS, PEER_HALF)
    kh, kl = _split_bf16(keys)
    prev_row = lambda c_: pl.BlockSpec((tb, c_), lambda i, c: (jnp.maximum(i - 1, 0), 0))
    chunk = pl.BlockSpec((ec, D_MODEL), lambda i, c: (c, 0))
    q_spec = pl.BlockSpec((tb // 2, 2 * PEER_HALF), lambda i, c: (jnp.minimum(i, nb - 1) * 2 + c % 2, c // 2))
    full3 = pl.BlockSpec((PEER_HEADS * 2, N_KEYS, PEER_HALF), lambda i, c: (0, 0, 0))
    return pl.pallas_call(
        functools.partial(_peer_kernel, tb=tb, ec=ec),
        grid=(nb + 1, nc),
        in_specs=[prev_row(D_MODEL), prev_row(D_MODEL), q_spec, full3, full3, chunk, chunk],
        out_specs=prev_row(D_MODEL),
        out_shape=jax.ShapeDtypeStruct((t, D_MODEL), F32),
        scratch_shapes=[pltpu.VMEM((tb // TOKENS_PER_GATE_GROUP, N_KEYS * SUBLANES, N_KEYS), jnp.uint32),
                        pltpu.VMEM((2, 2, 3, N_SEL, tb // 2), F32),
                        pltpu.VMEM((2, 3, tb, N_SEL), F32)],
        compiler_params=pltpu.CompilerParams(dimension_semantics=("arbitrary", "arbitrary"),
                                             vmem_limit_bytes=PEER_VMEM_LIMIT),
        name="peer_block_experts",
    )(x1, h2, pq, kh, kl, expert_u.astype(BF16), expert_v.astype(BF16))


def _tile_sizes(bsz, s):
    t = bsz * s
    tm = math.gcd(t, 512)
    ts = math.gcd(s, 512)
    tb = math.gcd(t, 512)
    return tm, ts, tb


def kernel(x, norm_mix, norm_ffn, w_in, b_gate, w_conv, q_norm, k_norm, sinks, rel_bias, w_conv_out, w_attn_out,
           w_out, w_query, sub_keys, expert_u, expert_v):
    bsz, s, d = x.shape
    assert d == D_MODEL and s % BLOCK == 0
    tm, ts, tb = _tile_sizes(bsz, s)
    biasm = _band_bias(rel_bias)
    x2 = x.reshape(bsz * s, d)
    for l in range(norm_mix.shape[0]):
        hc, bc, qn, kv, gc, ga = _proj_call(x2, norm_mix[l], w_in[l], b_gate[l], q_norm[l], k_norm[l], tm)
        x1, h2, pq = _mixer_call(x2, hc, bc, qn, kv, gc, ga, w_conv[l], biasm, sinks[l], w_conv_out[l],
                                 w_attn_out[l], w_out[l], norm_ffn[l], w_query[l], bsz, s, ts)
        x2 = _peer_call(x1, h2, pq, sub_keys[l], expert_u[l], expert_v[l], tb, EXPERT_CHUNK)
    return x2.reshape(bsz, s, d)
```

```python
import functools
import math

import jax
import jax.numpy as jnp
import numpy as np
from jax import lax
from jax.experimental import pallas as pl
from jax.experimental.pallas import tpu as pltpu

F32 = jnp.float32
BF16 = jnp.bfloat16

D_MODEL = 1024
CONV_DIM = 512
CONV_K = 3
N_HEADS = 8
N_KV_HEADS = 2
HEAD_DIM = 64
Q_DIM = N_HEADS * HEAD_DIM
KV_DIM = N_KV_HEADS * HEAD_DIM
GROUP = N_HEADS // N_KV_HEADS
WINDOW = 128
BLOCK = 128
N_BUCKETS = 32
MAX_DISTANCE = 128
PEER_HEADS = 8
N_KEYS = 128
N_EXPERTS = N_KEYS * N_KEYS
PEER_QDIM = 256
PEER_HALF = PEER_QDIM // 2
PEER_TOPK = 16
N_SEL = PEER_HEADS * PEER_TOPK
EPS = 1e-6
NEG_BIG = -1e30
PAIR_W = 2 * HEAD_DIM
KV_WIDE = N_KV_HEADS * PAIR_W

SUBLANES = 8
LANES = 128

EXPERT_CHUNK = N_EXPERTS // (2 * PEER_HEADS)
EXPERT_PIECE = 2 * N_KEYS
TOKENS_PER_GATE_GROUP = 2 * SUBLANES
GATE_GROUPS_PER_ITER = 4
PEER_VMEM_LIMIT = 60 * 1024 * 1024
DENSE_VMEM_LIMIT = 56 * 1024 * 1024

_OFFS = np.cumsum([0, CONV_DIM, CONV_DIM, CONV_DIM, Q_DIM, KV_DIM, KV_DIM, D_MODEL, D_MODEL]).tolist()

_NT = (((1,), (1,)), ((), ()))


def _split_bf16(a):
    hi = a.astype(BF16)
    lo = (a - hi.astype(F32)).astype(BF16)
    return hi, lo


def _proj_kernel(x_ref, g_ref, w_ref, bg_ref, qg_ref, kg_ref, avq_ref, avk_ref, exp_ref,
                 hc_ref, bc_ref, qn_ref, kv_ref, gc_ref, ga_ref):
    x = x_ref[...]
    ms = jnp.mean(x * x, axis=-1, keepdims=True)
    h = ((x * lax.rsqrt(ms + EPS)) * g_ref[...]).astype(BF16)

    def seg(i):
        return jnp.dot(h, w_ref[:, _OFFS[i]:_OFFS[i + 1]], preferred_element_type=F32)

    u = seg(0)
    hc_ref[...] = seg(2) * u
    bc_ref[...] = seg(1)

    def head_rms(a, av_ref):
        hi, lo = _split_bf16(a * a)
        return (jnp.dot(hi, av_ref[...], preferred_element_type=F32)
                + jnp.dot(lo, av_ref[...], preferred_element_type=F32))

    q = seg(3)
    qn = (q * lax.rsqrt(head_rms(q, avq_ref) + EPS)) * qg_ref[...] * (HEAD_DIM ** -0.5)
    qn_ref[...] = qn.astype(BF16)

    k = seg(4)
    kn = ((k * lax.rsqrt(head_rms(k, avk_ref) + EPS)) * kg_ref[...]).astype(BF16)
    v = seg(5).astype(BF16)
    kv_ref[...] = jnp.dot(jnp.concatenate([kn, v], axis=1), exp_ref[...], preferred_element_type=F32).astype(BF16)

    bg = bg_ref[...]
    gc_ref[...] = jax.nn.sigmoid(seg(6) + bg[:, :D_MODEL]).astype(BF16)
    ga_ref[...] = jax.nn.sigmoid(seg(7) + bg[:, D_MODEL:]).astype(BF16)


def _kv_layout():
    e = np.zeros((2 * KV_DIM, 3 * KV_WIDE), np.float32)
    for g in range(N_KV_HEADS):
        for d in range(HEAD_DIM):
            e[g * HEAD_DIM + d, g * PAIR_W + d] = 1.0
            e[g * HEAD_DIM + d, KV_WIDE + g * PAIR_W + HEAD_DIM + d] = 1.0
            e[KV_DIM + g * HEAD_DIM + d, 2 * KV_WIDE + g * PAIR_W + d] = 1.0
            e[KV_DIM + g * HEAD_DIM + d, 2 * KV_WIDE + g * PAIR_W + HEAD_DIM + d] = 1.0
    return jnp.asarray(e, BF16)


def _proj_call(x2, norm_mix, w_in, b_gate, q_norm, k_norm, tm):
    t = x2.shape[0]
    in_dim = w_in.shape[1]
    avq = jnp.kron(jnp.eye(N_HEADS, dtype=F32), jnp.full((HEAD_DIM, HEAD_DIM), 1.0 / HEAD_DIM, F32)).astype(BF16)
    avk = jnp.kron(jnp.eye(N_KV_HEADS, dtype=F32), jnp.full((HEAD_DIM, HEAD_DIM), 1.0 / HEAD_DIM, F32)).astype(BF16)
    qg = jnp.tile(q_norm.astype(F32), N_HEADS).reshape(1, Q_DIM)
    kg = jnp.tile(k_norm.astype(F32), N_KV_HEADS).reshape(1, KV_DIM)
    full = lambda shape: pl.BlockSpec(shape, lambda i: (0,) * len(shape))
    row = lambda c: pl.BlockSpec((tm, c), lambda i: (i, 0))
    return pl.pallas_call(
        _proj_kernel,
        grid=(t // tm,),
        in_specs=[row(D_MODEL), full((1, D_MODEL)), full((D_MODEL, in_dim)), full((1, 2 * D_MODEL)),
                  full((1, Q_DIM)), full((1, KV_DIM)), full((Q_DIM, Q_DIM)), full((KV_DIM, KV_DIM)),
                  full((2 * KV_DIM, 3 * KV_WIDE))],
        out_specs=[row(CONV_DIM), row(CONV_DIM), row(Q_DIM), row(3 * KV_WIDE), row(D_MODEL), row(D_MODEL)],
        out_shape=[jax.ShapeDtypeStruct((t, CONV_DIM), F32), jax.ShapeDtypeStruct((t, CONV_DIM), F32),
                   jax.ShapeDtypeStruct((t, Q_DIM), BF16), jax.ShapeDtypeStruct((t, 3 * KV_WIDE), BF16),
                   jax.ShapeDtypeStruct((t, D_MODEL), BF16), jax.ShapeDtypeStruct((t, D_MODEL), BF16)],
        compiler_params=pltpu.CompilerParams(dimension_semantics=("arbitrary",),
                                             vmem_limit_bytes=DENSE_VMEM_LIMIT),
        name="peer_block_proj",
    )(x2, norm_mix.reshape(1, D_MODEL).astype(F32), w_in.astype(BF16), b_gate.reshape(1, 2 * D_MODEL).astype(F32),
      qg, kg, avq, avk, _kv_layout())


def _mixer_kernel(x_ref, hc_ref, hcp_ref, bc_ref, qn_ref, kv_ref, kvp_ref, gc_ref, ga_ref,
                  wc_ref, bias_ref, sink_ref, wco_ref, wao_ref, wo_ref, g2_ref, wq_ref,
                  x1_ref, h2_ref, pq_ref, att_ref, *, ts):
    j = pl.program_id(1)
    first = j == 0

    hc = hc_ref[...]
    prev = jnp.where(first, 0.0, hcp_ref[...])
    rows = lax.broadcasted_iota(jnp.int32, hc.shape, 0)
    s1 = pltpu.roll(hc, 1, axis=0)
    s1 = jnp.where(rows == 0, prev[SUBLANES - 1:SUBLANES, :], s1)
    s2 = pltpu.roll(hc, 2, axis=0)
    s2 = jnp.where(rows == 0, prev[SUBLANES - 2:SUBLANES - 1, :], s2)
    s2 = jnp.where(rows == 1, prev[SUBLANES - 1:SUBLANES, :], s2)
    wc = wc_ref[...]
    conv = s2 * wc[0:1, :] + s1 * wc[1:2, :] + hc * wc[2:3, :]
    yc = (bc_ref[...] * conv).astype(BF16)
    y_conv = jnp.dot(yc, wco_ref[...], preferred_element_type=F32)

    kvfull = jnp.concatenate([kvp_ref[...], kv_ref[...]], axis=0)
    col = lax.broadcasted_iota(jnp.int32, (BLOCK, 2 * BLOCK), 1)
    pen0 = jnp.where(jnp.logical_and(first, col < BLOCK), NEG_BIG, 0.0)
    even_lanes = lax.broadcasted_iota(jnp.int32, (BLOCK, PAIR_W), 1) < HEAD_DIM
    for r in range(ts // BLOCK):
        kvb = kvfull[r * BLOCK:(r + 2) * BLOCK, :]
        for pair in range(N_HEADS // 2):
            g = (2 * pair) // GROUP
            qs = qn_ref[r * BLOCK:(r + 1) * BLOCK, pair * PAIR_W:(pair + 1) * PAIR_W]
            vsel = kvb[:, 2 * KV_WIDE + g * PAIR_W:2 * KV_WIDE + (g + 1) * PAIR_W]
            outs = []
            for par in range(2):
                hd = 2 * pair + par
                ksel = kvb[:, par * KV_WIDE + g * PAIR_W:par * KV_WIDE + (g + 1) * PAIR_W]
                lg = lax.dot_general(qs, ksel, _NT, preferred_element_type=F32) + bias_ref[hd]
                if r == 0:
                    lg = lg + pen0
                sink = sink_ref[hd]
                m = jnp.maximum(jnp.max(lg, axis=-1, keepdims=True), sink)
                p = jnp.exp(lg - m)
                denom = jnp.sum(p, axis=-1, keepdims=True) + jnp.exp(sink - m)
                pv = jnp.dot(p.astype(BF16), vsel, preferred_element_type=F32)
                outs.append(pv / denom)
            att_ref[r * BLOCK:(r + 1) * BLOCK, pair * PAIR_W:(pair + 1) * PAIR_W] = (
                jnp.where(even_lanes, outs[0], outs[1]).astype(BF16))
    y_attn = jnp.dot(att_ref[...], wao_ref[...], preferred_element_type=F32)

    mixed = (gc_ref[...].astype(F32) * y_conv + ga_ref[...].astype(F32) * y_attn).astype(BF16)
    x1 = x_ref[...] + jnp.dot(mixed, wo_ref[...], preferred_element_type=F32)
    x1_ref[...] = x1

    ms = jnp.mean(x1 * x1, axis=-1, keepdims=True)
    h2 = ((x1 * lax.rsqrt(ms + EPS)) * g2_ref[...]).astype(BF16)
    h2_ref[...] = h2
    pq_ref[...] = jnp.dot(h2, wq_ref[...], preferred_element_type=F32)


def _mixer_call(x2, hc, bc, qn, kv, gc, ga, w_conv, biasm, sinks, w_conv_out, w_attn_out, w_out,
                norm_ffn, w_query, bsz, s, ts):
    t = bsz * s
    qd = w_query.shape[1]
    nj = s // ts
    row = lambda c: pl.BlockSpec((ts, c), lambda b, j: (b * nj + j, 0))
    prev_blk = lambda c: pl.BlockSpec((BLOCK, c), lambda b, j: (jnp.maximum((b * nj + j) * (ts // BLOCK) - 1, 0), 0))
    prev8 = pl.BlockSpec((SUBLANES, CONV_DIM),
                         lambda b, j: (jnp.maximum((b * nj + j) * (ts // SUBLANES) - 1, 0), 0))
    full = lambda shape: pl.BlockSpec(shape, lambda b, j: (0,) * len(shape))
    return pl.pallas_call(
        functools.partial(_mixer_kernel, ts=ts),
        grid=(bsz, nj),
        in_specs=[row(D_MODEL), row(CONV_DIM), prev8, row(CONV_DIM), row(Q_DIM),
                  row(3 * KV_WIDE), prev_blk(3 * KV_WIDE), row(D_MODEL), row(D_MODEL),
                  full((CONV_K, CONV_DIM)), full((N_HEADS, BLOCK, 2 * BLOCK)),
                  pl.BlockSpec(memory_space=pltpu.SMEM),
                  full((CONV_DIM, D_MODEL)), full((Q_DIM, D_MODEL)), full((D_MODEL, D_MODEL)),
                  full((1, D_MODEL)), full((D_MODEL, qd))],
        out_specs=[row(D_MODEL), row(D_MODEL), row(qd)],
        out_shape=[jax.ShapeDtypeStruct((t, D_MODEL), F32), jax.ShapeDtypeStruct((t, D_MODEL), BF16),
                   jax.ShapeDtypeStruct((t, qd), F32)],
        scratch_shapes=[pltpu.VMEM((ts, Q_DIM), BF16)],
        compiler_params=pltpu.CompilerParams(dimension_semantics=("arbitrary", "arbitrary"),
                                             vmem_limit_bytes=DENSE_VMEM_LIMIT),
        name="peer_block_mixer",
    )(x2, hc, hc, bc, qn, kv, kv, gc, ga, w_conv.astype(F32), biasm, sinks.astype(F32),
      w_conv_out.astype(BF16), w_attn_out.astype(BF16), w_out.astype(BF16),
      norm_ffn.reshape(1, D_MODEL).astype(F32), w_query.astype(BF16))


def _band_bias(rel_bias):
    q_loc = jnp.arange(BLOCK, dtype=jnp.int32)[:, None]
    k_loc = jnp.arange(2 * BLOCK, dtype=jnp.int32)[None, :]
    dist = q_loc + BLOCK - k_loc
    max_exact = N_BUCKETS // 2
    d = jnp.maximum(dist, 0)
    df = jnp.maximum(d, 1).astype(F32)
    large = max_exact + (jnp.log(df / max_exact) / math.log(MAX_DISTANCE / max_exact)
                         * (N_BUCKETS - max_exact)).astype(jnp.int32)
    large = jnp.minimum(large, N_BUCKETS - 1)
    bucket = jnp.where(d < max_exact, d, large)
    onehot = (bucket[None] == jnp.arange(N_BUCKETS, dtype=jnp.int32)[:, None, None]).astype(F32)
    bias = jnp.einsum("bh,bqk->hqk", rel_bias.astype(F32), onehot, precision=lax.Precision.HIGHEST)
    valid = (dist >= 0) & (dist < WINDOW)
    return jnp.where(valid[None], bias, NEG_BIG)


def _sorting_network(n):
    def merge(lo, hi, r):
        step = r * 2
        if step < hi - lo:
            yield from merge(lo, hi, step)
            yield from merge(lo + r, hi, step)
            yield from ((i, i + r) for i in range(lo + r, hi - r, step))
        else:
            yield (lo, lo + r)

    def sort(lo, hi):
        if hi > lo:
            mid = lo + (hi - lo) // 2
            yield from sort(lo, mid)
            yield from sort(mid + 1, hi)
            yield from merge(lo, hi, 1)

    return list(sort(0, n - 1))


def _top16(vals, rid, big):
    n = vals.shape[0] // SUBLANES
    x = [vals[k * SUBLANES:(k + 1) * SUBLANES, :] for k in range(n)]
    ids = [rid[k * SUBLANES:(k + 1) * SUBLANES, :] for k in range(n)]
    for i, j in _sorting_network(n):
        a, b, ia, ib = x[i], x[j], ids[i], ids[j]
        first = (a > b) | ((a == b) & (ia < ib))
        x[i], x[j] = jnp.maximum(a, b), jnp.minimum(a, b)
        ids[i], ids[j] = jnp.where(first, ia, ib), jnp.where(first, ib, ia)
    tv, ti = [], []
    for r in range(PEER_TOPK):
        m = jnp.max(x[0], axis=0, keepdims=True)
        idx = jnp.min(jnp.where(x[0] == m, ids[0], big), axis=0, keepdims=True)
        tv.append(m)
        ti.append(idx)
        won = ids[0] == idx
        for k in range(min(n - 1, PEER_TOPK - 1 - r)):
            x[k] = jnp.where(won, x[k + 1], x[k])
            ids[k] = jnp.where(won, ids[k + 1], ids[k])
        if n - 1 < PEER_TOPK - 1 - r:
            x[n - 1] = jnp.where(won, -jnp.inf, x[n - 1])
    return jnp.concatenate(tv, axis=0), jnp.concatenate(ti, axis=0)


def _row_iota(n_rows):
    return lax.broadcasted_iota(jnp.int32, (n_rows, LANES), 0).astype(F32)


def _pair_candidates(v0, v1):
    r8 = _row_iota(SUBLANES)
    r16 = _row_iota(PEER_TOPK)
    vals = [v0[0:1, :] + v1]
    pos = [r16]
    for a in (1, 2, 3):
        vals.append(v0[a:a + 1, :] + v1[0:SUBLANES, :])
        pos.append(r8 + float(a * PEER_TOPK))
    low = r8 < 4.0
    v1dup = jnp.where(low, v1[0:SUBLANES, :], pltpu.roll(v1[0:SUBLANES, :], 4, axis=0))
    bdup = jnp.where(low, r8, r8 - 4.0)
    for a in (4, 6):
        vals.append(jnp.where(low, v0[a:a + 1, :], v0[a + 1:a + 2, :]) + v1dup)
        pos.append(jnp.where(low, float(a * PEER_TOPK), float((a + 1) * PEER_TOPK)) + bdup)
    vals.append(v0[SUBLANES:, :] + v1[0:1, :])
    pos.append((r8 + float(SUBLANES)) * float(PEER_TOPK))
    return jnp.concatenate(vals, axis=0), jnp.concatenate(pos, axis=0)


def _pick(table, sel):
    out = jnp.zeros_like(sel)
    for a in range(PEER_TOPK):
        out = jnp.where(sel == float(a), table[a:a + 1, :], out)
    return out


def _select_pairs(top0, top1):
    (v0, j0), (v1, j1) = top0, top1
    cand, cpos = _pair_candidates(v0, v1)
    ts_, pos = _top16(cand, cpos, float(PEER_TOPK * PEER_TOPK))
    pa = jnp.floor(pos * (1.0 / PEER_TOPK))
    pb = pos - pa * PEER_TOPK
    e = jnp.exp(ts_ - ts_[0:1, :])
    return _pick(j0, pa), _pick(j1, pb), e / jnp.sum(e, axis=0, keepdims=True)


def _gate_tiles(gd_ref, tg, i0b, i1b, gb, rid):
    zero = jnp.zeros_like(rid)
    one = jnp.ones_like(rid)
    for p in range(SUBLANES):
        tiles = []
        for t in (2 * p, 2 * p + 1):
            rt = jnp.where(rid == i0b[t:t + 1, :], gb[t:t + 1, :], zero)
            ct = jnp.where(rid == i1b[t:t + 1, :], one, zero)
            tiles.append(lax.dot_general(rt, ct, _NT, preferred_element_type=F32))
        lo_, hi_ = (pltpu.bitcast(t_.astype(BF16).astype(F32), jnp.uint32) for t_ in tiles)
        gd_ref[tg, pl.ds(p, N_KEYS, stride=SUBLANES), :] = (
            lax.shift_right_logical(lo_, jnp.uint32(16)) | (hi_ & jnp.uint32(0xFFFF0000)))


def _peer_kernel(x1_ref, h2_ref, q_ref, kh_ref, kl_ref, u_ref, v_ref, out_ref, gd_ref, rs_ref, tok_ref, *, tb, ec):
    ib = pl.program_id(0)
    c = pl.program_id(1)
    half_tok = tb // 2
    n_lg = half_tok // LANES
    n_pieces = ec // EXPERT_PIECE
    grp_per_piece = EXPERT_PIECE // N_KEYS
    w_slot = ib % 2
    r_slot = 1 - w_slot

    @pl.when(jnp.logical_and(ib == 0, c == 0))
    def _no_block_before_the_first():
        tok_ref[r_slot] = jnp.zeros(tok_ref.shape[1:], F32)

    @pl.when(c == 0)
    def _build_gate_matrix():
        rid = lax.broadcasted_iota(jnp.int32, (N_KEYS, N_SEL), 0).astype(F32).astype(BF16)

        def body(it, carry):
            for k in range(GATE_GROUPS_PER_ITER):
                tg = it * GATE_GROUPS_PER_ITER + k
                r0 = pl.multiple_of(tg * TOKENS_PER_GATE_GROUP, TOKENS_PER_GATE_GROUP)
                rows = pl.ds(r0, TOKENS_PER_GATE_GROUP)
                _gate_tiles(gd_ref, tg, tok_ref[r_slot, 0, rows, :].astype(BF16),
                            tok_ref[r_slot, 1, rows, :].astype(BF16),
                            (0.5 * tok_ref[r_slot, 2, rows, :]).astype(BF16), rid)
            return carry

        lax.fori_loop(0, tb // (TOKENS_PER_GATE_GROUP * GATE_GROUPS_PER_ITER), body, 0)
        out_ref[...] = x1_ref[...]

    hd = c // 2
    half = c % 2
    base = pl.multiple_of(c * (ec // N_KEYS * SUBLANES), SUBLANES)
    h2 = h2_ref[...]

    def scores(part):
        qh_, ql_ = _split_bf16(q_ref[:, part * PEER_HALF:(part + 1) * PEER_HALF])
        kh, kl = kh_ref[hd * 2 + part], kl_ref[hd * 2 + part]
        return (lax.dot_general(kh, qh_, _NT, preferred_element_type=F32)
                + lax.dot_general(kh, ql_, _NT, preferred_element_type=F32)
                + lax.dot_general(kl, qh_, _NT, preferred_element_type=F32))

    def weights(j):
        a = lax.dot_general(h2, u_ref[j * EXPERT_PIECE:(j + 1) * EXPERT_PIECE, :], _NT, preferred_element_type=F32)
        gd = jnp.concatenate(
            [pltpu.bitcast(gd_ref[:, pl.ds(base + (grp_per_piece * j + g) * SUBLANES, SUBLANES), :]
                           .reshape(tb // 2, N_KEYS), BF16) for g in range(grp_per_piece)], axis=1)
        return gd * (a * (1.0 + lax.erf(a * (2.0 ** -0.5)))).astype(BF16)

    tops = [[None] * n_lg, [None] * n_lg]
    w_pieces = []
    for part in range(2):
        st = scores(part)
        for lg in range(n_lg):
            tops[part][lg] = _top16(st[:, lg * LANES:(lg + 1) * LANES], _row_iota(N_KEYS), float(N_KEYS))
            w_pieces.extend(weights(j) for j in range(len(w_pieces), min(len(w_pieces) + 2, n_pieces)))
    w_pieces.extend(weights(j) for j in range(len(w_pieces), n_pieces))
    sel = [_select_pairs(tops[0][lg], tops[1][lg]) for lg in range(n_lg)]
    rows = pl.ds(pl.multiple_of(hd * PEER_TOPK, PEER_TOPK), PEER_TOPK)
    for k in range(3):
        rs_ref[w_slot, half, k, rows, :] = jnp.concatenate([s_[k] for s_ in sel], axis=1)
    out_ref[...] += jnp.dot(jnp.concatenate(w_pieces, axis=1), v_ref[...], preferred_element_type=F32)

    @pl.when(c == pl.num_programs(1) - 1)
    def _routing_to_token_major():
        for hf in range(2):
            for k in range(3):
                tok_ref[w_slot, k, hf * half_tok:(hf + 1) * half_tok, :] = rs_ref[w_slot, hf, k].T


def _peer_call(x1, h2, pq, sub_keys, expert_u, expert_v, tb, ec):
    t = x1.shape[0]
    nb = t // tb
    nc = N_EXPERTS // ec
    assert nc == 2 * PEER_HEADS and tb % (2 * LANES) == 0 and ec % EXPERT_PIECE == 0
    keys = sub_keys.astype(F32).reshape(PEER_HEADS * 2, N_KEYS, PEER_HALF)
    kh, kl = _split_bf16(keys)
    prev_row = lambda c_: pl.BlockSpec((tb, c_), lambda i, c: (jnp.maximum(i - 1, 0), 0))
    chunk = pl.BlockSpec((ec, D_MODEL), lambda i, c: (c, 0))
    q_spec = pl.BlockSpec((tb // 2, 2 * PEER_HALF), lambda i, c: (jnp.minimum(i, nb - 1) * 2 + c % 2, c // 2))
    full3 = pl.BlockSpec((PEER_HEADS * 2, N_KEYS, PEER_HALF), lambda i, c: (0, 0, 0))
    return pl.pallas_call(
        functools.partial(_peer_kernel, tb=tb, ec=ec),
        grid=(nb + 1, nc),
        in_specs=[prev_row(D_MODEL), prev_row(D_MODEL), q_spec, full3, full3, chunk, chunk],
        out_specs=prev_row(D_MODEL),
        out_shape=jax.ShapeDtypeStruct((t, D_MODEL), F32),
        scratch_shapes=[pltpu.VMEM((tb // TOKENS_PER_GATE_GROUP, N_KEYS * SUBLANES, N_KEYS), jnp.uint32),
                        pltpu.VMEM((2, 2, 3, N_SEL, tb // 2), F32),
                        pltpu.VMEM((2, 3, tb, N_SEL), F32)],
        compiler_params=pltpu.CompilerParams(dimension_semantics=("arbitrary", "arbitrary"),
                                             vmem_limit_bytes=PEER_VMEM_LIMIT),
        name="peer_block_experts",
    )(x1, h2, pq, kh, kl, expert_u.astype(BF16), expert_v.astype(BF16))


def _tile_sizes(bsz, s):
    t = bsz * s
    tm = math.gcd(t, 512)
    ts = math.gcd(s, 512)
    tb = math.gcd(t, 512)
    return tm, ts, tb


def kernel(x, norm_mix, norm_ffn, w_in, b_gate, w_conv, q_norm, k_norm, sinks, rel_bias, w_conv_out, w_attn_out,
           w_out, w_query, sub_keys, expert_u, expert_v):
    bsz, s, d = x.shape
    assert d == D_MODEL and s % BLOCK == 0
    tm, ts, tb = _tile_sizes(bsz, s)
    biasm = _band_bias(rel_bias)
    x2 = x.reshape(bsz * s, d)
    for l in range(norm_mix.shape[0]):
        hc, bc, qn, kv, gc, ga = _proj_call(x2, norm_mix[l], w_in[l], b_gate[l], q_norm[l], k_norm[l], tm)
        x1, h2, pq = _mixer_call(x2, hc, bc, qn, kv, gc, ga, w_conv[l], biasm, sinks[l], w_conv_out[l],
                                 w_attn_out[l], w_out[l], norm_ffn[l], w_query[l], bsz, s, ts)
        x2 = _peer_call(x1, h2, pq, sub_keys[l], expert_u[l], expert_v[l], tb, EXPERT_CHUNK)
    return x2.reshape(bsz, s, d)
```

```python
import functools
import math

import jax
import jax.numpy as jnp
import numpy as np
from jax import lax
from jax.experimental import pallas as pl
from jax.experimental.pallas import tpu as pltpu

F32 = jnp.float32
BF16 = jnp.bfloat16

D_MODEL = 1024
CONV_DIM = 512
CONV_K = 3
N_HEADS = 8
N_KV_HEADS = 2
HEAD_DIM = 64
Q_DIM = N_HEADS * HEAD_DIM
KV_DIM = N_KV_HEADS * HEAD_DIM
GROUP = N_HEADS // N_KV_HEADS
WINDOW = 128
BLOCK = 128
N_BUCKETS = 32
MAX_DISTANCE = 128
PEER_HEADS = 8
N_KEYS = 128
N_EXPERTS = N_KEYS * N_KEYS
PEER_QDIM = 256
PEER_HALF = PEER_QDIM // 2
PEER_TOPK = 16
N_SEL = PEER_HEADS * PEER_TOPK
EPS = 1e-6
NEG_BIG = -1e30
PAIR_W = 2 * HEAD_DIM
KV_WIDE = N_KV_HEADS * PAIR_W

SUBLANES = 8
LANES = 128

EXPERT_CHUNK = N_EXPERTS // (2 * PEER_HEADS)
EXPERT_PIECE = 2 * N_KEYS
TOKENS_PER_GATE_GROUP = 2 * SUBLANES
GATE_GROUPS_PER_ITER = 4
V7X_VMEM_BYTES = 64 * 1024 * 1024
COMPILER_TEMP_VMEM = 12 * 1024 * 1024

_OFFS = np.cumsum([0, CONV_DIM, CONV_DIM, CONV_DIM, Q_DIM, KV_DIM, KV_DIM, D_MODEL, D_MODEL]).tolist()

_NT = (((1,), (1,)), ((), ()))


def _vmem_limit(in_specs, operands, out_specs, out_shapes, scratch_shapes=()):
    nbytes = lambda shape, dtype: math.prod(shape) * jnp.dtype(dtype).itemsize
    windows = sum(nbytes(s.block_shape, a.dtype)
                  for s, a in zip(list(in_specs) + list(out_specs), list(operands) + list(out_shapes))
                  if s.block_shape is not None)
    need = 2 * windows + sum(nbytes(s.shape, s.dtype) for s in scratch_shapes) + COMPILER_TEMP_VMEM
    assert need <= V7X_VMEM_BYTES, need
    return need


def _split_bf16(a):
    hi = a.astype(BF16)
    lo = (a - hi.astype(F32)).astype(BF16)
    return hi, lo


def _proj_kernel(x_ref, g_ref, w_ref, bg_ref, qg_ref, kg_ref, avq_ref, avk_ref, exp_ref,
                 hc_ref, bc_ref, qn_ref, kv_ref, gc_ref, ga_ref):
    x = x_ref[...]
    ms = jnp.mean(x * x, axis=-1, keepdims=True)
    h = ((x * lax.rsqrt(ms + EPS)) * g_ref[...]).astype(BF16)

    def seg(i):
        return jnp.dot(h, w_ref[:, _OFFS[i]:_OFFS[i + 1]], preferred_element_type=F32)

    u = seg(0)
    hc_ref[...] = seg(2) * u
    bc_ref[...] = seg(1)

    def head_rms(a, av_ref):
        hi, lo = _split_bf16(a * a)
        return (jnp.dot(hi, av_ref[...], preferred_element_type=F32)
                + jnp.dot(lo, av_ref[...], preferred_element_type=F32))

    q = seg(3)
    qn = (q * lax.rsqrt(head_rms(q, avq_ref) + EPS)) * qg_ref[...] * (HEAD_DIM ** -0.5)
    qn_ref[...] = qn.astype(BF16)

    k = seg(4)
    kn = ((k * lax.rsqrt(head_rms(k, avk_ref) + EPS)) * kg_ref[...]).astype(BF16)
    v = seg(5).astype(BF16)
    kv_ref[...] = jnp.dot(jnp.concatenate([kn, v], axis=1), exp_ref[...], preferred_element_type=F32).astype(BF16)

    bg = bg_ref[...]
    gc_ref[...] = jax.nn.sigmoid(seg(6) + bg[:, :D_MODEL]).astype(BF16)
    ga_ref[...] = jax.nn.sigmoid(seg(7) + bg[:, D_MODEL:]).astype(BF16)


def _kv_layout():
    e = np.zeros((2 * KV_DIM, 3 * KV_WIDE), np.float32)
    for g in range(N_KV_HEADS):
        for d in range(HEAD_DIM):
            e[g * HEAD_DIM + d, g * PAIR_W + d] = 1.0
            e[g * HEAD_DIM + d, KV_WIDE + g * PAIR_W + HEAD_DIM + d] = 1.0
            e[KV_DIM + g * HEAD_DIM + d, 2 * KV_WIDE + g * PAIR_W + d] = 1.0
            e[KV_DIM + g * HEAD_DIM + d, 2 * KV_WIDE + g * PAIR_W + HEAD_DIM + d] = 1.0
    return jnp.asarray(e, BF16)


def _proj_call(x2, norm_mix, w_in, b_gate, q_norm, k_norm, tm):
    t = x2.shape[0]
    in_dim = w_in.shape[1]
    avq = jnp.kron(jnp.eye(N_HEADS, dtype=F32), jnp.full((HEAD_DIM, HEAD_DIM), 1.0 / HEAD_DIM, F32)).astype(BF16)
    avk = jnp.kron(jnp.eye(N_KV_HEADS, dtype=F32), jnp.full((HEAD_DIM, HEAD_DIM), 1.0 / HEAD_DIM, F32)).astype(BF16)
    qg = jnp.tile(q_norm.astype(F32), N_HEADS).reshape(1, Q_DIM)
    kg = jnp.tile(k_norm.astype(F32), N_KV_HEADS).reshape(1, KV_DIM)
    full = lambda shape: pl.BlockSpec(shape, lambda i: (0,) * len(shape))
    row = lambda c: pl.BlockSpec((tm, c), lambda i: (i, 0))
    in_specs = [row(D_MODEL), full((1, D_MODEL)), full((D_MODEL, in_dim)), full((1, 2 * D_MODEL)),
                full((1, Q_DIM)), full((1, KV_DIM)), full((Q_DIM, Q_DIM)), full((KV_DIM, KV_DIM)),
                full((2 * KV_DIM, 3 * KV_WIDE))]
    operands = [x2, norm_mix.reshape(1, D_MODEL).astype(F32), w_in.astype(BF16),
                b_gate.reshape(1, 2 * D_MODEL).astype(F32), qg, kg, avq, avk, _kv_layout()]
    out_specs = [row(CONV_DIM), row(CONV_DIM), row(Q_DIM), row(3 * KV_WIDE), row(D_MODEL), row(D_MODEL)]
    out_shape = [jax.ShapeDtypeStruct((t, CONV_DIM), F32), jax.ShapeDtypeStruct((t, CONV_DIM), F32),
                 jax.ShapeDtypeStruct((t, Q_DIM), BF16), jax.ShapeDtypeStruct((t, 3 * KV_WIDE), BF16),
                 jax.ShapeDtypeStruct((t, D_MODEL), BF16), jax.ShapeDtypeStruct((t, D_MODEL), BF16)]
    return pl.pallas_call(
        _proj_kernel,
        grid=(t // tm,),
        in_specs=in_specs,
        out_specs=out_specs,
        out_shape=out_shape,
        compiler_params=pltpu.CompilerParams(
            dimension_semantics=("arbitrary",),
            vmem_limit_bytes=_vmem_limit(in_specs, operands, out_specs, out_shape)),
        name="peer_block_proj",
    )(*operands)


def _mixer_kernel(x_ref, hc_ref, hcp_ref, bc_ref, qn_ref, kv_ref, kvp_ref, gc_ref, ga_ref,
                  wc_ref, bias_ref, sink_ref, wco_ref, wao_ref, wo_ref, g2_ref, wq_ref,
                  x1_ref, h2_ref, pq_ref, att_ref, *, ts):
    j = pl.program_id(1)
    first = j == 0

    hc = hc_ref[...]
    prev = jnp.where(first, 0.0, hcp_ref[...])
    rows = lax.broadcasted_iota(jnp.int32, hc.shape, 0)
    s1 = pltpu.roll(hc, 1, axis=0)
    s1 = jnp.where(rows == 0, prev[SUBLANES - 1:SUBLANES, :], s1)
    s2 = pltpu.roll(hc, 2, axis=0)
    s2 = jnp.where(rows == 0, prev[SUBLANES - 2:SUBLANES - 1, :], s2)
    s2 = jnp.where(rows == 1, prev[SUBLANES - 1:SUBLANES, :], s2)
    wc = wc_ref[...]
    conv = s2 * wc[0:1, :] + s1 * wc[1:2, :] + hc * wc[2:3, :]
    yc = (bc_ref[...] * conv).astype(BF16)
    y_conv = jnp.dot(yc, wco_ref[...], preferred_element_type=F32)

    kvfull = jnp.concatenate([kvp_ref[...], kv_ref[...]], axis=0)
    col = lax.broadcasted_iota(jnp.int32, (BLOCK, 2 * BLOCK), 1)
    pen0 = jnp.where(jnp.logical_and(first, col < BLOCK), NEG_BIG, 0.0)
    even_lanes = lax.broadcasted_iota(jnp.int32, (BLOCK, PAIR_W), 1) < HEAD_DIM
    for r in range(ts // BLOCK):
        kvb = kvfull[r * BLOCK:(r + 2) * BLOCK, :]
        for pair in range(N_HEADS // 2):
            g = (2 * pair) // GROUP
            qs = qn_ref[r * BLOCK:(r + 1) * BLOCK, pair * PAIR_W:(pair + 1) * PAIR_W]
            vsel = kvb[:, 2 * KV_WIDE + g * PAIR_W:2 * KV_WIDE + (g + 1) * PAIR_W]
            outs = []
            for par in range(2):
                hd = 2 * pair + par
                ksel = kvb[:, par * KV_WIDE + g * PAIR_W:par * KV_WIDE + (g + 1) * PAIR_W]
                lg = lax.dot_general(qs, ksel, _NT, preferred_element_type=F32) + bias_ref[hd]
                if r == 0:
                    lg = lg + pen0
                sink = sink_ref[hd]
                m = jnp.maximum(jnp.max(lg, axis=-1, keepdims=True), sink)
                p = jnp.exp(lg - m)
                denom = jnp.sum(p, axis=-1, keepdims=True) + jnp.exp(sink - m)
                pv = jnp.dot(p.astype(BF16), vsel, preferred_element_type=F32)
                outs.append(pv / denom)
            att_ref[r * BLOCK:(r + 1) * BLOCK, pair * PAIR_W:(pair + 1) * PAIR_W] = (
                jnp.where(even_lanes, outs[0], outs[1]).astype(BF16))
    y_attn = jnp.dot(att_ref[...], wao_ref[...], preferred_element_type=F32)

    mixed = (gc_ref[...].astype(F32) * y_conv + ga_ref[...].astype(F32) * y_attn).astype(BF16)
    x1 = x_ref[...] + jnp.dot(mixed, wo_ref[...], preferred_element_type=F32)
    x1_ref[...] = x1

    ms = jnp.mean(x1 * x1, axis=-1, keepdims=True)
    h2 = ((x1 * lax.rsqrt(ms + EPS)) * g2_ref[...]).astype(BF16)
    h2_ref[...] = h2
    pq_ref[...] = jnp.dot(h2, wq_ref[...], preferred_element_type=F32)


def _mixer_call(x2, hc, bc, qn, kv, gc, ga, w_conv, biasm, sinks, w_conv_out, w_attn_out, w_out,
                norm_ffn, w_query, bsz, s, ts):
    t = bsz * s
    qd = w_query.shape[1]
    nj = s // ts
    row = lambda c: pl.BlockSpec((ts, c), lambda b, j: (b * nj + j, 0))
    prev_blk = lambda c: pl.BlockSpec((BLOCK, c), lambda b, j: (jnp.maximum((b * nj + j) * (ts // BLOCK) - 1, 0), 0))
    prev8 = pl.BlockSpec((SUBLANES, CONV_DIM),
                         lambda b, j: (jnp.maximum((b * nj + j) * (ts // SUBLANES) - 1, 0), 0))
    full = lambda shape: pl.BlockSpec(shape, lambda b, j: (0,) * len(shape))
    in_specs = [row(D_MODEL), row(CONV_DIM), prev8, row(CONV_DIM), row(Q_DIM),
                row(3 * KV_WIDE), prev_blk(3 * KV_WIDE), row(D_MODEL), row(D_MODEL),
                full((CONV_K, CONV_DIM)), full((N_HEADS, BLOCK, 2 * BLOCK)),
                pl.BlockSpec(memory_space=pltpu.SMEM),
                full((CONV_DIM, D_MODEL)), full((Q_DIM, D_MODEL)), full((D_MODEL, D_MODEL)),
                full((1, D_MODEL)), full((D_MODEL, qd))]
    operands = [x2, hc, hc, bc, qn, kv, kv, gc, ga, w_conv.astype(F32), biasm, sinks.astype(F32),
                w_conv_out.astype(BF16), w_attn_out.astype(BF16), w_out.astype(BF16),
                norm_ffn.reshape(1, D_MODEL).astype(F32), w_query.astype(BF16)]
    out_specs = [row(D_MODEL), row(D_MODEL), row(qd)]
    out_shape = [jax.ShapeDtypeStruct((t, D_MODEL), F32), jax.ShapeDtypeStruct((t, D_MODEL), BF16),
                 jax.ShapeDtypeStruct((t, qd), F32)]
    scratch_shapes = [pltpu.VMEM((ts, Q_DIM), BF16)]
    return pl.pallas_call(
        functools.partial(_mixer_kernel, ts=ts),
        grid=(bsz, nj),
        in_specs=in_specs,
        out_specs=out_specs,
        out_shape=out_shape,
        scratch_shapes=scratch_shapes,
        compiler_params=pltpu.CompilerParams(
            dimension_semantics=("arbitrary", "arbitrary"),
            vmem_limit_bytes=_vmem_limit(in_specs, operands, out_specs, out_shape, scratch_shapes)),
        name="peer_block_mixer",
    )(*operands)


def _band_bias(rel_bias):
    q_loc = jnp.arange(BLOCK, dtype=jnp.int32)[:, None]
    k_loc = jnp.arange(2 * BLOCK, dtype=jnp.int32)[None, :]
    dist = q_loc + BLOCK - k_loc
    max_exact = N_BUCKETS // 2
    d = jnp.maximum(dist, 0)
    df = jnp.maximum(d, 1).astype(F32)
    large = max_exact + (jnp.log(df / max_exact) / math.log(MAX_DISTANCE / max_exact)
                         * (N_BUCKETS - max_exact)).astype(jnp.int32)
    large = jnp.minimum(large, N_BUCKETS - 1)
    bucket = jnp.where(d < max_exact, d, large)
    onehot = (bucket[None] == jnp.arange(N_BUCKETS, dtype=jnp.int32)[:, None, None]).astype(F32)
    bias = jnp.einsum("bh,bqk->hqk", rel_bias.astype(F32), onehot, precision=lax.Precision.HIGHEST)
    valid = (dist >= 0) & (dist < WINDOW)
    return jnp.where(valid[None], bias, NEG_BIG)


def _sorting_network(n):
    def merge(lo, hi, r):
        step = r * 2
        if step < hi - lo:
            yield from merge(lo, hi, step)
            yield from merge(lo + r, hi, step)
            yield from ((i, i + r) for i in range(lo + r, hi - r, step))
        else:
            yield (lo, lo + r)

    def sort(lo, hi):
        if hi > lo:
            mid = lo + (hi - lo) // 2
            yield from sort(lo, mid)
            yield from sort(mid + 1, hi)
            yield from merge(lo, hi, 1)

    return list(sort(0, n - 1))


def _top16(vals, rid, big):
    n = vals.shape[0] // SUBLANES
    x = [vals[k * SUBLANES:(k + 1) * SUBLANES, :] for k in range(n)]
    ids = [rid[k * SUBLANES:(k + 1) * SUBLANES, :] for k in range(n)]
    for i, j in _sorting_network(n):
        a, b, ia, ib = x[i], x[j], ids[i], ids[j]
        first = (a > b) | ((a == b) & (ia < ib))
        x[i], x[j] = jnp.maximum(a, b), jnp.minimum(a, b)
        ids[i], ids[j] = jnp.where(first, ia, ib), jnp.where(first, ib, ia)
    tv, ti = [], []
    for r in range(PEER_TOPK):
        m = jnp.max(x[0], axis=0, keepdims=True)
        idx = jnp.min(jnp.where(x[0] == m, ids[0], big), axis=0, keepdims=True)
        tv.append(m)
        ti.append(idx)
        won = ids[0] == idx
        for k in range(min(n - 1, PEER_TOPK - 1 - r)):
            x[k] = jnp.where(won, x[k + 1], x[k])
            ids[k] = jnp.where(won, ids[k + 1], ids[k])
        if n - 1 < PEER_TOPK - 1 - r:
            x[n - 1] = jnp.where(won, -jnp.inf, x[n - 1])
    return jnp.concatenate(tv, axis=0), jnp.concatenate(ti, axis=0)


def _row_iota(n_rows):
    return lax.broadcasted_iota(jnp.int32, (n_rows, LANES), 0).astype(F32)


def _pair_candidates(v0, v1):
    assert PEER_TOPK == 2 * SUBLANES
    half = SUBLANES // 2
    r8 = _row_iota(SUBLANES)
    r16 = _row_iota(PEER_TOPK)
    vals = [v0[0:1, :] + v1]
    pos = [r16]
    for a in (1, 2, 3):
        vals.append(v0[a:a + 1, :] + v1[0:SUBLANES, :])
        pos.append(r8 + float(a * PEER_TOPK))
    low = r8 < float(half)
    v1dup = jnp.where(low, v1[0:SUBLANES, :], pltpu.roll(v1[0:SUBLANES, :], half, axis=0))
    bdup = jnp.where(low, r8, r8 - float(half))
    for a in (4, 6):
        vals.append(jnp.where(low, v0[a:a + 1, :], v0[a + 1:a + 2, :]) + v1dup)
        pos.append(jnp.where(low, float(a * PEER_TOPK), float((a + 1) * PEER_TOPK)) + bdup)
    vals.append(v0[SUBLANES:, :] + v1[0:1, :])
    pos.append((r8 + float(SUBLANES)) * float(PEER_TOPK))
    return jnp.concatenate(vals, axis=0), jnp.concatenate(pos, axis=0)


def _pick(table, sel):
    out = jnp.zeros_like(sel)
    for a in range(PEER_TOPK):
        out = jnp.where(sel == float(a), table[a:a + 1, :], out)
    return out


def _select_pairs(top0, top1):
    (v0, j0), (v1, j1) = top0, top1
    cand, cpos = _pair_candidates(v0, v1)
    ts_, pos = _top16(cand, cpos, float(PEER_TOPK * PEER_TOPK))
    pa = jnp.floor(pos * (1.0 / PEER_TOPK))
    pb = pos - pa * PEER_TOPK
    e = jnp.exp(ts_ - ts_[0:1, :])
    return _pick(j0, pa), _pick(j1, pb), e / jnp.sum(e, axis=0, keepdims=True)


def _gate_tiles(gd_ref, tg, i0b, i1b, gb, rid):
    zero = jnp.zeros_like(rid)
    one = jnp.ones_like(rid)
    for p in range(SUBLANES):
        tiles = []
        for t in (2 * p, 2 * p + 1):
            rt = jnp.where(rid == i0b[t:t + 1, :], gb[t:t + 1, :], zero)
            ct = jnp.where(rid == i1b[t:t + 1, :], one, zero)
            tiles.append(lax.dot_general(rt, ct, _NT, preferred_element_type=F32))
        lo_, hi_ = (pltpu.bitcast(t_.astype(BF16).astype(F32), jnp.uint32) for t_ in tiles)
        gd_ref[tg, pl.ds(p, N_KEYS, stride=SUBLANES), :] = (
            lax.shift_right_logical(lo_, jnp.uint32(16)) | (hi_ & jnp.uint32(0xFFFF0000)))


def _peer_kernel(x1_ref, h2_ref, q_ref, kh_ref, kl_ref, u_ref, v_ref, out_ref, gd_ref, rs_ref, tok_ref, *, tb, ec):
    ib = pl.program_id(0)
    c = pl.program_id(1)
    half_tok = tb // 2
    n_lg = half_tok // LANES
    n_pieces = ec // EXPERT_PIECE
    grp_per_piece = EXPERT_PIECE // N_KEYS
    w_slot = ib % 2
    r_slot = 1 - w_slot

    @pl.when(jnp.logical_and(ib == 0, c == 0))
    def _no_block_before_the_first():
        tok_ref[r_slot] = jnp.zeros(tok_ref.shape[1:], F32)

    @pl.when(c == 0)
    def _build_gate_matrix():
        rid = lax.broadcasted_iota(jnp.int32, (N_KEYS, N_SEL), 0).astype(F32).astype(BF16)

        def body(it, carry):
            for k in range(GATE_GROUPS_PER_ITER):
                tg = it * GATE_GROUPS_PER_ITER + k
                r0 = pl.multiple_of(tg * TOKENS_PER_GATE_GROUP, TOKENS_PER_GATE_GROUP)
                rows = pl.ds(r0, TOKENS_PER_GATE_GROUP)
                _gate_tiles(gd_ref, tg, tok_ref[r_slot, 0, rows, :].astype(BF16),
                            tok_ref[r_slot, 1, rows, :].astype(BF16),
                            (0.5 * tok_ref[r_slot, 2, rows, :]).astype(BF16), rid)
            return carry

        lax.fori_loop(0, tb // (TOKENS_PER_GATE_GROUP * GATE_GROUPS_PER_ITER), body, 0)
        out_ref[...] = x1_ref[...]

    hd = c // 2
    half = c % 2
    base = pl.multiple_of(c * (ec // N_KEYS * SUBLANES), SUBLANES)
    h2 = h2_ref[...]

    def scores(part):
        qh_, ql_ = _split_bf16(q_ref[:, part * PEER_HALF:(part + 1) * PEER_HALF])
        kh, kl = kh_ref[hd * 2 + part], kl_ref[hd * 2 + part]
        return (lax.dot_general(kh, qh_, _NT, preferred_element_type=F32)
                + lax.dot_general(kh, ql_, _NT, preferred_element_type=F32)
                + lax.dot_general(kl, qh_, _NT, preferred_element_type=F32))

    def weights(j):
        a = lax.dot_general(h2, u_ref[j * EXPERT_PIECE:(j + 1) * EXPERT_PIECE, :], _NT, preferred_element_type=F32)
        gd = jnp.concatenate(
            [pltpu.bitcast(gd_ref[:, pl.ds(base + (grp_per_piece * j + g) * SUBLANES, SUBLANES), :]
                           .reshape(tb // 2, N_KEYS), BF16) for g in range(grp_per_piece)], axis=1)
        return gd * (a * (1.0 + lax.erf(a * (2.0 ** -0.5)))).astype(BF16)

    tops = [[None] * n_lg, [None] * n_lg]
    w_pieces = []
    for part in range(2):
        st = scores(part)
        for lg in range(n_lg):
            tops[part][lg] = _top16(st[:, lg * LANES:(lg + 1) * LANES], _row_iota(N_KEYS), float(N_KEYS))
            w_pieces.extend(weights(j) for j in range(len(w_pieces), min(len(w_pieces) + 2, n_pieces)))
    w_pieces.extend(weights(j) for j in range(len(w_pieces), n_pieces))
    sel = [_select_pairs(tops[0][lg], tops[1][lg]) for lg in range(n_lg)]
    rows = pl.ds(pl.multiple_of(hd * PEER_TOPK, PEER_TOPK), PEER_TOPK)
    for k in range(3):
        rs_ref[w_slot, half, k, rows, :] = jnp.concatenate([s_[k] for s_ in sel], axis=1)
    out_ref[...] += jnp.dot(jnp.concatenate(w_pieces, axis=1), v_ref[...], preferred_element_type=F32)

    @pl.when(c == pl.num_programs(1) - 1)
    def _routing_to_token_major():
        for hf in range(2):
            for k in range(3):
                tok_ref[w_slot, k, hf * half_tok:(hf + 1) * half_tok, :] = rs_ref[w_slot, hf, k].T


def _peer_call(x1, h2, pq, sub_keys, expert_u, expert_v, tb, ec):
    t = x1.shape[0]
    nb = t // tb
    nc = N_EXPERTS // ec
    assert nc == 2 * PEER_HEADS and tb % (2 * LANES) == 0 and ec % EXPERT_PIECE == 0
    keys = sub_keys.astype(F32).reshape(PEER_HEADS * 2, N_KEYS, PEER_HALF)
    kh, kl = _split_bf16(keys)
    prev_row = lambda c_: pl.BlockSpec((tb, c_), lambda i, c: (jnp.maximum(i - 1, 0), 0))
    chunk = pl.BlockSpec((ec, D_MODEL), lambda i, c: (c, 0))
    q_spec = pl.BlockSpec((tb // 2, 2 * PEER_HALF), lambda i, c: (jnp.minimum(i, nb - 1) * 2 + c % 2, c // 2))
    full3 = pl.BlockSpec((PEER_HEADS * 2, N_KEYS, PEER_HALF), lambda i, c: (0, 0, 0))
    in_specs = [prev_row(D_MODEL), prev_row(D_MODEL), q_spec, full3, full3, chunk, chunk]
    operands = [x1, h2, pq, kh, kl, expert_u.astype(BF16), expert_v.astype(BF16)]
    out_spec = prev_row(D_MODEL)
    out_shape = jax.ShapeDtypeStruct((t, D_MODEL), F32)
    scratch_shapes = [pltpu.VMEM((tb // TOKENS_PER_GATE_GROUP, N_KEYS * SUBLANES, N_KEYS), jnp.uint32),
                      pltpu.VMEM((2, 2, 3, N_SEL, tb // 2), F32),
                      pltpu.VMEM((2, 3, tb, N_SEL), F32)]
    return pl.pallas_call(
        functools.partial(_peer_kernel, tb=tb, ec=ec),
        grid=(nb + 1, nc),
        in_specs=in_specs,
        out_specs=out_spec,
        out_shape=out_shape,
        scratch_shapes=scratch_shapes,
        compiler_params=pltpu.CompilerParams(
            dimension_semantics=("arbitrary", "arbitrary"),
            vmem_limit_bytes=_vmem_limit(in_specs, operands, [out_spec], [out_shape], scratch_shapes)),
        name="peer_block_experts",
    )(*operands)


def _tile_sizes(bsz, s):
    t = bsz * s
    tm = math.gcd(t, 512)
    ts = math.gcd(s, 512)
    tb = math.gcd(t, 512)
    return tm, ts, tb


def kernel(x, norm_mix, norm_ffn, w_in, b_gate, w_conv, q_norm, k_norm, sinks, rel_bias, w_conv_out, w_attn_out,
           w_out, w_query, sub_keys, expert_u, expert_v):
    bsz, s, d = x.shape
    assert d == D_MODEL and s % BLOCK == 0
    tm, ts, tb = _tile_sizes(bsz, s)
    biasm = _band_bias(rel_bias)
    x2 = x.reshape(bsz * s, d)
    for l in range(norm_mix.shape[0]):
        hc, bc, qn, kv, gc, ga = _proj_call(x2, norm_mix[l], w_in[l], b_gate[l], q_norm[l], k_norm[l], tm)
        x1, h2, pq = _mixer_call(x2, hc, bc, qn, kv, gc, ga, w_conv[l], biasm, sinks[l], w_conv_out[l],
                                 w_attn_out[l], w_out[l], norm_ffn[l], w_query[l], bsz, s, ts)
        x2 = _peer_call(x1, h2, pq, sub_keys[l], expert_u[l], expert_v[l], tb, EXPERT_CHUNK)
    return x2.reshape(bsz, s, d)
```

```python
import functools
import math

import jax
import jax.numpy as jnp
import numpy as np
from jax import lax
from jax.experimental import pallas as pl
from jax.experimental.pallas import tpu as pltpu

F32 = jnp.float32
BF16 = jnp.bfloat16

D_MODEL = 1024
CONV_DIM = 512
CONV_K = 3
N_HEADS = 8
N_KV_HEADS = 2
HEAD_DIM = 64
Q_DIM = N_HEADS * HEAD_DIM
KV_DIM = N_KV_HEADS * HEAD_DIM
GROUP = N_HEADS // N_KV_HEADS
WINDOW = 128
BLOCK = 128
N_BUCKETS = 32
MAX_DISTANCE = 128
PEER_HEADS = 8
N_KEYS = 128
N_EXPERTS = N_KEYS * N_KEYS
PEER_QDIM = 256
PEER_HALF = PEER_QDIM // 2
PEER_TOPK = 16
N_SEL = PEER_HEADS * PEER_TOPK
EPS = 1e-6
NEG_BIG = -1e30
PAIR_W = 2 * HEAD_DIM
KV_WIDE = N_KV_HEADS * PAIR_W

SUBLANES = 8
LANES = 128

EXPERT_CHUNK = N_EXPERTS // (2 * PEER_HEADS)
EXPERT_PIECE = 2 * N_KEYS
TOKENS_PER_GATE_GROUP = 2 * SUBLANES
GATE_GROUPS_PER_ITER = 4
V7X_VMEM_BYTES = 64 * 1024 * 1024
COMPILER_TEMP_VMEM = 12 * 1024 * 1024

_OFFS = np.cumsum([0, CONV_DIM, CONV_DIM, CONV_DIM, Q_DIM, KV_DIM, KV_DIM, D_MODEL, D_MODEL]).tolist()

_NT = (((1,), (1,)), ((), ()))


def _vmem_limit(in_specs, operands, out_specs, out_shapes, scratch_shapes=()):
    nbytes = lambda shape, dtype: math.prod(shape) * jnp.dtype(dtype).itemsize
    windows = sum(nbytes(s.block_shape, a.dtype)
                  for s, a in zip(list(in_specs) + list(out_specs), list(operands) + list(out_shapes))
                  if s.block_shape is not None)
    need = 2 * windows + sum(nbytes(s.shape, s.dtype) for s in scratch_shapes) + COMPILER_TEMP_VMEM
    assert need <= V7X_VMEM_BYTES, need
    return need


def _split_bf16(a):
    hi = a.astype(BF16)
    lo = (a - hi.astype(F32)).astype(BF16)
    return hi, lo


def _proj_kernel(x_ref, g_ref, w_ref, bg_ref, qg_ref, kg_ref, avq_ref, avk_ref, exp_ref,
                 hc_ref, bc_ref, qn_ref, kv_ref, gc_ref, ga_ref):
    x = x_ref[...]
    ms = jnp.mean(x * x, axis=-1, keepdims=True)
    h = ((x * lax.rsqrt(ms + EPS)) * g_ref[...]).astype(BF16)

    def seg(i):
        return jnp.dot(h, w_ref[:, _OFFS[i]:_OFFS[i + 1]], preferred_element_type=F32)

    u = seg(0)
    hc_ref[...] = seg(2) * u
    bc_ref[...] = seg(1)

    def head_rms(a, av_ref):
        hi, lo = _split_bf16(a * a)
        return (jnp.dot(hi, av_ref[...], preferred_element_type=F32)
                + jnp.dot(lo, av_ref[...], preferred_element_type=F32))

    q = seg(3)
    qn = (q * lax.rsqrt(head_rms(q, avq_ref) + EPS)) * qg_ref[...] * (HEAD_DIM ** -0.5)
    qn_ref[...] = qn.astype(BF16)

    k = seg(4)
    kn = ((k * lax.rsqrt(head_rms(k, avk_ref) + EPS)) * kg_ref[...]).astype(BF16)
    v = seg(5).astype(BF16)
    kv_ref[...] = jnp.dot(jnp.concatenate([kn, v], axis=1), exp_ref[...], preferred_element_type=F32).astype(BF16)

    bg = bg_ref[...]
    gc_ref[...] = jax.nn.sigmoid(seg(6) + bg[:, :D_MODEL]).astype(BF16)
    ga_ref[...] = jax.nn.sigmoid(seg(7) + bg[:, D_MODEL:]).astype(BF16)


def _kv_layout():
    e = np.zeros((2 * KV_DIM, 3 * KV_WIDE), np.float32)
    for g in range(N_KV_HEADS):
        for d in range(HEAD_DIM):
            e[g * HEAD_DIM + d, g * PAIR_W + d] = 1.0
            e[g * HEAD_DIM + d, KV_WIDE + g * PAIR_W + HEAD_DIM + d] = 1.0
            e[KV_DIM + g * HEAD_DIM + d, 2 * KV_WIDE + g * PAIR_W + d] = 1.0
            e[KV_DIM + g * HEAD_DIM + d, 2 * KV_WIDE + g * PAIR_W + HEAD_DIM + d] = 1.0
    return jnp.asarray(e, BF16)


def _proj_call(x2, norm_mix, w_in, b_gate, q_norm, k_norm, tm):
    t = x2.shape[0]
    in_dim = w_in.shape[1]
    avq = jnp.kron(jnp.eye(N_HEADS, dtype=F32), jnp.full((HEAD_DIM, HEAD_DIM), 1.0 / HEAD_DIM, F32)).astype(BF16)
    avk = jnp.kron(jnp.eye(N_KV_HEADS, dtype=F32), jnp.full((HEAD_DIM, HEAD_DIM), 1.0 / HEAD_DIM, F32)).astype(BF16)
    qg = jnp.tile(q_norm.astype(F32), N_HEADS).reshape(1, Q_DIM)
    kg = jnp.tile(k_norm.astype(F32), N_KV_HEADS).reshape(1, KV_DIM)
    full = lambda shape: pl.BlockSpec(shape, lambda i: (0,) * len(shape))
    row = lambda c: pl.BlockSpec((tm, c), lambda i: (i, 0))
    in_specs = [row(D_MODEL), full((1, D_MODEL)), full((D_MODEL, in_dim)), full((1, 2 * D_MODEL)),
                full((1, Q_DIM)), full((1, KV_DIM)), full((Q_DIM, Q_DIM)), full((KV_DIM, KV_DIM)),
                full((2 * KV_DIM, 3 * KV_WIDE))]
    operands = [x2, norm_mix.reshape(1, D_MODEL).astype(F32), w_in.astype(BF16),
                b_gate.reshape(1, 2 * D_MODEL).astype(F32), qg, kg, avq, avk, _kv_layout()]
    out_specs = [row(CONV_DIM), row(CONV_DIM), row(Q_DIM), row(3 * KV_WIDE), row(D_MODEL), row(D_MODEL)]
    out_shape = [jax.ShapeDtypeStruct((t, CONV_DIM), F32), jax.ShapeDtypeStruct((t, CONV_DIM), F32),
                 jax.ShapeDtypeStruct((t, Q_DIM), BF16), jax.ShapeDtypeStruct((t, 3 * KV_WIDE), BF16),
                 jax.ShapeDtypeStruct((t, D_MODEL), BF16), jax.ShapeDtypeStruct((t, D_MODEL), BF16)]
    return pl.pallas_call(
        _proj_kernel,
        grid=(t // tm,),
        in_specs=in_specs,
        out_specs=out_specs,
        out_shape=out_shape,
        compiler_params=pltpu.CompilerParams(
            dimension_semantics=("arbitrary",),
            vmem_limit_bytes=_vmem_limit(in_specs, operands, out_specs, out_shape)),
        name="peer_block_proj",
    )(*operands)


def _mixer_kernel(x_ref, hc_ref, hcp_ref, bc_ref, qn_ref, kv_ref, kvp_ref, gc_ref, ga_ref,
                  wc_ref, bias_ref, sink_ref, wco_ref, wao_ref, wo_ref, g2_ref, wq_ref,
                  x1_ref, h2_ref, pq_ref, att_ref, *, ts):
    j = pl.program_id(1)
    first = j == 0

    hc = hc_ref[...]
    prev = jnp.where(first, 0.0, hcp_ref[...])
    rows = lax.broadcasted_iota(jnp.int32, hc.shape, 0)
    s1 = pltpu.roll(hc, 1, axis=0)
    s1 = jnp.where(rows == 0, prev[SUBLANES - 1:SUBLANES, :], s1)
    s2 = pltpu.roll(hc, 2, axis=0)
    s2 = jnp.where(rows == 0, prev[SUBLANES - 2:SUBLANES - 1, :], s2)
    s2 = jnp.where(rows == 1, prev[SUBLANES - 1:SUBLANES, :], s2)
    wc = wc_ref[...]
    conv = s2 * wc[0:1, :] + s1 * wc[1:2, :] + hc * wc[2:3, :]
    yc = (bc_ref[...] * conv).astype(BF16)
    y_conv = jnp.dot(yc, wco_ref[...], preferred_element_type=F32)

    kvfull = jnp.concatenate([kvp_ref[...], kv_ref[...]], axis=0)
    col = lax.broadcasted_iota(jnp.int32, (BLOCK, 2 * BLOCK), 1)
    pen0 = jnp.where(jnp.logical_and(first, col < BLOCK), NEG_BIG, 0.0)
    even_lanes = lax.broadcasted_iota(jnp.int32, (BLOCK, PAIR_W), 1) < HEAD_DIM
    for r in range(ts // BLOCK):
        kvb = kvfull[r * BLOCK:(r + 2) * BLOCK, :]
        for pair in range(N_HEADS // 2):
            g = (2 * pair) // GROUP
            qs = qn_ref[r * BLOCK:(r + 1) * BLOCK, pair * PAIR_W:(pair + 1) * PAIR_W]
            vsel = kvb[:, 2 * KV_WIDE + g * PAIR_W:2 * KV_WIDE + (g + 1) * PAIR_W]
            outs = []
            for par in range(2):
                hd = 2 * pair + par
                ksel = kvb[:, par * KV_WIDE + g * PAIR_W:par * KV_WIDE + (g + 1) * PAIR_W]
                lg = lax.dot_general(qs, ksel, _NT, preferred_element_type=F32) + bias_ref[hd]
                if r == 0:
                    lg = lg + pen0
                sink = sink_ref[hd]
                m = jnp.maximum(jnp.max(lg, axis=-1, keepdims=True), sink)
                p = jnp.exp(lg - m)
                denom = jnp.sum(p, axis=-1, keepdims=True) + jnp.exp(sink - m)
                pv = jnp.dot(p.astype(BF16), vsel, preferred_element_type=F32)
                outs.append(pv / denom)
            att_ref[r * BLOCK:(r + 1) * BLOCK, pair * PAIR_W:(pair + 1) * PAIR_W] = (
                jnp.where(even_lanes, outs[0], outs[1]).astype(BF16))
    y_attn = jnp.dot(att_ref[...], wao_ref[...], preferred_element_type=F32)

    mixed = (gc_ref[...].astype(F32) * y_conv + ga_ref[...].astype(F32) * y_attn).astype(BF16)
    x1 = x_ref[...] + jnp.dot(mixed, wo_ref[...], preferred_element_type=F32)
    x1_ref[...] = x1

    ms = jnp.mean(x1 * x1, axis=-1, keepdims=True)
    h2 = ((x1 * lax.rsqrt(ms + EPS)) * g2_ref[...]).astype(BF16)
    h2_ref[...] = h2
    pq_ref[...] = jnp.dot(h2, wq_ref[...], preferred_element_type=F32)


def _mixer_call(x2, hc, bc, qn, kv, gc, ga, w_conv, biasm, sinks, w_conv_out, w_attn_out, w_out,
                norm_ffn, w_query, bsz, s, ts):
    t = bsz * s
    qd = w_query.shape[1]
    nj = s // ts
    row = lambda c: pl.BlockSpec((ts, c), lambda b, j: (b * nj + j, 0))
    prev_blk = lambda c: pl.BlockSpec((BLOCK, c), lambda b, j: (jnp.maximum((b * nj + j) * (ts // BLOCK) - 1, 0), 0))
    prev8 = pl.BlockSpec((SUBLANES, CONV_DIM),
                         lambda b, j: (jnp.maximum((b * nj + j) * (ts // SUBLANES) - 1, 0), 0))
    full = lambda shape: pl.BlockSpec(shape, lambda b, j: (0,) * len(shape))
    in_specs = [row(D_MODEL), row(CONV_DIM), prev8, row(CONV_DIM), row(Q_DIM),
                row(3 * KV_WIDE), prev_blk(3 * KV_WIDE), row(D_MODEL), row(D_MODEL),
                full((CONV_K, CONV_DIM)), full((N_HEADS, BLOCK, 2 * BLOCK)),
                pl.BlockSpec(memory_space=pltpu.SMEM),
                full((CONV_DIM, D_MODEL)), full((Q_DIM, D_MODEL)), full((D_MODEL, D_MODEL)),
                full((1, D_MODEL)), full((D_MODEL, qd))]
    operands = [x2, hc, hc, bc, qn, kv, kv, gc, ga, w_conv.astype(F32), biasm, sinks.astype(F32),
                w_conv_out.astype(BF16), w_attn_out.astype(BF16), w_out.astype(BF16),
                norm_ffn.reshape(1, D_MODEL).astype(F32), w_query.astype(BF16)]
    out_specs = [row(D_MODEL), row(D_MODEL), row(qd)]
    out_shape = [jax.ShapeDtypeStruct((t, D_MODEL), F32), jax.ShapeDtypeStruct((t, D_MODEL), BF16),
                 jax.ShapeDtypeStruct((t, qd), F32)]
    scratch_shapes = [pltpu.VMEM((ts, Q_DIM), BF16)]
    return pl.pallas_call(
        functools.partial(_mixer_kernel, ts=ts),
        grid=(bsz, nj),
        in_specs=in_specs,
        out_specs=out_specs,
        out_shape=out_shape,
        scratch_shapes=scratch_shapes,
        compiler_params=pltpu.CompilerParams(
            dimension_semantics=("arbitrary", "arbitrary"),
            vmem_limit_bytes=_vmem_limit(in_specs, operands, out_specs, out_shape, scratch_shapes)),
        name="peer_block_mixer",
    )(*operands)


def _band_bias(rel_bias):
    q_loc = jnp.arange(BLOCK, dtype=jnp.int32)[:, None]
    k_loc = jnp.arange(2 * BLOCK, dtype=jnp.int32)[None, :]
    dist = q_loc + BLOCK - k_loc
    max_exact = N_BUCKETS // 2
    d = jnp.maximum(dist, 0)
    df = jnp.maximum(d, 1).astype(F32)
    large = max_exact + (jnp.log(df / max_exact) / math.log(MAX_DISTANCE / max_exact)
                         * (N_BUCKETS - max_exact)).astype(jnp.int32)
    large = jnp.minimum(large, N_BUCKETS - 1)
    bucket = jnp.where(d < max_exact, d, large)
    onehot = (bucket[None] == jnp.arange(N_BUCKETS, dtype=jnp.int32)[:, None, None]).astype(F32)
    bias = jnp.einsum("bh,bqk->hqk", rel_bias.astype(F32), onehot, precision=lax.Precision.HIGHEST)
    valid = (dist >= 0) & (dist < WINDOW)
    return jnp.where(valid[None], bias, NEG_BIG)


def _sorting_network(n):
    def merge(lo, hi, r):
        step = r * 2
        if step < hi - lo:
            yield from merge(lo, hi, step)
            yield from merge(lo + r, hi, step)
            yield from ((i, i + r) for i in range(lo + r, hi - r, step))
        else:
            yield (lo, lo + r)

    def sort(lo, hi):
        if hi > lo:
            mid = lo + (hi - lo) // 2
            yield from sort(lo, mid)
            yield from sort(mid + 1, hi)
            yield from merge(lo, hi, 1)

    return list(sort(0, n - 1))


def _top16(vals, rid, big):
    n = vals.shape[0] // SUBLANES
    x = [vals[k * SUBLANES:(k + 1) * SUBLANES, :] for k in range(n)]
    ids = [rid[k * SUBLANES:(k + 1) * SUBLANES, :] for k in range(n)]
    for i, j in _sorting_network(n):
        a, b, ia, ib = x[i], x[j], ids[i], ids[j]
        first = (a > b) | ((a == b) & (ia < ib))
        x[i], x[j] = jnp.maximum(a, b), jnp.minimum(a, b)
        ids[i], ids[j] = jnp.where(first, ia, ib), jnp.where(first, ib, ia)
    tv, ti = [], []
    for r in range(PEER_TOPK):
        m = jnp.max(x[0], axis=0, keepdims=True)
        idx = jnp.min(jnp.where(x[0] == m, ids[0], big), axis=0, keepdims=True)
        tv.append(m)
        ti.append(idx)
        won = ids[0] == idx
        for k in range(min(n - 1, PEER_TOPK - 1 - r)):
            x[k] = jnp.where(won, x[k + 1], x[k])
            ids[k] = jnp.where(won, ids[k + 1], ids[k])
        if n - 1 < PEER_TOPK - 1 - r:
            x[n - 1] = jnp.where(won, -jnp.inf, x[n - 1])
    return jnp.concatenate(tv, axis=0), jnp.concatenate(ti, axis=0)


def _row_iota(n_rows):
    return lax.broadcasted_iota(jnp.int32, (n_rows, LANES), 0).astype(F32)


def _pair_candidates(v0, v1):
    assert PEER_TOPK == 2 * SUBLANES
    half = SUBLANES // 2
    r8 = _row_iota(SUBLANES)
    r16 = _row_iota(PEER_TOPK)
    vals = [v0[0:1, :] + v1]
    pos = [r16]
    for a in (1, 2, 3):
        vals.append(v0[a:a + 1, :] + v1[0:SUBLANES, :])
        pos.append(r8 + float(a * PEER_TOPK))
    low = r8 < float(half)
    v1dup = jnp.where(low, v1[0:SUBLANES, :], pltpu.roll(v1[0:SUBLANES, :], half, axis=0))
    bdup = jnp.where(low, r8, r8 - float(half))
    for a in (4, 6):
        vals.append(jnp.where(low, v0[a:a + 1, :], v0[a + 1:a + 2, :]) + v1dup)
        pos.append(jnp.where(low, float(a * PEER_TOPK), float((a + 1) * PEER_TOPK)) + bdup)
    vals.append(v0[SUBLANES:, :] + v1[0:1, :])
    pos.append((r8 + float(SUBLANES)) * float(PEER_TOPK))
    return jnp.concatenate(vals, axis=0), jnp.concatenate(pos, axis=0)


def _pick(table, sel):
    out = jnp.zeros_like(sel)
    for a in range(PEER_TOPK):
        out = jnp.where(sel == float(a), table[a:a + 1, :], out)
    return out


def _select_pairs(top0, top1):
    (v0, j0), (v1, j1) = top0, top1
    cand, cpos = _pair_candidates(v0, v1)
    ts_, pos = _top16(cand, cpos, float(PEER_TOPK * PEER_TOPK))
    pa = jnp.floor(pos * (1.0 / PEER_TOPK))
    pb = pos - pa * PEER_TOPK
    e = jnp.exp(ts_ - ts_[0:1, :])
    return _pick(j0, pa), _pick(j1, pb), e / jnp.sum(e, axis=0, keepdims=True)


def _gate_tiles(gd_ref, tg, i0b, i1b, gb, rid):
    zero = jnp.zeros_like(rid)
    one = jnp.ones_like(rid)
    for p in range(SUBLANES):
        tiles = []
        for t in (2 * p, 2 * p + 1):
            rt = jnp.where(rid == i0b[t:t + 1, :], gb[t:t + 1, :], zero)
            ct = jnp.where(rid == i1b[t:t + 1, :], one, zero)
            tiles.append(lax.dot_general(rt, ct, _NT, preferred_element_type=F32))
        lo_, hi_ = (pltpu.bitcast(t_.astype(BF16).astype(F32), jnp.uint32) for t_ in tiles)
        gd_ref[tg, pl.ds(p, N_KEYS, stride=SUBLANES), :] = (
            lax.shift_right_logical(lo_, jnp.uint32(16)) | (hi_ & jnp.uint32(0xFFFF0000)))


def _peer_kernel(x1_ref, h2_ref, q_ref, kh_ref, kl_ref, u_ref, v_ref, out_ref, gd_ref, rs_ref, tok_ref, *, tb, ec):
    ib = pl.program_id(0)
    c = pl.program_id(1)
    half_tok = tb // 2
    n_lg = half_tok // LANES
    n_pieces = ec // EXPERT_PIECE
    grp_per_piece = EXPERT_PIECE // N_KEYS
    w_slot = ib % 2
    r_slot = 1 - w_slot

    @pl.when(jnp.logical_and(ib == 0, c == 0))
    def _no_block_before_the_first():
        tok_ref[r_slot] = jnp.zeros(tok_ref.shape[1:], F32)

    @pl.when(c == 0)
    def _build_gate_matrix():
        rid = lax.broadcasted_iota(jnp.int32, (N_KEYS, N_SEL), 0).astype(F32).astype(BF16)

        def body(it, carry):
            for k in range(GATE_GROUPS_PER_ITER):
                tg = it * GATE_GROUPS_PER_ITER + k
                r0 = pl.multiple_of(tg * TOKENS_PER_GATE_GROUP, TOKENS_PER_GATE_GROUP)
                rows = pl.ds(r0, TOKENS_PER_GATE_GROUP)
                _gate_tiles(gd_ref, tg, tok_ref[r_slot, 0, rows, :].astype(BF16),
                            tok_ref[r_slot, 1, rows, :].astype(BF16),
                            (0.5 * tok_ref[r_slot, 2, rows, :]).astype(BF16), rid)
            return carry

        lax.fori_loop(0, tb // (TOKENS_PER_GATE_GROUP * GATE_GROUPS_PER_ITER), body, 0)
        out_ref[...] = x1_ref[...]

    hd = c // 2
    half = c % 2
    base = pl.multiple_of(c * (ec // N_KEYS * SUBLANES), SUBLANES)
    h2 = h2_ref[...]

    def scores(part):
        qh_, ql_ = _split_bf16(q_ref[:, part * PEER_HALF:(part + 1) * PEER_HALF])
        kh, kl = kh_ref[hd * 2 + part], kl_ref[hd * 2 + part]
        return (lax.dot_general(kh, qh_, _NT, preferred_element_type=F32)
                + lax.dot_general(kh, ql_, _NT, preferred_element_type=F32)
                + lax.dot_general(kl, qh_, _NT, preferred_element_type=F32))

    def weights(j):
        a = lax.dot_general(h2, u_ref[j * EXPERT_PIECE:(j + 1) * EXPERT_PIECE, :], _NT, preferred_element_type=F32)
        gd = jnp.concatenate(
            [pltpu.bitcast(gd_ref[:, pl.ds(base + (grp_per_piece * j + g) * SUBLANES, SUBLANES), :]
                           .reshape(tb // 2, N_KEYS), BF16) for g in range(grp_per_piece)], axis=1)
        return gd * (a * (1.0 + lax.erf(a * (2.0 ** -0.5)))).astype(BF16)

    tops = [[None] * n_lg, [None] * n_lg]
    w_pieces = []
    for part in range(2):
        st = scores(part)
        for lg in range(n_lg):
            tops[part][lg] = _top16(st[:, lg * LANES:(lg + 1) * LANES], _row_iota(N_KEYS), float(N_KEYS))
            w_pieces.extend(weights(j) for j in range(len(w_pieces), min(len(w_pieces) + 2, n_pieces)))
    w_pieces.extend(weights(j) for j in range(len(w_pieces), n_pieces))
    acc = jnp.dot(jnp.concatenate(w_pieces, axis=1), v_ref[...], preferred_element_type=F32)
    bits = pltpu.bitcast(acc[0:SUBLANES, 0:LANES], jnp.uint32)
    zero = jnp.max(lax.shift_right_logical(lax.shift_right_logical(bits, jnp.uint32(16)), jnp.uint32(16)).astype(F32),
                   axis=0, keepdims=True)
    sel = [_select_pairs((tops[0][lg][0] + zero, tops[0][lg][1]), tops[1][lg]) for lg in range(n_lg)]
    rows = pl.ds(pl.multiple_of(hd * PEER_TOPK, PEER_TOPK), PEER_TOPK)
    for k in range(3):
        rs_ref[w_slot, half, k, rows, :] = jnp.concatenate([s_[k] for s_ in sel], axis=1)
    out_ref[...] += acc

    @pl.when(c == pl.num_programs(1) - 1)
    def _routing_to_token_major():
        for hf in range(2):
            for k in range(3):
                tok_ref[w_slot, k, hf * half_tok:(hf + 1) * half_tok, :] = rs_ref[w_slot, hf, k].T


def _peer_call(x1, h2, pq, sub_keys, expert_u, expert_v, tb, ec):
    t = x1.shape[0]
    nb = t // tb
    nc = N_EXPERTS // ec
    assert nc == 2 * PEER_HEADS and tb % (2 * LANES) == 0 and ec % EXPERT_PIECE == 0
    keys = sub_keys.astype(F32).reshape(PEER_HEADS * 2, N_KEYS, PEER_HALF)
    kh, kl = _split_bf16(keys)
    prev_row = lambda c_: pl.BlockSpec((tb, c_), lambda i, c: (jnp.maximum(i - 1, 0), 0))
    chunk = pl.BlockSpec((ec, D_MODEL), lambda i, c: (c, 0))
    q_spec = pl.BlockSpec((tb // 2, 2 * PEER_HALF), lambda i, c: (jnp.minimum(i, nb - 1) * 2 + c % 2, c // 2))
    full3 = pl.BlockSpec((PEER_HEADS * 2, N_KEYS, PEER_HALF), lambda i, c: (0, 0, 0))
    in_specs = [prev_row(D_MODEL), prev_row(D_MODEL), q_spec, full3, full3, chunk, chunk]
    operands = [x1, h2, pq, kh, kl, expert_u.astype(BF16), expert_v.astype(BF16)]
    out_spec = prev_row(D_MODEL)
    out_shape = jax.ShapeDtypeStruct((t, D_MODEL), F32)
    scratch_shapes = [pltpu.VMEM((tb // TOKENS_PER_GATE_GROUP, N_KEYS * SUBLANES, N_KEYS), jnp.uint32),
                      pltpu.VMEM((2, 2, 3, N_SEL, tb // 2), F32),
                      pltpu.VMEM((2, 3, tb, N_SEL), F32)]
    return pl.pallas_call(
        functools.partial(_peer_kernel, tb=tb, ec=ec),
        grid=(nb + 1, nc),
        in_specs=in_specs,
        out_specs=out_spec,
        out_shape=out_shape,
        scratch_shapes=scratch_shapes,
        compiler_params=pltpu.CompilerParams(
            dimension_semantics=("arbitrary", "arbitrary"),
            vmem_limit_bytes=_vmem_limit(in_specs, operands, [out_spec], [out_shape], scratch_shapes)),
        name="peer_block_experts",
    )(*operands)


def _tile_sizes(bsz, s):
    t = bsz * s
    tm = math.gcd(t, 512)
    ts = math.gcd(s, 512)
    tb = math.gcd(t, 512)
    return tm, ts, tb


def kernel(x, norm_mix, norm_ffn, w_in, b_gate, w_conv, q_norm, k_norm, sinks, rel_bias, w_conv_out, w_attn_out,
           w_out, w_query, sub_keys, expert_u, expert_v):
    bsz, s, d = x.shape
    assert d == D_MODEL and s % BLOCK == 0
    tm, ts, tb = _tile_sizes(bsz, s)
    biasm = _band_bias(rel_bias)
    x2 = x.reshape(bsz * s, d)
    for l in range(norm_mix.shape[0]):
        hc, bc, qn, kv, gc, ga = _proj_call(x2, norm_mix[l], w_in[l], b_gate[l], q_norm[l], k_norm[l], tm)
        x1, h2, pq = _mixer_call(x2, hc, bc, qn, kv, gc, ga, w_conv[l], biasm, sinks[l], w_conv_out[l],
                                 w_attn_out[l], w_out[l], norm_ffn[l], w_query[l], bsz, s, ts)
        x2 = _peer_call(x1, h2, pq, sub_keys[l], expert_u[l], expert_v[l], tb, EXPERT_CHUNK)
    return x2.reshape(bsz, s, d)
```

```python
import functools
import math

import jax
import jax.numpy as jnp
import numpy as np
from jax import lax
from jax.experimental import pallas as pl
from jax.experimental.pallas import tpu as pltpu

F32 = jnp.float32
BF16 = jnp.bfloat16

D_MODEL = 1024
CONV_DIM = 512
CONV_K = 3
N_HEADS = 8
N_KV_HEADS = 2
HEAD_DIM = 64
Q_DIM = N_HEADS * HEAD_DIM
KV_DIM = N_KV_HEADS * HEAD_DIM
GROUP = N_HEADS // N_KV_HEADS
WINDOW = 128
BLOCK = 128
N_BUCKETS = 32
MAX_DISTANCE = 128
PEER_HEADS = 8
N_KEYS = 128
N_EXPERTS = N_KEYS * N_KEYS
PEER_QDIM = 256
PEER_HALF = PEER_QDIM // 2
PEER_TOPK = 16
N_SEL = PEER_HEADS * PEER_TOPK
EPS = 1e-6
NEG_BIG = -1e30
PAIR_W = 2 * HEAD_DIM
KV_WIDE = N_KV_HEADS * PAIR_W

SUBLANES = 8
LANES = 128

EXPERT_CHUNK = N_EXPERTS // (2 * PEER_HEADS)
EXPERT_PIECE = 2 * N_KEYS
TOKENS_PER_GATE_GROUP = 2 * SUBLANES
GATE_GROUPS_PER_ITER = 4
V7X_VMEM_BYTES = 64 * 1024 * 1024
COMPILER_TEMP_VMEM = 12 * 1024 * 1024

_OFFS = np.cumsum([0, CONV_DIM, CONV_DIM, CONV_DIM, Q_DIM, KV_DIM, KV_DIM, D_MODEL, D_MODEL]).tolist()

_NT = (((1,), (1,)), ((), ()))


def _vmem_limit(in_specs, operands, out_specs, out_shapes, scratch_shapes=()):
    nbytes = lambda shape, dtype: math.prod(shape) * jnp.dtype(dtype).itemsize
    windows = sum(nbytes(s.block_shape, a.dtype)
                  for s, a in zip(list(in_specs) + list(out_specs), list(operands) + list(out_shapes))
                  if s.block_shape is not None)
    need = 2 * windows + sum(nbytes(s.shape, s.dtype) for s in scratch_shapes) + COMPILER_TEMP_VMEM
    assert need <= V7X_VMEM_BYTES, need
    return need


def _split_bf16(a):
    hi = a.astype(BF16)
    lo = (a - hi.astype(F32)).astype(BF16)
    return hi, lo


def _proj_kernel(x_ref, g_ref, w_ref, bg_ref, qg_ref, kg_ref, avq_ref, avk_ref, exp_ref,
                 hc_ref, bc_ref, qn_ref, kv_ref, gc_ref, ga_ref):
    x = x_ref[...]
    ms = jnp.mean(x * x, axis=-1, keepdims=True)
    h = ((x * lax.rsqrt(ms + EPS)) * g_ref[...]).astype(BF16)

    def seg(i):
        return jnp.dot(h, w_ref[:, _OFFS[i]:_OFFS[i + 1]], preferred_element_type=F32)

    u = seg(0)
    hc_ref[...] = seg(2) * u
    bc_ref[...] = seg(1)

    def head_rms(a, av_ref):
        hi, lo = _split_bf16(a * a)
        return (jnp.dot(hi, av_ref[...], preferred_element_type=F32)
                + jnp.dot(lo, av_ref[...], preferred_element_type=F32))

    q = seg(3)
    qn = (q * lax.rsqrt(head_rms(q, avq_ref) + EPS)) * qg_ref[...] * (HEAD_DIM ** -0.5)
    qn_ref[...] = qn.astype(BF16)

    k = seg(4)
    kn = ((k * lax.rsqrt(head_rms(k, avk_ref) + EPS)) * kg_ref[...]).astype(BF16)
    v = seg(5).astype(BF16)
    kv_ref[...] = jnp.dot(jnp.concatenate([kn, v], axis=1), exp_ref[...], preferred_element_type=F32).astype(BF16)

    bg = bg_ref[...]
    gc_ref[...] = jax.nn.sigmoid(seg(6) + bg[:, :D_MODEL]).astype(BF16)
    ga_ref[...] = jax.nn.sigmoid(seg(7) + bg[:, D_MODEL:]).astype(BF16)


def _kv_layout():
    e = np.zeros((2 * KV_DIM, 3 * KV_WIDE), np.float32)
    for g in range(N_KV_HEADS):
        for d in range(HEAD_DIM):
            e[g * HEAD_DIM + d, g * PAIR_W + d] = 1.0
            e[g * HEAD_DIM + d, KV_WIDE + g * PAIR_W + HEAD_DIM + d] = 1.0
            e[KV_DIM + g * HEAD_DIM + d, 2 * KV_WIDE + g * PAIR_W + d] = 1.0
            e[KV_DIM + g * HEAD_DIM + d, 2 * KV_WIDE + g * PAIR_W + HEAD_DIM + d] = 1.0
    return jnp.asarray(e, BF16)


def _proj_call(x2, norm_mix, w_in, b_gate, q_norm, k_norm, tm):
    t = x2.shape[0]
    in_dim = w_in.shape[1]
    avq = jnp.kron(jnp.eye(N_HEADS, dtype=F32), jnp.full((HEAD_DIM, HEAD_DIM), 1.0 / HEAD_DIM, F32)).astype(BF16)
    avk = jnp.kron(jnp.eye(N_KV_HEADS, dtype=F32), jnp.full((HEAD_DIM, HEAD_DIM), 1.0 / HEAD_DIM, F32)).astype(BF16)
    qg = jnp.tile(q_norm.astype(F32), N_HEADS).reshape(1, Q_DIM)
    kg = jnp.tile(k_norm.astype(F32), N_KV_HEADS).reshape(1, KV_DIM)
    full = lambda shape: pl.BlockSpec(shape, lambda i: (0,) * len(shape))
    row = lambda c: pl.BlockSpec((tm, c), lambda i: (i, 0))
    in_specs = [row(D_MODEL), full((1, D_MODEL)), full((D_MODEL, in_dim)), full((1, 2 * D_MODEL)),
                full((1, Q_DIM)), full((1, KV_DIM)), full((Q_DIM, Q_DIM)), full((KV_DIM, KV_DIM)),
                full((2 * KV_DIM, 3 * KV_WIDE))]
    operands = [x2, norm_mix.reshape(1, D_MODEL).astype(F32), w_in.astype(BF16),
                b_gate.reshape(1, 2 * D_MODEL).astype(F32), qg, kg, avq, avk, _kv_layout()]
    out_specs = [row(CONV_DIM), row(CONV_DIM), row(Q_DIM), row(3 * KV_WIDE), row(D_MODEL), row(D_MODEL)]
    out_shape = [jax.ShapeDtypeStruct((t, CONV_DIM), F32), jax.ShapeDtypeStruct((t, CONV_DIM), F32),
                 jax.ShapeDtypeStruct((t, Q_DIM), BF16), jax.ShapeDtypeStruct((t, 3 * KV_WIDE), BF16),
                 jax.ShapeDtypeStruct((t, D_MODEL), BF16), jax.ShapeDtypeStruct((t, D_MODEL), BF16)]
    return pl.pallas_call(
        _proj_kernel,
        grid=(t // tm,),
        in_specs=in_specs,
        out_specs=out_specs,
        out_shape=out_shape,
        compiler_params=pltpu.CompilerParams(
            dimension_semantics=("arbitrary",),
            vmem_limit_bytes=_vmem_limit(in_specs, operands, out_specs, out_shape)),
        name="peer_block_proj",
    )(*operands)


def _mixer_kernel(x_ref, hc_ref, hcp_ref, bc_ref, qn_ref, kv_ref, kvp_ref, gc_ref, ga_ref,
                  wc_ref, bias_ref, sink_ref, wco_ref, wao_ref, wo_ref, g2_ref, wq_ref,
                  x1_ref, h2_ref, pq_ref, att_ref, *, ts):
    j = pl.program_id(1)
    first = j == 0

    hc = hc_ref[...]
    prev = jnp.where(first, 0.0, hcp_ref[...])
    rows = lax.broadcasted_iota(jnp.int32, hc.shape, 0)
    s1 = pltpu.roll(hc, 1, axis=0)
    s1 = jnp.where(rows == 0, prev[SUBLANES - 1:SUBLANES, :], s1)
    s2 = pltpu.roll(hc, 2, axis=0)
    s2 = jnp.where(rows == 0, prev[SUBLANES - 2:SUBLANES - 1, :], s2)
    s2 = jnp.where(rows == 1, prev[SUBLANES - 1:SUBLANES, :], s2)
    wc = wc_ref[...]
    conv = s2 * wc[0:1, :] + s1 * wc[1:2, :] + hc * wc[2:3, :]
    yc = (bc_ref[...] * conv).astype(BF16)
    y_conv = jnp.dot(yc, wco_ref[...], preferred_element_type=F32)

    kvfull = jnp.concatenate([kvp_ref[...], kv_ref[...]], axis=0)
    col = lax.broadcasted_iota(jnp.int32, (BLOCK, 2 * BLOCK), 1)
    pen0 = jnp.where(jnp.logical_and(first, col < BLOCK), NEG_BIG, 0.0)
    even_lanes = lax.broadcasted_iota(jnp.int32, (BLOCK, PAIR_W), 1) < HEAD_DIM
    for r in range(ts // BLOCK):
        kvb = kvfull[r * BLOCK:(r + 2) * BLOCK, :]
        for pair in range(N_HEADS // 2):
            g = (2 * pair) // GROUP
            qs = qn_ref[r * BLOCK:(r + 1) * BLOCK, pair * PAIR_W:(pair + 1) * PAIR_W]
            vsel = kvb[:, 2 * KV_WIDE + g * PAIR_W:2 * KV_WIDE + (g + 1) * PAIR_W]
            outs = []
            for par in range(2):
                hd = 2 * pair + par
                ksel = kvb[:, par * KV_WIDE + g * PAIR_W:par * KV_WIDE + (g + 1) * PAIR_W]
                lg = lax.dot_general(qs, ksel, _NT, preferred_element_type=F32) + bias_ref[hd]
                if r == 0:
                    lg = lg + pen0
                sink = sink_ref[hd]
                m = jnp.maximum(jnp.max(lg, axis=-1, keepdims=True), sink)
                p = jnp.exp(lg - m)
                denom = jnp.sum(p, axis=-1, keepdims=True) + jnp.exp(sink - m)
                pv = jnp.dot(p.astype(BF16), vsel, preferred_element_type=F32)
                outs.append(pv / denom)
            att_ref[r * BLOCK:(r + 1) * BLOCK, pair * PAIR_W:(pair + 1) * PAIR_W] = (
                jnp.where(even_lanes, outs[0], outs[1]).astype(BF16))
    y_attn = jnp.dot(att_ref[...], wao_ref[...], preferred_element_type=F32)

    mixed = (gc_ref[...].astype(F32) * y_conv + ga_ref[...].astype(F32) * y_attn).astype(BF16)
    x1 = x_ref[...] + jnp.dot(mixed, wo_ref[...], preferred_element_type=F32)
    x1_ref[...] = x1

    ms = jnp.mean(x1 * x1, axis=-1, keepdims=True)
    h2 = ((x1 * lax.rsqrt(ms + EPS)) * g2_ref[...]).astype(BF16)
    h2_ref[...] = h2
    pq_ref[...] = jnp.dot(h2, wq_ref[...], preferred_element_type=F32)


def _mixer_call(x2, hc, bc, qn, kv, gc, ga, w_conv, biasm, sinks, w_conv_out, w_attn_out, w_out,
                norm_ffn, w_query, bsz, s, ts):
    t = bsz * s
    qd = w_query.shape[1]
    nj = s // ts
    row = lambda c: pl.BlockSpec((ts, c), lambda b, j: (b * nj + j, 0))
    prev_blk = lambda c: pl.BlockSpec((BLOCK, c), lambda b, j: (jnp.maximum((b * nj + j) * (ts // BLOCK) - 1, 0), 0))
    prev8 = pl.BlockSpec((SUBLANES, CONV_DIM),
                         lambda b, j: (jnp.maximum((b * nj + j) * (ts // SUBLANES) - 1, 0), 0))
    full = lambda shape: pl.BlockSpec(shape, lambda b, j: (0,) * len(shape))
    in_specs = [row(D_MODEL), row(CONV_DIM), prev8, row(CONV_DIM), row(Q_DIM),
                row(3 * KV_WIDE), prev_blk(3 * KV_WIDE), row(D_MODEL), row(D_MODEL),
                full((CONV_K, CONV_DIM)), full((N_HEADS, BLOCK, 2 * BLOCK)),
                pl.BlockSpec(memory_space=pltpu.SMEM),
                full((CONV_DIM, D_MODEL)), full((Q_DIM, D_MODEL)), full((D_MODEL, D_MODEL)),
                full((1, D_MODEL)), full((D_MODEL, qd))]
    operands = [x2, hc, hc, bc, qn, kv, kv, gc, ga, w_conv.astype(F32), biasm, sinks.astype(F32),
                w_conv_out.astype(BF16), w_attn_out.astype(BF16), w_out.astype(BF16),
                norm_ffn.reshape(1, D_MODEL).astype(F32), w_query.astype(BF16)]
    out_specs = [row(D_MODEL), row(D_MODEL), row(qd)]
    out_shape = [jax.ShapeDtypeStruct((t, D_MODEL), F32), jax.ShapeDtypeStruct((t, D_MODEL), BF16),
                 jax.ShapeDtypeStruct((t, qd), F32)]
    scratch_shapes = [pltpu.VMEM((ts, Q_DIM), BF16)]
    return pl.pallas_call(
        functools.partial(_mixer_kernel, ts=ts),
        grid=(bsz, nj),
        in_specs=in_specs,
        out_specs=out_specs,
        out_shape=out_shape,
        scratch_shapes=scratch_shapes,
        compiler_params=pltpu.CompilerParams(
            dimension_semantics=("arbitrary", "arbitrary"),
            vmem_limit_bytes=_vmem_limit(in_specs, operands, out_specs, out_shape, scratch_shapes)),
        name="peer_block_mixer",
    )(*operands)


def _band_bias(rel_bias):
    q_loc = jnp.arange(BLOCK, dtype=jnp.int32)[:, None]
    k_loc = jnp.arange(2 * BLOCK, dtype=jnp.int32)[None, :]
    dist = q_loc + BLOCK - k_loc
    max_exact = N_BUCKETS // 2
    d = jnp.maximum(dist, 0)
    df = jnp.maximum(d, 1).astype(F32)
    large = max_exact + (jnp.log(df / max_exact) / math.log(MAX_DISTANCE / max_exact)
                         * (N_BUCKETS - max_exact)).astype(jnp.int32)
    large = jnp.minimum(large, N_BUCKETS - 1)
    bucket = jnp.where(d < max_exact, d, large)
    onehot = (bucket[None] == jnp.arange(N_BUCKETS, dtype=jnp.int32)[:, None, None]).astype(F32)
    bias = jnp.einsum("bh,bqk->hqk", rel_bias.astype(F32), onehot, precision=lax.Precision.HIGHEST)
    valid = (dist >= 0) & (dist < WINDOW)
    return jnp.where(valid[None], bias, NEG_BIG)


def _sorting_network(n):
    def merge(lo, hi, r):
        step = r * 2
        if step < hi - lo:
            yield from merge(lo, hi, step)
            yield from merge(lo + r, hi, step)
            yield from ((i, i + r) for i in range(lo + r, hi - r, step))
        else:
            yield (lo, lo + r)

    def sort(lo, hi):
        if hi > lo:
            mid = lo + (hi - lo) // 2
            yield from sort(lo, mid)
            yield from sort(mid + 1, hi)
            yield from merge(lo, hi, 1)

    return list(sort(0, n - 1))


def _top16(vals, rid, big):
    n = vals.shape[0] // SUBLANES
    x = [vals[k * SUBLANES:(k + 1) * SUBLANES, :] for k in range(n)]
    ids = [rid[k * SUBLANES:(k + 1) * SUBLANES, :] for k in range(n)]
    for i, j in _sorting_network(n):
        a, b, ia, ib = x[i], x[j], ids[i], ids[j]
        first = (a > b) | ((a == b) & (ia < ib))
        x[i], x[j] = jnp.maximum(a, b), jnp.minimum(a, b)
        ids[i], ids[j] = jnp.where(first, ia, ib), jnp.where(first, ib, ia)
    tv, ti = [], []
    for r in range(PEER_TOPK):
        m = jnp.max(x[0], axis=0, keepdims=True)
        idx = jnp.min(jnp.where(x[0] == m, ids[0], big), axis=0, keepdims=True)
        tv.append(m)
        ti.append(idx)
        won = ids[0] == idx
        for k in range(min(n - 1, PEER_TOPK - 1 - r)):
            x[k] = jnp.where(won, x[k + 1], x[k])
            ids[k] = jnp.where(won, ids[k + 1], ids[k])
        if n - 1 < PEER_TOPK - 1 - r:
            x[n - 1] = jnp.where(won, -jnp.inf, x[n - 1])
    return jnp.concatenate(tv, axis=0), jnp.concatenate(ti, axis=0)


def _row_iota(n_rows):
    return lax.broadcasted_iota(jnp.int32, (n_rows, LANES), 0).astype(F32)


def _pair_candidates(v0, v1):
    assert PEER_TOPK == 2 * SUBLANES
    half = SUBLANES // 2
    r8 = _row_iota(SUBLANES)
    r16 = _row_iota(PEER_TOPK)
    vals = [v0[0:1, :] + v1]
    pos = [r16]
    for a in (1, 2, 3):
        vals.append(v0[a:a + 1, :] + v1[0:SUBLANES, :])
        pos.append(r8 + float(a * PEER_TOPK))
    low = r8 < float(half)
    v1dup = jnp.where(low, v1[0:SUBLANES, :], pltpu.roll(v1[0:SUBLANES, :], half, axis=0))
    bdup = jnp.where(low, r8, r8 - float(half))
    for a in (4, 6):
        vals.append(jnp.where(low, v0[a:a + 1, :], v0[a + 1:a + 2, :]) + v1dup)
        pos.append(jnp.where(low, float(a * PEER_TOPK), float((a + 1) * PEER_TOPK)) + bdup)
    vals.append(v0[SUBLANES:, :] + v1[0:1, :])
    pos.append((r8 + float(SUBLANES)) * float(PEER_TOPK))
    return jnp.concatenate(vals, axis=0), jnp.concatenate(pos, axis=0)


def _pick(table, sel):
    out = jnp.zeros_like(sel)
    for a in range(PEER_TOPK):
        out = jnp.where(sel == float(a), table[a:a + 1, :], out)
    return out


def _select_pairs(top0, top1):
    (v0, j0), (v1, j1) = top0, top1
    cand, cpos = _pair_candidates(v0, v1)
    ts_, pos = _top16(cand, cpos, float(PEER_TOPK * PEER_TOPK))
    pa = jnp.floor(pos * (1.0 / PEER_TOPK))
    pb = pos - pa * PEER_TOPK
    e = jnp.exp(ts_ - ts_[0:1, :])
    return _pick(j0, pa), _pick(j1, pb), e / jnp.sum(e, axis=0, keepdims=True)


def _gate_tiles(gd_ref, tg, i0b, i1b, gb, rid):
    zero = jnp.zeros_like(rid)
    one = jnp.ones_like(rid)
    for p in range(SUBLANES):
        tiles = []
        for t in (2 * p, 2 * p + 1):
            rt = jnp.where(rid == i0b[t:t + 1, :], gb[t:t + 1, :], zero)
            ct = jnp.where(rid == i1b[t:t + 1, :], one, zero)
            tiles.append(lax.dot_general(rt, ct, _NT, preferred_element_type=F32))
        lo_, hi_ = (pltpu.bitcast(t_.astype(BF16).astype(F32), jnp.uint32) for t_ in tiles)
        gd_ref[tg, pl.ds(p, N_KEYS, stride=SUBLANES), :] = (
            lax.shift_right_logical(lo_, jnp.uint32(16)) | (hi_ & jnp.uint32(0xFFFF0000)))


def _peer_kernel(x1_ref, h2_ref, q_ref, kh_ref, kl_ref, ut_ref, v_ref, out_ref, gd_ref, rs_ref, tok_ref, *, tb, ec):
    ib = pl.program_id(0)
    c = pl.program_id(1)
    half_tok = tb // 2
    n_lg = half_tok // LANES
    n_pieces = ec // EXPERT_PIECE
    grp_per_piece = EXPERT_PIECE // N_KEYS
    w_slot = ib % 2
    r_slot = 1 - w_slot

    @pl.when(jnp.logical_and(ib == 0, c == 0))
    def _no_block_before_the_first():
        tok_ref[r_slot] = jnp.zeros(tok_ref.shape[1:], F32)

    @pl.when(c == 0)
    def _build_gate_matrix():
        rid = lax.broadcasted_iota(jnp.int32, (N_KEYS, N_SEL), 0).astype(F32).astype(BF16)

        def body(it, carry):
            for k in range(GATE_GROUPS_PER_ITER):
                tg = it * GATE_GROUPS_PER_ITER + k
                r0 = pl.multiple_of(tg * TOKENS_PER_GATE_GROUP, TOKENS_PER_GATE_GROUP)
                rows = pl.ds(r0, TOKENS_PER_GATE_GROUP)
                _gate_tiles(gd_ref, tg, tok_ref[r_slot, 0, rows, :].astype(BF16),
                            tok_ref[r_slot, 1, rows, :].astype(BF16),
                            (0.5 * tok_ref[r_slot, 2, rows, :]).astype(BF16), rid)
            return carry

        lax.fori_loop(0, tb // (TOKENS_PER_GATE_GROUP * GATE_GROUPS_PER_ITER), body, 0)
        out_ref[...] = x1_ref[...]

    hd = c // 2
    half = c % 2
    base = pl.multiple_of(c * (ec // N_KEYS * SUBLANES), SUBLANES)
    h2 = h2_ref[...]

    def scores(part):
        qh_, ql_ = _split_bf16(q_ref[:, part * PEER_HALF:(part + 1) * PEER_HALF])
        kh, kl = kh_ref[hd * 2 + part], kl_ref[hd * 2 + part]
        return (lax.dot_general(kh, qh_, _NT, preferred_element_type=F32)
                + lax.dot_general(kh, ql_, _NT, preferred_element_type=F32)
                + lax.dot_general(kl, qh_, _NT, preferred_element_type=F32))

    def weights(j):
        a = jnp.dot(h2, ut_ref[:, j * EXPERT_PIECE:(j + 1) * EXPERT_PIECE], preferred_element_type=F32)
        gd = jnp.concatenate(
            [pltpu.bitcast(gd_ref[:, pl.ds(base + (grp_per_piece * j + g) * SUBLANES, SUBLANES), :]
                           .reshape(tb // 2, N_KEYS), BF16) for g in range(grp_per_piece)], axis=1)
        return gd * (a * (1.0 + lax.erf(a * (2.0 ** -0.5)))).astype(BF16)

    tops = [[None] * n_lg, [None] * n_lg]
    w_pieces = []
    for part in range(2):
        st = scores(part)
        for lg in range(n_lg):
            tops[part][lg] = _top16(st[:, lg * LANES:(lg + 1) * LANES], _row_iota(N_KEYS), float(N_KEYS))
            w_pieces.extend(weights(j) for j in range(len(w_pieces), min(len(w_pieces) + 2, n_pieces)))
    w_pieces.extend(weights(j) for j in range(len(w_pieces), n_pieces))
    acc = jnp.dot(jnp.concatenate(w_pieces, axis=1), v_ref[...], preferred_element_type=F32)
    bits = pltpu.bitcast(acc[0:SUBLANES, 0:LANES], jnp.uint32)
    zero = jnp.max(lax.shift_right_logical(lax.shift_right_logical(bits, jnp.uint32(16)), jnp.uint32(16)).astype(F32),
                   axis=0, keepdims=True)
    sel = [_select_pairs((tops[0][lg][0] + zero, tops[0][lg][1]), tops[1][lg]) for lg in range(n_lg)]
    rows = pl.ds(pl.multiple_of(hd * PEER_TOPK, PEER_TOPK), PEER_TOPK)
    for k in range(3):
        rs_ref[w_slot, half, k, rows, :] = jnp.concatenate([s_[k] for s_ in sel], axis=1)
    out_ref[...] += acc

    @pl.when(c == pl.num_programs(1) - 1)
    def _routing_to_token_major():
        for hf in range(2):
            for k in range(3):
                tok_ref[w_slot, k, hf * half_tok:(hf + 1) * half_tok, :] = rs_ref[w_slot, hf, k].T


def _peer_call(x1, h2, pq, sub_keys, expert_u, expert_v, tb, ec):
    t = x1.shape[0]
    nb = t // tb
    nc = N_EXPERTS // ec
    assert nc == 2 * PEER_HEADS and tb % (2 * LANES) == 0 and ec % EXPERT_PIECE == 0
    keys = sub_keys.astype(F32).reshape(PEER_HEADS * 2, N_KEYS, PEER_HALF)
    kh, kl = _split_bf16(keys)
    prev_row = lambda c_: pl.BlockSpec((tb, c_), lambda i, c: (jnp.maximum(i - 1, 0), 0))
    chunk = pl.BlockSpec((ec, D_MODEL), lambda i, c: (c, 0))
    q_spec = pl.BlockSpec((tb // 2, 2 * PEER_HALF), lambda i, c: (jnp.minimum(i, nb - 1) * 2 + c % 2, c // 2))
    full3 = pl.BlockSpec((PEER_HEADS * 2, N_KEYS, PEER_HALF), lambda i, c: (0, 0, 0))
    chunk_t = pl.BlockSpec((D_MODEL, ec), lambda i, c: (0, c))
    in_specs = [prev_row(D_MODEL), prev_row(D_MODEL), q_spec, full3, full3, chunk_t, chunk]
    operands = [x1, h2, pq, kh, kl, expert_u.astype(BF16).T, expert_v.astype(BF16)]
    out_spec = prev_row(D_MODEL)
    out_shape = jax.ShapeDtypeStruct((t, D_MODEL), F32)
    scratch_shapes = [pltpu.VMEM((tb // TOKENS_PER_GATE_GROUP, N_KEYS * SUBLANES, N_KEYS), jnp.uint32),
                      pltpu.VMEM((2, 2, 3, N_SEL, tb // 2), F32),
                      pltpu.VMEM((2, 3, tb, N_SEL), F32)]
    return pl.pallas_call(
        functools.partial(_peer_kernel, tb=tb, ec=ec),
        grid=(nb + 1, nc),
        in_specs=in_specs,
        out_specs=out_spec,
        out_shape=out_shape,
        scratch_shapes=scratch_shapes,
        compiler_params=pltpu.CompilerParams(
            dimension_semantics=("arbitrary", "arbitrary"),
            vmem_limit_bytes=_vmem_limit(in_specs, operands, [out_spec], [out_shape], scratch_shapes)),
        name="peer_block_experts",
    )(*operands)


def _tile_sizes(bsz, s):
    t = bsz * s
    tm = math.gcd(t, 512)
    ts = math.gcd(s, 512)
    tb = math.gcd(t, 512)
    return tm, ts, tb


def kernel(x, norm_mix, norm_ffn, w_in, b_gate, w_conv, q_norm, k_norm, sinks, rel_bias, w_conv_out, w_attn_out,
           w_out, w_query, sub_keys, expert_u, expert_v):
    bsz, s, d = x.shape
    assert d == D_MODEL and s % BLOCK == 0
    tm, ts, tb = _tile_sizes(bsz, s)
    biasm = _band_bias(rel_bias)
    x2 = x.reshape(bsz * s, d)
    for l in range(norm_mix.shape[0]):
        hc, bc, qn, kv, gc, ga = _proj_call(x2, norm_mix[l], w_in[l], b_gate[l], q_norm[l], k_norm[l], tm)
        x1, h2, pq = _mixer_call(x2, hc, bc, qn, kv, gc, ga, w_conv[l], biasm, sinks[l], w_conv_out[l],
                                 w_attn_out[l], w_out[l], norm_ffn[l], w_query[l], bsz, s, ts)
        x2 = _peer_call(x1, h2, pq, sub_keys[l], expert_u[l], expert_v[l], tb, EXPERT_CHUNK)
    return x2.reshape(bsz, s, d)
```

```python
import functools
import math

import jax
import jax.numpy as jnp
import numpy as np
from jax import lax
from jax.experimental import pallas as pl
from jax.experimental.pallas import tpu as pltpu

F32 = jnp.float32
BF16 = jnp.bfloat16

D_MODEL = 1024
CONV_DIM = 512
CONV_K = 3
N_HEADS = 8
N_KV_HEADS = 2
HEAD_DIM = 64
Q_DIM = N_HEADS * HEAD_DIM
KV_DIM = N_KV_HEADS * HEAD_DIM
GROUP = N_HEADS // N_KV_HEADS
WINDOW = 128
BLOCK = 128
N_BUCKETS = 32
MAX_DISTANCE = 128
PEER_HEADS = 8
N_KEYS = 128
N_EXPERTS = N_KEYS * N_KEYS
PEER_QDIM = 256
PEER_HALF = PEER_QDIM // 2
PEER_TOPK = 16
N_SEL = PEER_HEADS * PEER_TOPK
EPS = 1e-6
NEG_BIG = -1e30
PAIR_W = 2 * HEAD_DIM
KV_WIDE = N_KV_HEADS * PAIR_W

SUBLANES = 8
LANES = 128

EXPERT_CHUNK = N_EXPERTS // (2 * PEER_HEADS)
EXPERT_PIECE = 2 * N_KEYS
TOKENS_PER_GATE_GROUP = 2 * SUBLANES
GATE_GROUPS_PER_ITER = 8
V7X_VMEM_BYTES = 64 * 1024 * 1024
COMPILER_TEMP_VMEM = 12 * 1024 * 1024

_OFFS = np.cumsum([0, CONV_DIM, CONV_DIM, CONV_DIM, Q_DIM, KV_DIM, KV_DIM, D_MODEL, D_MODEL]).tolist()

_NT = (((1,), (1,)), ((), ()))


def _vmem_limit(in_specs, operands, out_specs, out_shapes, scratch_shapes=()):
    nbytes = lambda shape, dtype: math.prod(shape) * jnp.dtype(dtype).itemsize
    windows = sum(nbytes(s.block_shape, a.dtype)
                  for s, a in zip(list(in_specs) + list(out_specs), list(operands) + list(out_shapes))
                  if s.block_shape is not None)
    need = 2 * windows + sum(nbytes(s.shape, s.dtype) for s in scratch_shapes) + COMPILER_TEMP_VMEM
    assert need <= V7X_VMEM_BYTES, need
    return need


def _split_bf16(a):
    hi = a.astype(BF16)
    lo = (a - hi.astype(F32)).astype(BF16)
    return hi, lo


def _proj_kernel(x_ref, g_ref, w_ref, bg_ref, qg_ref, kg_ref, avq_ref, avk_ref, exp_ref,
                 hc_ref, bc_ref, qn_ref, kv_ref, gc_ref, ga_ref):
    x = x_ref[...]
    ms = jnp.mean(x * x, axis=-1, keepdims=True)
    h = ((x * lax.rsqrt(ms + EPS)) * g_ref[...]).astype(BF16)

    def seg(i):
        return jnp.dot(h, w_ref[:, _OFFS[i]:_OFFS[i + 1]], preferred_element_type=F32)

    u = seg(0)
    hc_ref[...] = seg(2) * u
    bc_ref[...] = seg(1)

    def head_rms(a, av_ref):
        hi, lo = _split_bf16(a * a)
        return (jnp.dot(hi, av_ref[...], preferred_element_type=F32)
                + jnp.dot(lo, av_ref[...], preferred_element_type=F32))

    q = seg(3)
    qn = (q * lax.rsqrt(head_rms(q, avq_ref) + EPS)) * qg_ref[...] * (HEAD_DIM ** -0.5)
    qn_ref[...] = qn.astype(BF16)

    k = seg(4)
    kn = ((k * lax.rsqrt(head_rms(k, avk_ref) + EPS)) * kg_ref[...]).astype(BF16)
    v = seg(5).astype(BF16)
    kv_ref[...] = jnp.dot(jnp.concatenate([kn, v], axis=1), exp_ref[...], preferred_element_type=F32).astype(BF16)

    bg = bg_ref[...]
    gc_ref[...] = jax.nn.sigmoid(seg(6) + bg[:, :D_MODEL]).astype(BF16)
    ga_ref[...] = jax.nn.sigmoid(seg(7) + bg[:, D_MODEL:]).astype(BF16)


def _kv_layout():
    e = np.zeros((2 * KV_DIM, 3 * KV_WIDE), np.float32)
    for g in range(N_KV_HEADS):
        for d in range(HEAD_DIM):
            e[g * HEAD_DIM + d, g * PAIR_W + d] = 1.0
            e[g * HEAD_DIM + d, KV_WIDE + g * PAIR_W + HEAD_DIM + d] = 1.0
            e[KV_DIM + g * HEAD_DIM + d, 2 * KV_WIDE + g * PAIR_W + d] = 1.0
            e[KV_DIM + g * HEAD_DIM + d, 2 * KV_WIDE + g * PAIR_W + HEAD_DIM + d] = 1.0
    return jnp.asarray(e, BF16)


def _proj_call(x2, norm_mix, w_in, b_gate, q_norm, k_norm, tm):
    t = x2.shape[0]
    in_dim = w_in.shape[1]
    avq = jnp.kron(jnp.eye(N_HEADS, dtype=F32), jnp.full((HEAD_DIM, HEAD_DIM), 1.0 / HEAD_DIM, F32)).astype(BF16)
    avk = jnp.kron(jnp.eye(N_KV_HEADS, dtype=F32), jnp.full((HEAD_DIM, HEAD_DIM), 1.0 / HEAD_DIM, F32)).astype(BF16)
    qg = jnp.tile(q_norm.astype(F32), N_HEADS).reshape(1, Q_DIM)
    kg = jnp.tile(k_norm.astype(F32), N_KV_HEADS).reshape(1, KV_DIM)
    full = lambda shape: pl.BlockSpec(shape, lambda i: (0,) * len(shape))
    row = lambda c: pl.BlockSpec((tm, c), lambda i: (i, 0))
    in_specs = [row(D_MODEL), full((1, D_MODEL)), full((D_MODEL, in_dim)), full((1, 2 * D_MODEL)),
                full((1, Q_DIM)), full((1, KV_DIM)), full((Q_DIM, Q_DIM)), full((KV_DIM, KV_DIM)),
                full((2 * KV_DIM, 3 * KV_WIDE))]
    operands = [x2, norm_mix.reshape(1, D_MODEL).astype(F32), w_in.astype(BF16),
                b_gate.reshape(1, 2 * D_MODEL).astype(F32), qg, kg, avq, avk, _kv_layout()]
    out_specs = [row(CONV_DIM), row(CONV_DIM), row(Q_DIM), row(3 * KV_WIDE), row(D_MODEL), row(D_MODEL)]
    out_shape = [jax.ShapeDtypeStruct((t, CONV_DIM), F32), jax.ShapeDtypeStruct((t, CONV_DIM), F32),
                 jax.ShapeDtypeStruct((t, Q_DIM), BF16), jax.ShapeDtypeStruct((t, 3 * KV_WIDE), BF16),
                 jax.ShapeDtypeStruct((t, D_MODEL), BF16), jax.ShapeDtypeStruct((t, D_MODEL), BF16)]
    return pl.pallas_call(
        _proj_kernel,
        grid=(t // tm,),
        in_specs=in_specs,
        out_specs=out_specs,
        out_shape=out_shape,
        compiler_params=pltpu.CompilerParams(
            dimension_semantics=("arbitrary",),
            vmem_limit_bytes=_vmem_limit(in_specs, operands, out_specs, out_shape)),
        name="peer_block_proj",
    )(*operands)


def _mixer_kernel(x_ref, hc_ref, hcp_ref, bc_ref, qn_ref, kv_ref, kvp_ref, gc_ref, ga_ref,
                  wc_ref, bias_ref, sink_ref, wco_ref, wao_ref, wo_ref, g2_ref, wq_ref,
                  x1_ref, h2_ref, pq_ref, att_ref, *, ts):
    j = pl.program_id(1)
    first = j == 0

    hc = hc_ref[...]
    prev = jnp.where(first, 0.0, hcp_ref[...])
    rows = lax.broadcasted_iota(jnp.int32, hc.shape, 0)
    s1 = pltpu.roll(hc, 1, axis=0)
    s1 = jnp.where(rows == 0, prev[SUBLANES - 1:SUBLANES, :], s1)
    s2 = pltpu.roll(hc, 2, axis=0)
    s2 = jnp.where(rows == 0, prev[SUBLANES - 2:SUBLANES - 1, :], s2)
    s2 = jnp.where(rows == 1, prev[SUBLANES - 1:SUBLANES, :], s2)
    wc = wc_ref[...]
    conv = s2 * wc[0:1, :] + s1 * wc[1:2, :] + hc * wc[2:3, :]
    yc = (bc_ref[...] * conv).astype(BF16)
    y_conv = jnp.dot(yc, wco_ref[...], preferred_element_type=F32)

    kvfull = jnp.concatenate([kvp_ref[...], kv_ref[...]], axis=0)
    col = lax.broadcasted_iota(jnp.int32, (BLOCK, 2 * BLOCK), 1)
    pen0 = jnp.where(jnp.logical_and(first, col < BLOCK), NEG_BIG, 0.0)
    even_lanes = lax.broadcasted_iota(jnp.int32, (BLOCK, PAIR_W), 1) < HEAD_DIM
    for r in range(ts // BLOCK):
        kvb = kvfull[r * BLOCK:(r + 2) * BLOCK, :]
        for pair in range(N_HEADS // 2):
            g = (2 * pair) // GROUP
            qs = qn_ref[r * BLOCK:(r + 1) * BLOCK, pair * PAIR_W:(pair + 1) * PAIR_W]
            vsel = kvb[:, 2 * KV_WIDE + g * PAIR_W:2 * KV_WIDE + (g + 1) * PAIR_W]
            outs = []
            for par in range(2):
                hd = 2 * pair + par
                ksel = kvb[:, par * KV_WIDE + g * PAIR_W:par * KV_WIDE + (g + 1) * PAIR_W]
                lg = lax.dot_general(qs, ksel, _NT, preferred_element_type=F32) + bias_ref[hd]
                if r == 0:
                    lg = lg + pen0
                sink = sink_ref[hd]
                m = jnp.maximum(jnp.max(lg, axis=-1, keepdims=True), sink)
                p = jnp.exp(lg - m)
                denom = jnp.sum(p, axis=-1, keepdims=True) + jnp.exp(sink - m)
                pv = jnp.dot(p.astype(BF16), vsel, preferred_element_type=F32)
                outs.append(pv / denom)
            att_ref[r * BLOCK:(r + 1) * BLOCK, pair * PAIR_W:(pair + 1) * PAIR_W] = (
                jnp.where(even_lanes, outs[0], outs[1]).astype(BF16))
    y_attn = jnp.dot(att_ref[...], wao_ref[...], preferred_element_type=F32)

    mixed = (gc_ref[...].astype(F32) * y_conv + ga_ref[...].astype(F32) * y_attn).astype(BF16)
    x1 = x_ref[...] + jnp.dot(mixed, wo_ref[...], preferred_element_type=F32)
    x1_ref[...] = x1

    ms = jnp.mean(x1 * x1, axis=-1, keepdims=True)
    h2 = ((x1 * lax.rsqrt(ms + EPS)) * g2_ref[...]).astype(BF16)
    h2_ref[...] = h2
    pq_ref[...] = jnp.dot(h2, wq_ref[...], preferred_element_type=F32)


def _mixer_call(x2, hc, bc, qn, kv, gc, ga, w_conv, biasm, sinks, w_conv_out, w_attn_out, w_out,
                norm_ffn, w_query, bsz, s, ts):
    t = bsz * s
    qd = w_query.shape[1]
    nj = s // ts
    row = lambda c: pl.BlockSpec((ts, c), lambda b, j: (b * nj + j, 0))
    prev_blk = lambda c: pl.BlockSpec((BLOCK, c), lambda b, j: (jnp.maximum((b * nj + j) * (ts // BLOCK) - 1, 0), 0))
    prev8 = pl.BlockSpec((SUBLANES, CONV_DIM),
                         lambda b, j: (jnp.maximum((b * nj + j) * (ts // SUBLANES) - 1, 0), 0))
    full = lambda shape: pl.BlockSpec(shape, lambda b, j: (0,) * len(shape))
    in_specs = [row(D_MODEL), row(CONV_DIM), prev8, row(CONV_DIM), row(Q_DIM),
                row(3 * KV_WIDE), prev_blk(3 * KV_WIDE), row(D_MODEL), row(D_MODEL),
                full((CONV_K, CONV_DIM)), full((N_HEADS, BLOCK, 2 * BLOCK)),
                pl.BlockSpec(memory_space=pltpu.SMEM),
                full((CONV_DIM, D_MODEL)), full((Q_DIM, D_MODEL)), full((D_MODEL, D_MODEL)),
                full((1, D_MODEL)), full((D_MODEL, qd))]
    operands = [x2, hc, hc, bc, qn, kv, kv, gc, ga, w_conv.astype(F32), biasm, sinks.astype(F32),
                w_conv_out.astype(BF16), w_attn_out.astype(BF16), w_out.astype(BF16),
                norm_ffn.reshape(1, D_MODEL).astype(F32), w_query.astype(BF16)]
    out_specs = [row(D_MODEL), row(D_MODEL), row(qd)]
    out_shape = [jax.ShapeDtypeStruct((t, D_MODEL), F32), jax.ShapeDtypeStruct((t, D_MODEL), BF16),
                 jax.ShapeDtypeStruct((t, qd), F32)]
    scratch_shapes = [pltpu.VMEM((ts, Q_DIM), BF16)]
    return pl.pallas_call(
        functools.partial(_mixer_kernel, ts=ts),
        grid=(bsz, nj),
        in_specs=in_specs,
        out_specs=out_specs,
        out_shape=out_shape,
        scratch_shapes=scratch_shapes,
        compiler_params=pltpu.CompilerParams(
            dimension_semantics=("arbitrary", "arbitrary"),
            vmem_limit_bytes=_vmem_limit(in_specs, operands, out_specs, out_shape, scratch_shapes)),
        name="peer_block_mixer",
    )(*operands)


def _band_bias(rel_bias):
    q_loc = jnp.arange(BLOCK, dtype=jnp.int32)[:, None]
    k_loc = jnp.arange(2 * BLOCK, dtype=jnp.int32)[None, :]
    dist = q_loc + BLOCK - k_loc
    max_exact = N_BUCKETS // 2
    d = jnp.maximum(dist, 0)
    df = jnp.maximum(d, 1).astype(F32)
    large = max_exact + (jnp.log(df / max_exact) / math.log(MAX_DISTANCE / max_exact)
                         * (N_BUCKETS - max_exact)).astype(jnp.int32)
    large = jnp.minimum(large, N_BUCKETS - 1)
    bucket = jnp.where(d < max_exact, d, large)
    onehot = (bucket[None] == jnp.arange(N_BUCKETS, dtype=jnp.int32)[:, None, None]).astype(F32)
    bias = jnp.einsum("bh,bqk->hqk", rel_bias.astype(F32), onehot, precision=lax.Precision.HIGHEST)
    valid = (dist >= 0) & (dist < WINDOW)
    return jnp.where(valid[None], bias, NEG_BIG)


def _sorting_network(n):
    def merge(lo, hi, r):
        step = r * 2
        if step < hi - lo:
            yield from merge(lo, hi, step)
            yield from merge(lo + r, hi, step)
            yield from ((i, i + r) for i in range(lo + r, hi - r, step))
        else:
            yield (lo, lo + r)

    def sort(lo, hi):
        if hi > lo:
            mid = lo + (hi - lo) // 2
            yield from sort(lo, mid)
            yield from sort(mid + 1, hi)
            yield from merge(lo, hi, 1)

    return list(sort(0, n - 1))


def _top16(vals, rid, big):
    n = vals.shape[0] // SUBLANES
    x = [vals[k * SUBLANES:(k + 1) * SUBLANES, :] for k in range(n)]
    ids = [rid[k * SUBLANES:(k + 1) * SUBLANES, :] for k in range(n)]
    for i, j in _sorting_network(n):
        a, b, ia, ib = x[i], x[j], ids[i], ids[j]
        first = (a > b) | ((a == b) & (ia < ib))
        x[i], x[j] = jnp.maximum(a, b), jnp.minimum(a, b)
        ids[i], ids[j] = jnp.where(first, ia, ib), jnp.where(first, ib, ia)
    tv, ti = [], []
    for r in range(PEER_TOPK):
        m = jnp.max(x[0], axis=0, keepdims=True)
        idx = jnp.min(jnp.where(x[0] == m, ids[0], big), axis=0, keepdims=True)
        tv.append(m)
        ti.append(idx)
        won = ids[0] == idx
        for k in range(min(n - 1, PEER_TOPK - 1 - r)):
            x[k] = jnp.where(won, x[k + 1], x[k])
            ids[k] = jnp.where(won, ids[k + 1], ids[k])
        if n - 1 < PEER_TOPK - 1 - r:
            x[n - 1] = jnp.where(won, -jnp.inf, x[n - 1])
    return jnp.concatenate(tv, axis=0), jnp.concatenate(ti, axis=0)


def _row_iota(n_rows):
    return lax.broadcasted_iota(jnp.int32, (n_rows, LANES), 0).astype(F32)


def _pair_candidates(v0, v1):
    assert PEER_TOPK == 2 * SUBLANES
    half = SUBLANES // 2
    r8 = _row_iota(SUBLANES)
    r16 = _row_iota(PEER_TOPK)
    vals = [v0[0:1, :] + v1]
    pos = [r16]
    for a in (1, 2, 3):
        vals.append(v0[a:a + 1, :] + v1[0:SUBLANES, :])
        pos.append(r8 + float(a * PEER_TOPK))
    low = r8 < float(half)
    v1dup = jnp.where(low, v1[0:SUBLANES, :], pltpu.roll(v1[0:SUBLANES, :], half, axis=0))
    bdup = jnp.where(low, r8, r8 - float(half))
    for a in (4, 6):
        vals.append(jnp.where(low, v0[a:a + 1, :], v0[a + 1:a + 2, :]) + v1dup)
        pos.append(jnp.where(low, float(a * PEER_TOPK), float((a + 1) * PEER_TOPK)) + bdup)
    vals.append(v0[SUBLANES:, :] + v1[0:1, :])
    pos.append((r8 + float(SUBLANES)) * float(PEER_TOPK))
    return jnp.concatenate(vals, axis=0), jnp.concatenate(pos, axis=0)


def _pick(table, sel):
    out = jnp.zeros_like(sel)
    for a in range(PEER_TOPK):
        out = jnp.where(sel == float(a), table[a:a + 1, :], out)
    return out


def _select_pairs(top0, top1):
    (v0, j0), (v1, j1) = top0, top1
    cand, cpos = _pair_candidates(v0, v1)
    ts_, pos = _top16(cand, cpos, float(PEER_TOPK * PEER_TOPK))
    pa = jnp.floor(pos * (1.0 / PEER_TOPK))
    pb = pos - pa * PEER_TOPK
    e = jnp.exp(ts_ - ts_[0:1, :])
    return _pick(j0, pa), _pick(j1, pb), e / jnp.sum(e, axis=0, keepdims=True)


def _gate_tiles(gd_ref, tg, i0b, i1b, gb, rid):
    zero = jnp.zeros_like(rid)
    one = jnp.ones_like(rid)
    for p in range(SUBLANES):
        tiles = []
        for t in (2 * p, 2 * p + 1):
            rt = jnp.where(rid == i0b[t:t + 1, :], gb[t:t + 1, :], zero)
            ct = jnp.where(rid == i1b[t:t + 1, :], one, zero)
            tiles.append(lax.dot_general(rt, ct, _NT, preferred_element_type=F32))
        lo_, hi_ = (pltpu.bitcast(t_.astype(BF16).astype(F32), jnp.uint32) for t_ in tiles)
        gd_ref[tg, pl.ds(p, N_KEYS, stride=SUBLANES), :] = (
            lax.shift_right_logical(lo_, jnp.uint32(16)) | (hi_ & jnp.uint32(0xFFFF0000)))


def _peer_kernel(x1_ref, h2_ref, q_ref, kh_ref, kl_ref, u_ref, v_ref, out_ref, gd_ref, rs_ref, tok_ref, *, tb, ec):
    ib = pl.program_id(0)
    c = pl.program_id(1)
    half_tok = tb // 2
    n_lg = half_tok // LANES
    n_pieces = ec // EXPERT_PIECE
    grp_per_piece = EXPERT_PIECE // N_KEYS
    w_slot = ib % 2
    r_slot = 1 - w_slot

    @pl.when(jnp.logical_and(ib == 0, c == 0))
    def _no_block_before_the_first():
        tok_ref[r_slot] = jnp.zeros(tok_ref.shape[1:], F32)

    @pl.when(c == 0)
    def _build_gate_matrix():
        rid = lax.broadcasted_iota(jnp.int32, (N_KEYS, N_SEL), 0).astype(F32).astype(BF16)

        def body(it, carry):
            for k in range(GATE_GROUPS_PER_ITER):
                tg = it * GATE_GROUPS_PER_ITER + k
                r0 = pl.multiple_of(tg * TOKENS_PER_GATE_GROUP, TOKENS_PER_GATE_GROUP)
                rows = pl.ds(r0, TOKENS_PER_GATE_GROUP)
                _gate_tiles(gd_ref, tg, tok_ref[r_slot, 0, rows, :].astype(BF16),
                            tok_ref[r_slot, 1, rows, :].astype(BF16),
                            (0.5 * tok_ref[r_slot, 2, rows, :]).astype(BF16), rid)
            return carry

        lax.fori_loop(0, tb // (TOKENS_PER_GATE_GROUP * GATE_GROUPS_PER_ITER), body, 0)
        out_ref[...] = x1_ref[...]

    hd = c // 2
    half = c % 2
    base = pl.multiple_of(c * (ec // N_KEYS * SUBLANES), SUBLANES)
    h2 = h2_ref[...]

    def scores(part):
        qh_, ql_ = _split_bf16(q_ref[:, part * PEER_HALF:(part + 1) * PEER_HALF])
        kh, kl = kh_ref[hd * 2 + part], kl_ref[hd * 2 + part]
        return (lax.dot_general(kh, qh_, _NT, preferred_element_type=F32)
                + lax.dot_general(kh, ql_, _NT, preferred_element_type=F32)
                + lax.dot_general(kl, qh_, _NT, preferred_element_type=F32))

    def weights(j):
        a = lax.dot_general(h2, u_ref[j * EXPERT_PIECE:(j + 1) * EXPERT_PIECE, :], _NT, preferred_element_type=F32)
        gd = jnp.concatenate(
            [pltpu.bitcast(gd_ref[:, pl.ds(base + (grp_per_piece * j + g) * SUBLANES, SUBLANES), :]
                           .reshape(tb // 2, N_KEYS), BF16) for g in range(grp_per_piece)], axis=1)
        return gd * (a * (1.0 + lax.erf(a * (2.0 ** -0.5)))).astype(BF16)

    def zero_after(words):
        u = pltpu.bitcast(words, jnp.uint32)
        z = lax.shift_right_logical(lax.shift_right_logical(u, jnp.uint32(16)), jnp.uint32(16))
        return jnp.max(z.astype(F32), axis=0, keepdims=True)

    tops = [[None] * n_lg, [None] * n_lg]
    w_pieces = []
    for part in range(2):
        st = scores(part)
        for lg in range(n_lg):
            blk = st[:, lg * LANES:(lg + 1) * LANES]
            if part == 1 and lg == n_lg - 1 and len(w_pieces) == n_pieces:
                blk = blk + zero_after(w_pieces[-1][0:2 * SUBLANES, 0:LANES])
            tops[part][lg] = _top16(blk, _row_iota(N_KEYS), float(N_KEYS))
            w_pieces.extend(weights(j) for j in range(len(w_pieces), min(len(w_pieces) + 2, n_pieces)))
    w_pieces.extend(weights(j) for j in range(len(w_pieces), n_pieces))
    acc = jnp.dot(jnp.concatenate(w_pieces, axis=1), v_ref[...], preferred_element_type=F32)
    zero = zero_after(acc[0:SUBLANES, 0:LANES])
    sel = [_select_pairs((tops[0][lg][0] + zero, tops[0][lg][1]), tops[1][lg]) for lg in range(n_lg)]
    rows = pl.ds(pl.multiple_of(hd * PEER_TOPK, PEER_TOPK), PEER_TOPK)
    for k in range(3):
        rs_ref[w_slot, half, k, rows, :] = jnp.concatenate([s_[k] for s_ in sel], axis=1)
    out_ref[...] += acc

    @pl.when(c == pl.num_programs(1) - 1)
    def _routing_to_token_major():
        for hf in range(2):
            for k in range(3):
                tok_ref[w_slot, k, hf * half_tok:(hf + 1) * half_tok, :] = rs_ref[w_slot, hf, k].T


def _peer_call(x1, h2, pq, sub_keys, expert_u, expert_v, tb, ec):
    t = x1.shape[0]
    nb = t // tb
    nc = N_EXPERTS // ec
    assert nc == 2 * PEER_HEADS and tb % (2 * LANES) == 0 and ec % EXPERT_PIECE == 0
    keys = sub_keys.astype(F32).reshape(PEER_HEADS * 2, N_KEYS, PEER_HALF)
    kh, kl = _split_bf16(keys)
    prev_row = lambda c_: pl.BlockSpec((tb, c_), lambda i, c: (jnp.maximum(i - 1, 0), 0))
    chunk = pl.BlockSpec((ec, D_MODEL), lambda i, c: (c, 0))
    q_spec = pl.BlockSpec((tb // 2, 2 * PEER_HALF), lambda i, c: (jnp.minimum(i, nb - 1) * 2 + c % 2, c // 2))
    full3 = pl.BlockSpec((PEER_HEADS * 2, N_KEYS, PEER_HALF), lambda i, c: (0, 0, 0))
    in_specs = [prev_row(D_MODEL), prev_row(D_MODEL), q_spec, full3, full3, chunk, chunk]
    operands = [x1, h2, pq, kh, kl, expert_u.astype(BF16), expert_v.astype(BF16)]
    out_spec = prev_row(D_MODEL)
    out_shape = jax.ShapeDtypeStruct((t, D_MODEL), F32)
    scratch_shapes = [pltpu.VMEM((tb // TOKENS_PER_GATE_GROUP, N_KEYS * SUBLANES, N_KEYS), jnp.uint32),
                      pltpu.VMEM((2, 2, 3, N_SEL, tb // 2), F32),
                      pltpu.VMEM((2, 3, tb, N_SEL), F32)]
    return pl.pallas_call(
        functools.partial(_peer_kernel, tb=tb, ec=ec),
        grid=(nb + 1, nc),
        in_specs=in_specs,
        out_specs=out_spec,
        out_shape=out_shape,
        scratch_shapes=scratch_shapes,
        compiler_params=pltpu.CompilerParams(
            dimension_semantics=("arbitrary", "arbitrary"),
            vmem_limit_bytes=_vmem_limit(in_specs, operands, [out_spec], [out_shape], scratch_shapes)),
        name="peer_block_experts",
    )(*operands)


def _tile_sizes(bsz, s):
    t = bsz * s
    tm = math.gcd(t, 512)
    ts = math.gcd(s, 512)
    tb = math.gcd(t, 512)
    return tm, ts, tb


def kernel(x, norm_mix, norm_ffn, w_in, b_gate, w_conv, q_norm, k_norm, sinks, rel_bias, w_conv_out, w_attn_out,
           w_out, w_query, sub_keys, expert_u, expert_v):
    bsz, s, d = x.shape
    assert d == D_MODEL and s % BLOCK == 0
    tm, ts, tb = _tile_sizes(bsz, s)
    biasm = _band_bias(rel_bias)
    x2 = x.reshape(bsz * s, d)
    for l in range(norm_mix.shape[0]):
        hc, bc, qn, kv, gc, ga = _proj_call(x2, norm_mix[l], w_in[l], b_gate[l], q_norm[l], k_norm[l], tm)
        x1, h2, pq = _mixer_call(x2, hc, bc, qn, kv, gc, ga, w_conv[l], biasm, sinks[l], w_conv_out[l],
                                 w_attn_out[l], w_out[l], norm_ffn[l], w_query[l], bsz, s, ts)
        x2 = _peer_call(x1, h2, pq, sub_keys[l], expert_u[l], expert_v[l], tb, EXPERT_CHUNK)
    return x2.reshape(bsz, s, d)
```

```python
import functools
import math

import jax
import jax.numpy as jnp
import numpy as np
from jax import lax
from jax.experimental import pallas as pl
from jax.experimental.pallas import tpu as pltpu

F32 = jnp.float32
BF16 = jnp.bfloat16

D_MODEL = 1024
CONV_DIM = 512
CONV_K = 3
N_HEADS = 8
N_KV_HEADS = 2
HEAD_DIM = 64
Q_DIM = N_HEADS * HEAD_DIM
KV_DIM = N_KV_HEADS * HEAD_DIM
GROUP = N_HEADS // N_KV_HEADS
WINDOW = 128
BLOCK = 128
N_BUCKETS = 32
MAX_DISTANCE = 128
PEER_HEADS = 8
N_KEYS = 128
N_EXPERTS = N_KEYS * N_KEYS
PEER_QDIM = 256
PEER_HALF = PEER_QDIM // 2
PEER_TOPK = 16
N_SEL = PEER_HEADS * PEER_TOPK
EPS = 1e-6
NEG_BIG = -1e30
PAIR_W = 2 * HEAD_DIM
KV_WIDE = N_KV_HEADS * PAIR_W

SUBLANES = 8
LANES = 128

EXPERT_CHUNK = N_EXPERTS // (2 * PEER_HEADS)
EXPERT_PIECE = 2 * N_KEYS
TOKENS_PER_GATE_GROUP = 2 * SUBLANES
GATE_GROUPS_PER_ITER = 8
MAX_CAST_SLAB = 2 * 1024 * 1024
V7X_VMEM_BYTES = 64 * 1024 * 1024
COMPILER_TEMP_VMEM = 12 * 1024 * 1024

_OFFS = np.cumsum([0, CONV_DIM, CONV_DIM, CONV_DIM, Q_DIM, KV_DIM, KV_DIM, D_MODEL, D_MODEL]).tolist()

_NT = (((1,), (1,)), ((), ()))


def _vmem_limit(in_specs, operands, out_specs, out_shapes, scratch_shapes=()):
    nbytes = lambda shape, dtype: math.prod(shape) * jnp.dtype(dtype).itemsize
    windows = sum(nbytes(s.block_shape, a.dtype)
                  for s, a in zip(list(in_specs) + list(out_specs), list(operands) + list(out_shapes))
                  if s.block_shape is not None)
    need = 2 * windows + sum(nbytes(s.shape, s.dtype) for s in scratch_shapes) + COMPILER_TEMP_VMEM
    assert need <= V7X_VMEM_BYTES, need
    return need


def _split_bf16(a):
    hi = a.astype(BF16)
    lo = (a - hi.astype(F32)).astype(BF16)
    return hi, lo


def _proj_kernel(x_ref, g_ref, w_ref, bg_ref, qg_ref, kg_ref, avq_ref, avk_ref, exp_ref, *rest):
    hc_ref, bc_ref, qn_ref, kv_ref, gc_ref, ga_ref = rest[-6:]
    n_tab = (len(rest) - 6) // 2
    for k in range(n_tab):
        rest[n_tab + k][...] = rest[k][...].astype(BF16)
    x = x_ref[...]
    ms = jnp.mean(x * x, axis=-1, keepdims=True)
    h = ((x * lax.rsqrt(ms + EPS)) * g_ref[...]).astype(BF16)

    def seg(i):
        return jnp.dot(h, w_ref[:, _OFFS[i]:_OFFS[i + 1]], preferred_element_type=F32)

    u = seg(0)
    hc_ref[...] = seg(2) * u
    bc_ref[...] = seg(1)

    def head_rms(a, av_ref):
        hi, lo = _split_bf16(a * a)
        return (jnp.dot(hi, av_ref[...], preferred_element_type=F32)
                + jnp.dot(lo, av_ref[...], preferred_element_type=F32))

    q = seg(3)
    qn = (q * lax.rsqrt(head_rms(q, avq_ref) + EPS)) * qg_ref[...] * (HEAD_DIM ** -0.5)
    qn_ref[...] = qn.astype(BF16)

    k = seg(4)
    kn = ((k * lax.rsqrt(head_rms(k, avk_ref) + EPS)) * kg_ref[...]).astype(BF16)
    v = seg(5).astype(BF16)
    kv_ref[...] = jnp.dot(jnp.concatenate([kn, v], axis=1), exp_ref[...], preferred_element_type=F32).astype(BF16)

    bg = bg_ref[...]
    gc_ref[...] = jax.nn.sigmoid(seg(6) + bg[:, :D_MODEL]).astype(BF16)
    ga_ref[...] = jax.nn.sigmoid(seg(7) + bg[:, D_MODEL:]).astype(BF16)


def _kv_layout():
    e = np.zeros((2 * KV_DIM, 3 * KV_WIDE), np.float32)
    for g in range(N_KV_HEADS):
        for d in range(HEAD_DIM):
            e[g * HEAD_DIM + d, g * PAIR_W + d] = 1.0
            e[g * HEAD_DIM + d, KV_WIDE + g * PAIR_W + HEAD_DIM + d] = 1.0
            e[KV_DIM + g * HEAD_DIM + d, 2 * KV_WIDE + g * PAIR_W + d] = 1.0
            e[KV_DIM + g * HEAD_DIM + d, 2 * KV_WIDE + g * PAIR_W + HEAD_DIM + d] = 1.0
    return jnp.asarray(e, BF16)


def _proj_call(x2, norm_mix, w_in, b_gate, q_norm, k_norm, tm, tables=()):
    t = x2.shape[0]
    in_dim = w_in.shape[1]
    avq = jnp.kron(jnp.eye(N_HEADS, dtype=F32), jnp.full((HEAD_DIM, HEAD_DIM), 1.0 / HEAD_DIM, F32)).astype(BF16)
    avk = jnp.kron(jnp.eye(N_KV_HEADS, dtype=F32), jnp.full((HEAD_DIM, HEAD_DIM), 1.0 / HEAD_DIM, F32)).astype(BF16)
    qg = jnp.tile(q_norm.astype(F32), N_HEADS).reshape(1, Q_DIM)
    kg = jnp.tile(k_norm.astype(F32), N_KV_HEADS).reshape(1, KV_DIM)
    full = lambda shape: pl.BlockSpec(shape, lambda i: (0,) * len(shape))
    row = lambda c: pl.BlockSpec((tm, c), lambda i: (i, 0))
    in_specs = [row(D_MODEL), full((1, D_MODEL)), full((D_MODEL, in_dim)), full((1, 2 * D_MODEL)),
                full((1, Q_DIM)), full((1, KV_DIM)), full((Q_DIM, Q_DIM)), full((KV_DIM, KV_DIM)),
                full((2 * KV_DIM, 3 * KV_WIDE))]
    operands = [x2, norm_mix.reshape(1, D_MODEL).astype(F32), w_in.astype(BF16),
                b_gate.reshape(1, 2 * D_MODEL).astype(F32), qg, kg, avq, avk, _kv_layout()]
    out_specs = [row(CONV_DIM), row(CONV_DIM), row(Q_DIM), row(3 * KV_WIDE), row(D_MODEL), row(D_MODEL)]
    out_shape = [jax.ShapeDtypeStruct((t, CONV_DIM), F32), jax.ShapeDtypeStruct((t, CONV_DIM), F32),
                 jax.ShapeDtypeStruct((t, Q_DIM), BF16), jax.ShapeDtypeStruct((t, 3 * KV_WIDE), BF16),
                 jax.ShapeDtypeStruct((t, D_MODEL), BF16), jax.ShapeDtypeStruct((t, D_MODEL), BF16)]
    for tab in tables:
        slab = pl.BlockSpec((tab.shape[0] // (t // tm), tab.shape[1]), lambda i: (i, 0))
        in_specs.append(slab)
        operands.append(tab)
    out_specs = [pl.BlockSpec(s.block_shape, s.index_map) for s in in_specs[len(in_specs) - len(tables):]] + out_specs
    out_shape = [jax.ShapeDtypeStruct(tab.shape, BF16) for tab in tables] + out_shape
    return pl.pallas_call(
        _proj_kernel,
        grid=(t // tm,),
        in_specs=in_specs,
        out_specs=out_specs,
        out_shape=out_shape,
        compiler_params=pltpu.CompilerParams(
            dimension_semantics=("arbitrary",),
            vmem_limit_bytes=_vmem_limit(in_specs, operands, out_specs, out_shape)),
        name="peer_block_proj",
    )(*operands)


def _mixer_kernel(x_ref, hc_ref, hcp_ref, bc_ref, qn_ref, kv_ref, kvp_ref, gc_ref, ga_ref,
                  wc_ref, bias_ref, sink_ref, wco_ref, wao_ref, wo_ref, g2_ref, wq_ref,
                  x1_ref, h2_ref, pq_ref, att_ref, *, ts):
    j = pl.program_id(1)
    first = j == 0

    hc = hc_ref[...]
    prev = jnp.where(first, 0.0, hcp_ref[...])
    rows = lax.broadcasted_iota(jnp.int32, hc.shape, 0)
    s1 = pltpu.roll(hc, 1, axis=0)
    s1 = jnp.where(rows == 0, prev[SUBLANES - 1:SUBLANES, :], s1)
    s2 = pltpu.roll(hc, 2, axis=0)
    s2 = jnp.where(rows == 0, prev[SUBLANES - 2:SUBLANES - 1, :], s2)
    s2 = jnp.where(rows == 1, prev[SUBLANES - 1:SUBLANES, :], s2)
    wc = wc_ref[...]
    conv = s2 * wc[0:1, :] + s1 * wc[1:2, :] + hc * wc[2:3, :]
    yc = (bc_ref[...] * conv).astype(BF16)
    y_conv = jnp.dot(yc, wco_ref[...], preferred_element_type=F32)

    kvfull = jnp.concatenate([kvp_ref[...], kv_ref[...]], axis=0)
    col = lax.broadcasted_iota(jnp.int32, (BLOCK, 2 * BLOCK), 1)
    pen0 = jnp.where(jnp.logical_and(first, col < BLOCK), NEG_BIG, 0.0)
    even_lanes = lax.broadcasted_iota(jnp.int32, (BLOCK, PAIR_W), 1) < HEAD_DIM
    for r in range(ts // BLOCK):
        kvb = kvfull[r * BLOCK:(r + 2) * BLOCK, :]
        for pair in range(N_HEADS // 2):
            g = (2 * pair) // GROUP
            qs = qn_ref[r * BLOCK:(r + 1) * BLOCK, pair * PAIR_W:(pair + 1) * PAIR_W]
            vsel = kvb[:, 2 * KV_WIDE + g * PAIR_W:2 * KV_WIDE + (g + 1) * PAIR_W]
            outs = []
            for par in range(2):
                hd = 2 * pair + par
                ksel = kvb[:, par * KV_WIDE + g * PAIR_W:par * KV_WIDE + (g + 1) * PAIR_W]
                lg = lax.dot_general(qs, ksel, _NT, preferred_element_type=F32) + bias_ref[hd]
                if r == 0:
                    lg = lg + pen0
                sink = sink_ref[hd]
                m = jnp.maximum(jnp.max(lg, axis=-1, keepdims=True), sink)
                p = jnp.exp(lg - m)
                denom = jnp.sum(p, axis=-1, keepdims=True) + jnp.exp(sink - m)
                pv = jnp.dot(p.astype(BF16), vsel, preferred_element_type=F32)
                outs.append(pv / denom)
            att_ref[r * BLOCK:(r + 1) * BLOCK, pair * PAIR_W:(pair + 1) * PAIR_W] = (
                jnp.where(even_lanes, outs[0], outs[1]).astype(BF16))
    y_attn = jnp.dot(att_ref[...], wao_ref[...], preferred_element_type=F32)

    mixed = (gc_ref[...].astype(F32) * y_conv + ga_ref[...].astype(F32) * y_attn).astype(BF16)
    x1 = x_ref[...] + jnp.dot(mixed, wo_ref[...], preferred_element_type=F32)
    x1_ref[...] = x1

    ms = jnp.mean(x1 * x1, axis=-1, keepdims=True)
    h2 = ((x1 * lax.rsqrt(ms + EPS)) * g2_ref[...]).astype(BF16)
    h2_ref[...] = h2
    pq_ref[...] = jnp.dot(h2, wq_ref[...], preferred_element_type=F32)


def _mixer_call(x2, hc, bc, qn, kv, gc, ga, w_conv, biasm, sinks, w_conv_out, w_attn_out, w_out,
                norm_ffn, w_query, bsz, s, ts):
    t = bsz * s
    qd = w_query.shape[1]
    nj = s // ts
    row = lambda c: pl.BlockSpec((ts, c), lambda b, j: (b * nj + j, 0))
    prev_blk = lambda c: pl.BlockSpec((BLOCK, c), lambda b, j: (jnp.maximum((b * nj + j) * (ts // BLOCK) - 1, 0), 0))
    prev8 = pl.BlockSpec((SUBLANES, CONV_DIM),
                         lambda b, j: (jnp.maximum((b * nj + j) * (ts // SUBLANES) - 1, 0), 0))
    full = lambda shape: pl.BlockSpec(shape, lambda b, j: (0,) * len(shape))
    in_specs = [row(D_MODEL), row(CONV_DIM), prev8, row(CONV_DIM), row(Q_DIM),
                row(3 * KV_WIDE), prev_blk(3 * KV_WIDE), row(D_MODEL), row(D_MODEL),
                full((CONV_K, CONV_DIM)), full((N_HEADS, BLOCK, 2 * BLOCK)),
                pl.BlockSpec(memory_space=pltpu.SMEM),
                full((CONV_DIM, D_MODEL)), full((Q_DIM, D_MODEL)), full((D_MODEL, D_MODEL)),
                full((1, D_MODEL)), full((D_MODEL, qd))]
    operands = [x2, hc, hc, bc, qn, kv, kv, gc, ga, w_conv.astype(F32), biasm, sinks.astype(F32),
                w_conv_out.astype(BF16), w_attn_out.astype(BF16), w_out.astype(BF16),
                norm_ffn.reshape(1, D_MODEL).astype(F32), w_query.astype(BF16)]
    out_specs = [row(D_MODEL), row(D_MODEL), row(qd)]
    out_shape = [jax.ShapeDtypeStruct((t, D_MODEL), F32), jax.ShapeDtypeStruct((t, D_MODEL), BF16),
                 jax.ShapeDtypeStruct((t, qd), F32)]
    scratch_shapes = [pltpu.VMEM((ts, Q_DIM), BF16)]
    return pl.pallas_call(
        functools.partial(_mixer_kernel, ts=ts),
        grid=(bsz, nj),
        in_specs=in_specs,
        out_specs=out_specs,
        out_shape=out_shape,
        scratch_shapes=scratch_shapes,
        compiler_params=pltpu.CompilerParams(
            dimension_semantics=("arbitrary", "arbitrary"),
            vmem_limit_bytes=_vmem_limit(in_specs, operands, out_specs, out_shape, scratch_shapes)),
        name="peer_block_mixer",
    )(*operands)


def _band_bias(rel_bias):
    q_loc = jnp.arange(BLOCK, dtype=jnp.int32)[:, None]
    k_loc = jnp.arange(2 * BLOCK, dtype=jnp.int32)[None, :]
    dist = q_loc + BLOCK - k_loc
    max_exact = N_BUCKETS // 2
    d = jnp.maximum(dist, 0)
    df = jnp.maximum(d, 1).astype(F32)
    large = max_exact + (jnp.log(df / max_exact) / math.log(MAX_DISTANCE / max_exact)
                         * (N_BUCKETS - max_exact)).astype(jnp.int32)
    large = jnp.minimum(large, N_BUCKETS - 1)
    bucket = jnp.where(d < max_exact, d, large)
    onehot = (bucket[None] == jnp.arange(N_BUCKETS, dtype=jnp.int32)[:, None, None]).astype(F32)
    bias = jnp.einsum("bh,bqk->hqk", rel_bias.astype(F32), onehot, precision=lax.Precision.HIGHEST)
    valid = (dist >= 0) & (dist < WINDOW)
    return jnp.where(valid[None], bias, NEG_BIG)


def _sorting_network(n):
    def merge(lo, hi, r):
        step = r * 2
        if step < hi - lo:
            yield from merge(lo, hi, step)
            yield from merge(lo + r, hi, step)
            yield from ((i, i + r) for i in range(lo + r, hi - r, step))
        else:
            yield (lo, lo + r)

    def sort(lo, hi):
        if hi > lo:
            mid = lo + (hi - lo) // 2
            yield from sort(lo, mid)
            yield from sort(mid + 1, hi)
            yield from merge(lo, hi, 1)

    return list(sort(0, n - 1))


def _top16(vals, rid, big):
    n = vals.shape[0] // SUBLANES
    x = [vals[k * SUBLANES:(k + 1) * SUBLANES, :] for k in range(n)]
    ids = [rid[k * SUBLANES:(k + 1) * SUBLANES, :] for k in range(n)]
    for i, j in _sorting_network(n):
        a, b, ia, ib = x[i], x[j], ids[i], ids[j]
        first = (a > b) | ((a == b) & (ia < ib))
        x[i], x[j] = jnp.maximum(a, b), jnp.minimum(a, b)
        ids[i], ids[j] = jnp.where(first, ia, ib), jnp.where(first, ib, ia)
    tv, ti = [], []
    for r in range(PEER_TOPK):
        m = jnp.max(x[0], axis=0, keepdims=True)
        idx = jnp.min(jnp.where(x[0] == m, ids[0], big), axis=0, keepdims=True)
        tv.append(m)
        ti.append(idx)
        won = ids[0] == idx
        for k in range(min(n - 1, PEER_TOPK - 1 - r)):
            x[k] = jnp.where(won, x[k + 1], x[k])
            ids[k] = jnp.where(won, ids[k + 1], ids[k])
        if n - 1 < PEER_TOPK - 1 - r:
            x[n - 1] = jnp.where(won, -jnp.inf, x[n - 1])
    return jnp.concatenate(tv, axis=0), jnp.concatenate(ti, axis=0)


def _row_iota(n_rows):
    return lax.broadcasted_iota(jnp.int32, (n_rows, LANES), 0).astype(F32)


def _pair_candidates(v0, v1):
    assert PEER_TOPK == 2 * SUBLANES
    half = SUBLANES // 2
    r8 = _row_iota(SUBLANES)
    r16 = _row_iota(PEER_TOPK)
    vals = [v0[0:1, :] + v1]
    pos = [r16]
    for a in (1, 2, 3):
        vals.append(v0[a:a + 1, :] + v1[0:SUBLANES, :])
        pos.append(r8 + float(a * PEER_TOPK))
    low = r8 < float(half)
    v1dup = jnp.where(low, v1[0:SUBLANES, :], pltpu.roll(v1[0:SUBLANES, :], half, axis=0))
    bdup = jnp.where(low, r8, r8 - float(half))
    for a in (4, 6):
        vals.append(jnp.where(low, v0[a:a + 1, :], v0[a + 1:a + 2, :]) + v1dup)
        pos.append(jnp.where(low, float(a * PEER_TOPK), float((a + 1) * PEER_TOPK)) + bdup)
    vals.append(v0[SUBLANES:, :] + v1[0:1, :])
    pos.append((r8 + float(SUBLANES)) * float(PEER_TOPK))
    return jnp.concatenate(vals, axis=0), jnp.concatenate(pos, axis=0)


def _pick(table, sel):
    out = jnp.zeros_like(sel)
    for a in range(PEER_TOPK):
        out = jnp.where(sel == float(a), table[a:a + 1, :], out)
    return out


def _select_pairs(top0, top1):
    (v0, j0), (v1, j1) = top0, top1
    cand, cpos = _pair_candidates(v0, v1)
    ts_, pos = _top16(cand, cpos, float(PEER_TOPK * PEER_TOPK))
    pa = jnp.floor(pos * (1.0 / PEER_TOPK))
    pb = pos - pa * PEER_TOPK
    e = jnp.exp(ts_ - ts_[0:1, :])
    return _pick(j0, pa), _pick(j1, pb), e / jnp.sum(e, axis=0, keepdims=True)


def _gate_tiles(gd_ref, tg, i0b, i1b, gb, rid):
    zero = jnp.zeros_like(rid)
    one = jnp.ones_like(rid)
    for p in range(SUBLANES):
        tiles = []
        for t in (2 * p, 2 * p + 1):
            rt = jnp.where(rid == i0b[t:t + 1, :], gb[t:t + 1, :], zero)
            ct = jnp.where(rid == i1b[t:t + 1, :], one, zero)
            tiles.append(lax.dot_general(rt, ct, _NT, preferred_element_type=F32))
        lo_, hi_ = (pltpu.bitcast(t_.astype(BF16).astype(F32), jnp.uint32) for t_ in tiles)
        gd_ref[tg, pl.ds(p, N_KEYS, stride=SUBLANES), :] = (
            lax.shift_right_logical(lo_, jnp.uint32(16)) | (hi_ & jnp.uint32(0xFFFF0000)))


def _peer_kernel(x1_ref, h2_ref, q_ref, kh_ref, kl_ref, u_ref, v_ref, out_ref, gd_ref, rs_ref, tok_ref, *, tb, ec):
    ib = pl.program_id(0)
    c = pl.program_id(1)
    half_tok = tb // 2
    n_lg = half_tok // LANES
    n_pieces = ec // EXPERT_PIECE
    grp_per_piece = EXPERT_PIECE // N_KEYS
    w_slot = ib % 2
    r_slot = 1 - w_slot

    @pl.when(jnp.logical_and(ib == 0, c == 0))
    def _no_block_before_the_first():
        tok_ref[r_slot] = jnp.zeros(tok_ref.shape[1:], F32)

    @pl.when(c == 0)
    def _build_gate_matrix():
        rid = lax.broadcasted_iota(jnp.int32, (N_KEYS, N_SEL), 0).astype(F32).astype(BF16)

        def body(it, carry):
            for k in range(GATE_GROUPS_PER_ITER):
                tg = it * GATE_GROUPS_PER_ITER + k
                r0 = pl.multiple_of(tg * TOKENS_PER_GATE_GROUP, TOKENS_PER_GATE_GROUP)
                rows = pl.ds(r0, TOKENS_PER_GATE_GROUP)
                _gate_tiles(gd_ref, tg, tok_ref[r_slot, 0, rows, :].astype(BF16),
                            tok_ref[r_slot, 1, rows, :].astype(BF16),
                            (0.5 * tok_ref[r_slot, 2, rows, :]).astype(BF16), rid)
            return carry

        lax.fori_loop(0, tb // (TOKENS_PER_GATE_GROUP * GATE_GROUPS_PER_ITER), body, 0)
        out_ref[...] = x1_ref[...]

    hd = c // 2
    half = c % 2
    base = pl.multiple_of(c * (ec // N_KEYS * SUBLANES), SUBLANES)
    h2 = h2_ref[...]

    def scores(part):
        qh_, ql_ = _split_bf16(q_ref[:, part * PEER_HALF:(part + 1) * PEER_HALF])
        kh, kl = kh_ref[hd * 2 + part], kl_ref[hd * 2 + part]
        return (lax.dot_general(kh, qh_, _NT, preferred_element_type=F32)
                + lax.dot_general(kh, ql_, _NT, preferred_element_type=F32)
                + lax.dot_general(kl, qh_, _NT, preferred_element_type=F32))

    def weights(j):
        a = lax.dot_general(h2, u_ref[j * EXPERT_PIECE:(j + 1) * EXPERT_PIECE, :], _NT, preferred_element_type=F32)
        gd = jnp.concatenate(
            [pltpu.bitcast(gd_ref[:, pl.ds(base + (grp_per_piece * j + g) * SUBLANES, SUBLANES), :]
                           .reshape(tb // 2, N_KEYS), BF16) for g in range(grp_per_piece)], axis=1)
        return gd * (a * (1.0 + lax.erf(a * (2.0 ** -0.5)))).astype(BF16)

    def zero_after(words):
        u = pltpu.bitcast(words, jnp.uint32)
        z = lax.shift_right_logical(lax.shift_right_logical(u, jnp.uint32(16)), jnp.uint32(16))
        return jnp.max(z.astype(F32), axis=0, keepdims=True)

    tops = [[None] * n_lg, [None] * n_lg]
    w_pieces = []
    for part in range(2):
        st = scores(part)
        for lg in range(n_lg):
            blk = st[:, lg * LANES:(lg + 1) * LANES]
            if part == 1 and lg == n_lg - 1 and len(w_pieces) == n_pieces:
                blk = blk + zero_after(w_pieces[-1][0:2 * SUBLANES, 0:LANES])
            tops[part][lg] = _top16(blk, _row_iota(N_KEYS), float(N_KEYS))
            w_pieces.extend(weights(j) for j in range(len(w_pieces), min(len(w_pieces) + 2, n_pieces)))
    w_pieces.extend(weights(j) for j in range(len(w_pieces), n_pieces))
    acc = jnp.dot(jnp.concatenate(w_pieces, axis=1), v_ref[...], preferred_element_type=F32)
    zero = zero_after(acc[0:SUBLANES, 0:LANES])
    sel = [_select_pairs((tops[0][lg][0] + zero, tops[0][lg][1]), tops[1][lg]) for lg in range(n_lg)]
    rows = pl.ds(pl.multiple_of(hd * PEER_TOPK, PEER_TOPK), PEER_TOPK)
    for k in range(3):
        rs_ref[w_slot, half, k, rows, :] = jnp.concatenate([s_[k] for s_ in sel], axis=1)
    out_ref[...] += acc

    @pl.when(c == pl.num_programs(1) - 1)
    def _routing_to_token_major():
        for hf in range(2):
            for k in range(3):
                tok_ref[w_slot, k, hf * half_tok:(hf + 1) * half_tok, :] = rs_ref[w_slot, hf, k].T


def _peer_call(x1, h2, pq, sub_keys, expert_u, expert_v, tb, ec):
    t = x1.shape[0]
    nb = t // tb
    nc = N_EXPERTS // ec
    assert nc == 2 * PEER_HEADS and tb % (2 * LANES) == 0 and ec % EXPERT_PIECE == 0
    keys = sub_keys.astype(F32).reshape(PEER_HEADS * 2, N_KEYS, PEER_HALF)
    kh, kl = _split_bf16(keys)
    prev_row = lambda c_: pl.BlockSpec((tb, c_), lambda i, c: (jnp.maximum(i - 1, 0), 0))
    chunk = pl.BlockSpec((ec, D_MODEL), lambda i, c: (c, 0))
    q_spec = pl.BlockSpec((tb // 2, 2 * PEER_HALF), lambda i, c: (jnp.minimum(i, nb - 1) * 2 + c % 2, c // 2))
    full3 = pl.BlockSpec((PEER_HEADS * 2, N_KEYS, PEER_HALF), lambda i, c: (0, 0, 0))
    in_specs = [prev_row(D_MODEL), prev_row(D_MODEL), q_spec, full3, full3, chunk, chunk]
    operands = [x1, h2, pq, kh, kl, expert_u, expert_v]
    out_spec = prev_row(D_MODEL)
    out_shape = jax.ShapeDtypeStruct((t, D_MODEL), F32)
    scratch_shapes = [pltpu.VMEM((tb // TOKENS_PER_GATE_GROUP, N_KEYS * SUBLANES, N_KEYS), jnp.uint32),
                      pltpu.VMEM((2, 2, 3, N_SEL, tb // 2), F32),
                      pltpu.VMEM((2, 3, tb, N_SEL), F32)]
    return pl.pallas_call(
        functools.partial(_peer_kernel, tb=tb, ec=ec),
        grid=(nb + 1, nc),
        in_specs=in_specs,
        out_specs=out_spec,
        out_shape=out_shape,
        scratch_shapes=scratch_shapes,
        compiler_params=pltpu.CompilerParams(
            dimension_semantics=("arbitrary", "arbitrary"),
            vmem_limit_bytes=_vmem_limit(in_specs, operands, [out_spec], [out_shape], scratch_shapes)),
        name="peer_block_experts",
    )(*operands)


def _tile_sizes(bsz, s):
    t = bsz * s
    tm = math.gcd(t, 512)
    ts = math.gcd(s, 512)
    tb = math.gcd(t, 512)
    return tm, ts, tb


def kernel(x, norm_mix, norm_ffn, w_in, b_gate, w_conv, q_norm, k_norm, sinks, rel_bias, w_conv_out, w_attn_out,
           w_out, w_query, sub_keys, expert_u, expert_v):
    bsz, s, d = x.shape
    assert d == D_MODEL and s % BLOCK == 0
    tm, ts, tb = _tile_sizes(bsz, s)
    biasm = _band_bias(rel_bias)
    x2 = x.reshape(bsz * s, d)
    for l in range(norm_mix.shape[0]):
        steps = (bsz * s) // tm
        rows = N_EXPERTS // steps
        in_call = N_EXPERTS % steps == 0 and rows % TOKENS_PER_GATE_GROUP == 0 and rows * D_MODEL * 4 <= MAX_CAST_SLAB
        tables = (expert_u[l].astype(F32), expert_v[l].astype(F32)) if in_call else ()
        *cast, hc, bc, qn, kv, gc, ga = _proj_call(x2, norm_mix[l], w_in[l], b_gate[l], q_norm[l], k_norm[l], tm,
                                                   tables)
        u16, v16 = cast if in_call else (expert_u[l].astype(BF16), expert_v[l].astype(BF16))
        x1, h2, pq = _mixer_call(x2, hc, bc, qn, kv, gc, ga, w_conv[l], biasm, sinks[l], w_conv_out[l],
                                 w_attn_out[l], w_out[l], norm_ffn[l], w_query[l], bsz, s, ts)
        x2 = _peer_call(x1, h2, pq, sub_keys[l], u16, v16, tb, EXPERT_CHUNK)
    return x2.reshape(bsz, s, d)
```

```python
import functools
import math

import jax
import jax.numpy as jnp
import numpy as np
from jax import lax
from jax.experimental import pallas as pl
from jax.experimental.pallas import tpu as pltpu

F32 = jnp.float32
BF16 = jnp.bfloat16

D_MODEL = 1024
CONV_DIM = 512
CONV_K = 3
N_HEADS = 8
N_KV_HEADS = 2
HEAD_DIM = 64
Q_DIM = N_HEADS * HEAD_DIM
KV_DIM = N_KV_HEADS * HEAD_DIM
GROUP = N_HEADS // N_KV_HEADS
WINDOW = 128
BLOCK = 128
N_BUCKETS = 32
MAX_DISTANCE = 128
PEER_HEADS = 8
N_KEYS = 128
N_EXPERTS = N_KEYS * N_KEYS
PEER_QDIM = 256
PEER_HALF = PEER_QDIM // 2
PEER_TOPK = 16
N_SEL = PEER_HEADS * PEER_TOPK
EPS = 1e-6
NEG_BIG = -1e30
PAIR_W = 2 * HEAD_DIM
KV_WIDE = N_KV_HEADS * PAIR_W

SUBLANES = 8
LANES = 128

EXPERT_CHUNK = N_EXPERTS // (2 * PEER_HEADS)
EXPERT_PIECE = 2 * N_KEYS
TOKENS_PER_GATE_GROUP = 2 * SUBLANES
GATE_GROUPS_PER_ITER = 8
MAX_CAST_SLAB = 2 * 1024 * 1024
V7X_VMEM_BYTES = 64 * 1024 * 1024
COMPILER_TEMP_VMEM = 12 * 1024 * 1024

_OFFS = np.cumsum([0, CONV_DIM, CONV_DIM, CONV_DIM, Q_DIM, KV_DIM, KV_DIM, D_MODEL, D_MODEL]).tolist()

_NT = (((1,), (1,)), ((), ()))


def _vmem_limit(in_specs, operands, out_specs, out_shapes, scratch_shapes=()):
    nbytes = lambda shape, dtype: math.prod(shape) * jnp.dtype(dtype).itemsize
    windows = sum(nbytes(s.block_shape, a.dtype)
                  for s, a in zip(list(in_specs) + list(out_specs), list(operands) + list(out_shapes))
                  if s.block_shape is not None)
    need = 2 * windows + sum(nbytes(s.shape, s.dtype) for s in scratch_shapes) + COMPILER_TEMP_VMEM
    assert need <= V7X_VMEM_BYTES, need
    return need


def _split_bf16(a):
    hi = a.astype(BF16)
    lo = (a - hi.astype(F32)).astype(BF16)
    return hi, lo


def _proj_kernel(x_ref, g_ref, w_ref, bg_ref, qg_ref, kg_ref, avq_ref, avk_ref, exp_ref, *rest):
    hc_ref, bc_ref, qn_ref, kv_ref, gc_ref, ga_ref = rest[-6:]
    n_tab = (len(rest) - 6) // 2
    for k in range(n_tab):
        rest[n_tab + k][...] = rest[k][...].astype(BF16)
    x = x_ref[...]
    ms = jnp.mean(x * x, axis=-1, keepdims=True)
    h = ((x * lax.rsqrt(ms + EPS)) * g_ref[...]).astype(BF16)

    def seg(i):
        return jnp.dot(h, w_ref[:, _OFFS[i]:_OFFS[i + 1]], preferred_element_type=F32)

    u = seg(0)
    hc_ref[...] = seg(2) * u
    bc_ref[...] = seg(1)

    def head_rms(a, av_ref):
        hi, lo = _split_bf16(a * a)
        return (jnp.dot(hi, av_ref[...], preferred_element_type=F32)
                + jnp.dot(lo, av_ref[...], preferred_element_type=F32))

    q = seg(3)
    qn = (q * lax.rsqrt(head_rms(q, avq_ref) + EPS)) * qg_ref[...] * (HEAD_DIM ** -0.5)
    qn_ref[...] = qn.astype(BF16)

    k = seg(4)
    kn = ((k * lax.rsqrt(head_rms(k, avk_ref) + EPS)) * kg_ref[...]).astype(BF16)
    v = seg(5).astype(BF16)
    kv_ref[...] = jnp.dot(jnp.concatenate([kn, v], axis=1), exp_ref[...], preferred_element_type=F32).astype(BF16)

    bg = bg_ref[...]
    gc_ref[...] = jax.nn.sigmoid(seg(6) + bg[:, :D_MODEL]).astype(BF16)
    ga_ref[...] = jax.nn.sigmoid(seg(7) + bg[:, D_MODEL:]).astype(BF16)


def _kv_layout():
    e = np.zeros((2 * KV_DIM, 3 * KV_WIDE), np.float32)
    for g in range(N_KV_HEADS):
        for d in range(HEAD_DIM):
            e[g * HEAD_DIM + d, g * PAIR_W + d] = 1.0
            e[g * HEAD_DIM + d, KV_WIDE + g * PAIR_W + HEAD_DIM + d] = 1.0
            e[KV_DIM + g * HEAD_DIM + d, 2 * KV_WIDE + g * PAIR_W + d] = 1.0
            e[KV_DIM + g * HEAD_DIM + d, 2 * KV_WIDE + g * PAIR_W + HEAD_DIM + d] = 1.0
    return jnp.asarray(e, BF16)


def _proj_call(x2, norm_mix, w_in, b_gate, q_norm, k_norm, tm, tables=()):
    t = x2.shape[0]
    in_dim = w_in.shape[1]
    avq = jnp.kron(jnp.eye(N_HEADS, dtype=F32), jnp.full((HEAD_DIM, HEAD_DIM), 1.0 / HEAD_DIM, F32)).astype(BF16)
    avk = jnp.kron(jnp.eye(N_KV_HEADS, dtype=F32), jnp.full((HEAD_DIM, HEAD_DIM), 1.0 / HEAD_DIM, F32)).astype(BF16)
    qg = jnp.tile(q_norm.astype(F32), N_HEADS).reshape(1, Q_DIM)
    kg = jnp.tile(k_norm.astype(F32), N_KV_HEADS).reshape(1, KV_DIM)
    full = lambda shape: pl.BlockSpec(shape, lambda i: (0,) * len(shape))
    row = lambda c: pl.BlockSpec((tm, c), lambda i: (i, 0))
    in_specs = [row(D_MODEL), full((1, D_MODEL)), full((D_MODEL, in_dim)), full((1, 2 * D_MODEL)),
                full((1, Q_DIM)), full((1, KV_DIM)), full((Q_DIM, Q_DIM)), full((KV_DIM, KV_DIM)),
                full((2 * KV_DIM, 3 * KV_WIDE))]
    operands = [x2, norm_mix.reshape(1, D_MODEL).astype(F32), w_in.astype(BF16),
                b_gate.reshape(1, 2 * D_MODEL).astype(F32), qg, kg, avq, avk, _kv_layout()]
    out_specs = [row(CONV_DIM), row(CONV_DIM), row(Q_DIM), row(3 * KV_WIDE), row(D_MODEL), row(D_MODEL)]
    out_shape = [jax.ShapeDtypeStruct((t, CONV_DIM), F32), jax.ShapeDtypeStruct((t, CONV_DIM), F32),
                 jax.ShapeDtypeStruct((t, Q_DIM), BF16), jax.ShapeDtypeStruct((t, 3 * KV_WIDE), BF16),
                 jax.ShapeDtypeStruct((t, D_MODEL), BF16), jax.ShapeDtypeStruct((t, D_MODEL), BF16)]
    for tab in tables:
        slab = pl.BlockSpec((tab.shape[0] // (t // tm), tab.shape[1]), lambda i: (i, 0))
        in_specs.append(slab)
        operands.append(tab)
    out_specs = [pl.BlockSpec(s.block_shape, s.index_map) for s in in_specs[len(in_specs) - len(tables):]] + out_specs
    out_shape = [jax.ShapeDtypeStruct(tab.shape, BF16) for tab in tables] + out_shape
    return pl.pallas_call(
        _proj_kernel,
        grid=(t // tm,),
        in_specs=in_specs,
        out_specs=out_specs,
        out_shape=out_shape,
        compiler_params=pltpu.CompilerParams(
            dimension_semantics=("arbitrary",),
            vmem_limit_bytes=_vmem_limit(in_specs, operands, out_specs, out_shape)),
        name="peer_block_proj",
    )(*operands)


def _mixer_kernel(x_ref, hc_ref, hcp_ref, bc_ref, qn_ref, kv_ref, kvp_ref, gc_ref, ga_ref,
                  wc_ref, bias_ref, sink_ref, wco_ref, wao_ref, wo_ref, g2_ref, wq_ref,
                  x1_ref, h2_ref, pq_ref, att_ref, *, ts):
    j = pl.program_id(1)
    first = j == 0

    hc = hc_ref[...]
    prev = jnp.where(first, 0.0, hcp_ref[...])
    rows = lax.broadcasted_iota(jnp.int32, hc.shape, 0)
    s1 = pltpu.roll(hc, 1, axis=0)
    s1 = jnp.where(rows == 0, prev[SUBLANES - 1:SUBLANES, :], s1)
    s2 = pltpu.roll(hc, 2, axis=0)
    s2 = jnp.where(rows == 0, prev[SUBLANES - 2:SUBLANES - 1, :], s2)
    s2 = jnp.where(rows == 1, prev[SUBLANES - 1:SUBLANES, :], s2)
    wc = wc_ref[...]
    conv = s2 * wc[0:1, :] + s1 * wc[1:2, :] + hc * wc[2:3, :]
    yc = (bc_ref[...] * conv).astype(BF16)
    y_conv = jnp.dot(yc, wco_ref[...], preferred_element_type=F32)

    kvfull = jnp.concatenate([kvp_ref[...], kv_ref[...]], axis=0)
    col = lax.broadcasted_iota(jnp.int32, (BLOCK, 2 * BLOCK), 1)
    pen0 = jnp.where(jnp.logical_and(first, col < BLOCK), NEG_BIG, 0.0)
    even_lanes = lax.broadcasted_iota(jnp.int32, (BLOCK, PAIR_W), 1) < HEAD_DIM
    for r in range(ts // BLOCK):
        kvb = kvfull[r * BLOCK:(r + 2) * BLOCK, :]
        for pair in range(N_HEADS // 2):
            g = (2 * pair) // GROUP
            qs = qn_ref[r * BLOCK:(r + 1) * BLOCK, pair * PAIR_W:(pair + 1) * PAIR_W]
            vsel = kvb[:, 2 * KV_WIDE + g * PAIR_W:2 * KV_WIDE + (g + 1) * PAIR_W]
            outs = []
            for par in range(2):
                hd = 2 * pair + par
                ksel = kvb[:, par * KV_WIDE + g * PAIR_W:par * KV_WIDE + (g + 1) * PAIR_W]
                lg = lax.dot_general(qs, ksel, _NT, preferred_element_type=F32) + bias_ref[hd]
                if r == 0:
                    lg = lg + pen0
                sink = sink_ref[hd]
                m = jnp.maximum(jnp.max(lg, axis=-1, keepdims=True), sink)
                p = jnp.exp(lg - m)
                denom = jnp.sum(p, axis=-1, keepdims=True) + jnp.exp(sink - m)
                pv = jnp.dot(p.astype(BF16), vsel, preferred_element_type=F32)
                outs.append(pv / denom)
            att_ref[r * BLOCK:(r + 1) * BLOCK, pair * PAIR_W:(pair + 1) * PAIR_W] = (
                jnp.where(even_lanes, outs[0], outs[1]).astype(BF16))
    y_attn = jnp.dot(att_ref[...], wao_ref[...], preferred_element_type=F32)

    mixed = (gc_ref[...].astype(F32) * y_conv + ga_ref[...].astype(F32) * y_attn).astype(BF16)
    x1 = x_ref[...] + jnp.dot(mixed, wo_ref[...], preferred_element_type=F32)
    x1_ref[...] = x1

    ms = jnp.mean(x1 * x1, axis=-1, keepdims=True)
    h2 = ((x1 * lax.rsqrt(ms + EPS)) * g2_ref[...]).astype(BF16)
    h2_ref[...] = h2
    pq_ref[...] = jnp.dot(h2, wq_ref[...], preferred_element_type=F32)


def _mixer_call(x2, hc, bc, qn, kv, gc, ga, w_conv, biasm, sinks, w_conv_out, w_attn_out, w_out,
                norm_ffn, w_query, bsz, s, ts):
    t = bsz * s
    qd = w_query.shape[1]
    nj = s // ts
    row = lambda c: pl.BlockSpec((ts, c), lambda b, j: (b * nj + j, 0))
    prev_blk = lambda c: pl.BlockSpec((BLOCK, c), lambda b, j: (jnp.maximum((b * nj + j) * (ts // BLOCK) - 1, 0), 0))
    prev8 = pl.BlockSpec((SUBLANES, CONV_DIM),
                         lambda b, j: (jnp.maximum((b * nj + j) * (ts // SUBLANES) - 1, 0), 0))
    full = lambda shape: pl.BlockSpec(shape, lambda b, j: (0,) * len(shape))
    in_specs = [row(D_MODEL), row(CONV_DIM), prev8, row(CONV_DIM), row(Q_DIM),
                row(3 * KV_WIDE), prev_blk(3 * KV_WIDE), row(D_MODEL), row(D_MODEL),
                full((CONV_K, CONV_DIM)), full((N_HEADS, BLOCK, 2 * BLOCK)),
                pl.BlockSpec(memory_space=pltpu.SMEM),
                full((CONV_DIM, D_MODEL)), full((Q_DIM, D_MODEL)), full((D_MODEL, D_MODEL)),
                full((1, D_MODEL)), full((D_MODEL, qd))]
    operands = [x2, hc, hc, bc, qn, kv, kv, gc, ga, w_conv.astype(F32), biasm, sinks.astype(F32),
                w_conv_out.astype(BF16), w_attn_out.astype(BF16), w_out.astype(BF16),
                norm_ffn.reshape(1, D_MODEL).astype(F32), w_query.astype(BF16)]
    out_specs = [row(D_MODEL), row(D_MODEL), row(qd)]
    out_shape = [jax.ShapeDtypeStruct((t, D_MODEL), F32), jax.ShapeDtypeStruct((t, D_MODEL), BF16),
                 jax.ShapeDtypeStruct((t, qd), F32)]
    scratch_shapes = [pltpu.VMEM((ts, Q_DIM), BF16)]
    return pl.pallas_call(
        functools.partial(_mixer_kernel, ts=ts),
        grid=(bsz, nj),
        in_specs=in_specs,
        out_specs=out_specs,
        out_shape=out_shape,
        scratch_shapes=scratch_shapes,
        compiler_params=pltpu.CompilerParams(
            dimension_semantics=("arbitrary", "arbitrary"),
            vmem_limit_bytes=_vmem_limit(in_specs, operands, out_specs, out_shape, scratch_shapes)),
        name="peer_block_mixer",
    )(*operands)


def _band_bias(rel_bias):
    q_loc = jnp.arange(BLOCK, dtype=jnp.int32)[:, None]
    k_loc = jnp.arange(2 * BLOCK, dtype=jnp.int32)[None, :]
    dist = q_loc + BLOCK - k_loc
    max_exact = N_BUCKETS // 2
    d = jnp.maximum(dist, 0)
    df = jnp.maximum(d, 1).astype(F32)
    large = max_exact + (jnp.log(df / max_exact) / math.log(MAX_DISTANCE / max_exact)
                         * (N_BUCKETS - max_exact)).astype(jnp.int32)
    large = jnp.minimum(large, N_BUCKETS - 1)
    bucket = jnp.where(d < max_exact, d, large)
    onehot = (bucket[None] == jnp.arange(N_BUCKETS, dtype=jnp.int32)[:, None, None]).astype(F32)
    bias = jnp.einsum("bh,bqk->hqk", rel_bias.astype(F32), onehot, precision=lax.Precision.HIGHEST)
    valid = (dist >= 0) & (dist < WINDOW)
    return jnp.where(valid[None], bias, NEG_BIG)


def _sorting_network(n):
    def merge(lo, hi, r):
        step = r * 2
        if step < hi - lo:
            yield from merge(lo, hi, step)
            yield from merge(lo + r, hi, step)
            yield from ((i, i + r) for i in range(lo + r, hi - r, step))
        else:
            yield (lo, lo + r)

    def sort(lo, hi):
        if hi > lo:
            mid = lo + (hi - lo) // 2
            yield from sort(lo, mid)
            yield from sort(mid + 1, hi)
            yield from merge(lo, hi, 1)

    return list(sort(0, n - 1))


def _top16(vals, rid, big):
    n = vals.shape[0] // SUBLANES
    x = [vals[k * SUBLANES:(k + 1) * SUBLANES, :] for k in range(n)]
    ids = [rid[k * SUBLANES:(k + 1) * SUBLANES, :] for k in range(n)]
    for i, j in _sorting_network(n):
        a, b, ia, ib = x[i], x[j], ids[i], ids[j]
        first = (a > b) | ((a == b) & (ia < ib))
        x[i], x[j] = jnp.maximum(a, b), jnp.minimum(a, b)
        ids[i], ids[j] = jnp.where(first, ia, ib), jnp.where(first, ib, ia)
    tv, ti = [], []
    for r in range(PEER_TOPK):
        m = jnp.max(x[0], axis=0, keepdims=True)
        idx = jnp.min(jnp.where(x[0] == m, ids[0], big), axis=0, keepdims=True)
        tv.append(m)
        ti.append(idx)
        won = ids[0] == idx
        for k in range(min(n - 1, PEER_TOPK - 1 - r)):
            x[k] = jnp.where(won, x[k + 1], x[k])
            ids[k] = jnp.where(won, ids[k + 1], ids[k])
        if n - 1 < PEER_TOPK - 1 - r:
            x[n - 1] = jnp.where(won, -jnp.inf, x[n - 1])
    return jnp.concatenate(tv, axis=0), jnp.concatenate(ti, axis=0)


def _row_iota(n_rows):
    return lax.broadcasted_iota(jnp.int32, (n_rows, LANES), 0).astype(F32)


def _pair_candidates(v0, v1):
    assert PEER_TOPK == 2 * SUBLANES
    half = SUBLANES // 2
    r8 = _row_iota(SUBLANES)
    r16 = _row_iota(PEER_TOPK)
    vals = [v0[0:1, :] + v1]
    pos = [r16]
    for a in (1, 2, 3):
        vals.append(v0[a:a + 1, :] + v1[0:SUBLANES, :])
        pos.append(r8 + float(a * PEER_TOPK))
    low = r8 < float(half)
    v1dup = jnp.where(low, v1[0:SUBLANES, :], pltpu.roll(v1[0:SUBLANES, :], half, axis=0))
    bdup = jnp.where(low, r8, r8 - float(half))
    for a in (4, 6):
        vals.append(jnp.where(low, v0[a:a + 1, :], v0[a + 1:a + 2, :]) + v1dup)
        pos.append(jnp.where(low, float(a * PEER_TOPK), float((a + 1) * PEER_TOPK)) + bdup)
    vals.append(v0[SUBLANES:, :] + v1[0:1, :])
    pos.append((r8 + float(SUBLANES)) * float(PEER_TOPK))
    return jnp.concatenate(vals, axis=0), jnp.concatenate(pos, axis=0)


def _pick(table, sel):
    out = jnp.zeros_like(sel)
    for a in range(PEER_TOPK):
        out = jnp.where(sel == float(a), table[a:a + 1, :], out)
    return out


def _select_pairs(top0, top1):
    (v0, j0), (v1, j1) = top0, top1
    cand, cpos = _pair_candidates(v0, v1)
    ts_, pos = _top16(cand, cpos, float(PEER_TOPK * PEER_TOPK))
    pa = jnp.floor(pos * (1.0 / PEER_TOPK))
    pb = pos - pa * PEER_TOPK
    e = jnp.exp(ts_ - ts_[0:1, :])
    return _pick(j0, pa), _pick(j1, pb), e / jnp.sum(e, axis=0, keepdims=True)


def _gate_tiles(gd_ref, tg, i0b, i1b, gb, rid):
    zero = jnp.zeros_like(rid)
    one = jnp.ones_like(rid)
    for p in range(SUBLANES):
        tiles = []
        for t in (2 * p, 2 * p + 1):
            rt = jnp.where(rid == i0b[t:t + 1, :], gb[t:t + 1, :], zero)
            ct = jnp.where(rid == i1b[t:t + 1, :], one, zero)
            tiles.append(lax.dot_general(rt, ct, _NT, preferred_element_type=F32))
        lo_, hi_ = (pltpu.bitcast(t_.astype(BF16).astype(F32), jnp.uint32) for t_ in tiles)
        gd_ref[tg, pl.ds(p, N_KEYS, stride=SUBLANES), :] = (
            lax.shift_right_logical(lo_, jnp.uint32(16)) | (hi_ & jnp.uint32(0xFFFF0000)))


def _peer_kernel(x1_ref, h2_ref, q_ref, kh_ref, kl_ref, u_ref, v_ref, out_ref, gd_ref, rs_ref, tok_ref, *, tb, ec):
    ib = pl.program_id(0)
    c = pl.program_id(1)
    half_tok = tb // 2
    n_lg = half_tok // LANES
    n_pieces = ec // EXPERT_PIECE
    grp_per_piece = EXPERT_PIECE // N_KEYS
    w_slot = ib % 2
    r_slot = 1 - w_slot

    @pl.when(jnp.logical_and(ib > 0, c == 0))
    def _build_gate_matrix():
        rid = lax.broadcasted_iota(jnp.int32, (N_KEYS, N_SEL), 0).astype(F32).astype(BF16)

        def body(it, carry):
            for k in range(GATE_GROUPS_PER_ITER):
                tg = it * GATE_GROUPS_PER_ITER + k
                r0 = pl.multiple_of(tg * TOKENS_PER_GATE_GROUP, TOKENS_PER_GATE_GROUP)
                rows = pl.ds(r0, TOKENS_PER_GATE_GROUP)
                _gate_tiles(gd_ref, tg, tok_ref[r_slot, 0, rows, :].astype(BF16),
                            tok_ref[r_slot, 1, rows, :].astype(BF16),
                            (0.5 * tok_ref[r_slot, 2, rows, :]).astype(BF16), rid)
            return carry

        lax.fori_loop(0, tb // (TOKENS_PER_GATE_GROUP * GATE_GROUPS_PER_ITER), body, 0)
        out_ref[...] = x1_ref[...]

    hd = c // 2
    half = c % 2
    base = pl.multiple_of(c * (ec // N_KEYS * SUBLANES), SUBLANES)

    def scores(part):
        qh_, ql_ = _split_bf16(q_ref[:, part * PEER_HALF:(part + 1) * PEER_HALF])
        kh, kl = kh_ref[hd * 2 + part], kl_ref[hd * 2 + part]
        return (lax.dot_general(kh, qh_, _NT, preferred_element_type=F32)
                + lax.dot_general(kh, ql_, _NT, preferred_element_type=F32)
                + lax.dot_general(kl, qh_, _NT, preferred_element_type=F32))

    def weights(j):
        a = lax.dot_general(h2_ref[...], u_ref[j * EXPERT_PIECE:(j + 1) * EXPERT_PIECE, :], _NT,
                            preferred_element_type=F32)
        gd = jnp.concatenate(
            [pltpu.bitcast(gd_ref[:, pl.ds(base + (grp_per_piece * j + g) * SUBLANES, SUBLANES), :]
                           .reshape(tb // 2, N_KEYS), BF16) for g in range(grp_per_piece)], axis=1)
        return gd * (a * (1.0 + lax.erf(a * (2.0 ** -0.5)))).astype(BF16)

    def zero_after(words):
        u = pltpu.bitcast(words, jnp.uint32)
        z = lax.shift_right_logical(lax.shift_right_logical(u, jnp.uint32(16)), jnp.uint32(16))
        return jnp.max(z.astype(F32), axis=0, keepdims=True)

    def step(with_experts):
        tops = [[None] * n_lg, [None] * n_lg]
        w_pieces = []
        want = n_pieces if with_experts else 0
        for part in range(2):
            st = scores(part)
            for lg in range(n_lg):
                blk = st[:, lg * LANES:(lg + 1) * LANES]
                if with_experts and part == 1 and lg == n_lg - 1 and len(w_pieces) == n_pieces:
                    blk = blk + zero_after(w_pieces[-1][0:2 * SUBLANES, 0:LANES])
                tops[part][lg] = _top16(blk, _row_iota(N_KEYS), float(N_KEYS))
                w_pieces.extend(weights(j) for j in range(len(w_pieces), min(len(w_pieces) + 2, want)))
        w_pieces.extend(weights(j) for j in range(len(w_pieces), want))
        zero = 0.0
        if with_experts:
            acc = jnp.dot(jnp.concatenate(w_pieces, axis=1), v_ref[...], preferred_element_type=F32)
            zero = zero_after(acc[0:SUBLANES, 0:LANES])
        sel = [_select_pairs((tops[0][lg][0] + zero, tops[0][lg][1]), tops[1][lg]) for lg in range(n_lg)]
        rows = pl.ds(pl.multiple_of(hd * PEER_TOPK, PEER_TOPK), PEER_TOPK)
        for k in range(3):
            rs_ref[w_slot, half, k, rows, :] = jnp.concatenate([s_[k] for s_ in sel], axis=1)
        if with_experts:
            out_ref[...] += acc

    pl.when(ib == 0)(functools.partial(step, False))
    pl.when(ib > 0)(functools.partial(step, True))

    @pl.when(c == pl.num_programs(1) - 1)
    def _routing_to_token_major():
        for hf in range(2):
            for k in range(3):
                tok_ref[w_slot, k, hf * half_tok:(hf + 1) * half_tok, :] = rs_ref[w_slot, hf, k].T


def _peer_call(x1, h2, pq, sub_keys, expert_u, expert_v, tb, ec):
    t = x1.shape[0]
    nb = t // tb
    nc = N_EXPERTS // ec
    assert nc == 2 * PEER_HEADS and tb % (2 * LANES) == 0 and ec % EXPERT_PIECE == 0
    keys = sub_keys.astype(F32).reshape(PEER_HEADS * 2, N_KEYS, PEER_HALF)
    kh, kl = _split_bf16(keys)
    prev_row = lambda c_: pl.BlockSpec((tb, c_), lambda i, c: (jnp.maximum(i - 1, 0), 0))
    chunk = pl.BlockSpec((ec, D_MODEL), lambda i, c: (c, 0))
    q_spec = pl.BlockSpec((tb // 2, 2 * PEER_HALF), lambda i, c: (jnp.minimum(i, nb - 1) * 2 + c % 2, c // 2))
    full3 = pl.BlockSpec((PEER_HEADS * 2, N_KEYS, PEER_HALF), lambda i, c: (0, 0, 0))
    in_specs = [prev_row(D_MODEL), prev_row(D_MODEL), q_spec, full3, full3, chunk, chunk]
    operands = [x1, h2, pq, kh, kl, expert_u, expert_v]
    out_spec = prev_row(D_MODEL)
    out_shape = jax.ShapeDtypeStruct((t, D_MODEL), F32)
    scratch_shapes = [pltpu.VMEM((tb // TOKENS_PER_GATE_GROUP, N_KEYS * SUBLANES, N_KEYS), jnp.uint32),
                      pltpu.VMEM((2, 2, 3, N_SEL, tb // 2), F32),
                      pltpu.VMEM((2, 3, tb, N_SEL), F32)]
    return pl.pallas_call(
        functools.partial(_peer_kernel, tb=tb, ec=ec),
        grid=(nb + 1, nc),
        in_specs=in_specs,
        out_specs=out_spec,
        out_shape=out_shape,
        scratch_shapes=scratch_shapes,
        compiler_params=pltpu.CompilerParams(
            dimension_semantics=("arbitrary", "arbitrary"),
            vmem_limit_bytes=_vmem_limit(in_specs, operands, [out_spec], [out_shape], scratch_shapes)),
        name="peer_block_experts",
    )(*operands)


def _tile_sizes(bsz, s):
    t = bsz * s
    tm = math.gcd(t, 512)
    ts = math.gcd(s, 512)
    tb = math.gcd(t, 512)
    return tm, ts, tb


def kernel(x, norm_mix, norm_ffn, w_in, b_gate, w_conv, q_norm, k_norm, sinks, rel_bias, w_conv_out, w_attn_out,
           w_out, w_query, sub_keys, expert_u, expert_v):
    bsz, s, d = x.shape
    assert d == D_MODEL and s % BLOCK == 0
    tm, ts, tb = _tile_sizes(bsz, s)
    biasm = _band_bias(rel_bias)
    x2 = x.reshape(bsz * s, d)
    for l in range(norm_mix.shape[0]):
        steps = (bsz * s) // tm
        rows = N_EXPERTS // steps
        in_call = N_EXPERTS % steps == 0 and rows % TOKENS_PER_GATE_GROUP == 0 and rows * D_MODEL * 4 <= MAX_CAST_SLAB
        tables = (expert_u[l].astype(F32), expert_v[l].astype(F32)) if in_call else ()
        *cast, hc, bc, qn, kv, gc, ga = _proj_call(x2, norm_mix[l], w_in[l], b_gate[l], q_norm[l], k_norm[l], tm,
                                                   tables)
        u16, v16 = cast if in_call else (expert_u[l].astype(BF16), expert_v[l].astype(BF16))
        x1, h2, pq = _mixer_call(x2, hc, bc, qn, kv, gc, ga, w_conv[l], biasm, sinks[l], w_conv_out[l],
                                 w_attn_out[l], w_out[l], norm_ffn[l], w_query[l], bsz, s, ts)
        x2 = _peer_call(x1, h2, pq, sub_keys[l], u16, v16, tb, EXPERT_CHUNK)
    return x2.reshape(bsz, s, d)
```

```python
import functools
import math

import jax
import jax.numpy as jnp
import numpy as np
from jax import lax
from jax.experimental import pallas as pl
from jax.experimental.pallas import tpu as pltpu

F32 = jnp.float32
BF16 = jnp.bfloat16

D_MODEL = 1024
CONV_DIM = 512
CONV_K = 3
N_HEADS = 8
N_KV_HEADS = 2
HEAD_DIM = 64
Q_DIM = N_HEADS * HEAD_DIM
KV_DIM = N_KV_HEADS * HEAD_DIM
GROUP = N_HEADS // N_KV_HEADS
WINDOW = 128
BLOCK = 128
N_BUCKETS = 32
MAX_DISTANCE = 128
PEER_HEADS = 8
N_KEYS = 128
N_EXPERTS = N_KEYS * N_KEYS
PEER_QDIM = 256
PEER_HALF = PEER_QDIM // 2
PEER_TOPK = 16
N_SEL = PEER_HEADS * PEER_TOPK
EPS = 1e-6
NEG_BIG = -1e30
PAIR_W = 2 * HEAD_DIM
KV_WIDE = N_KV_HEADS * PAIR_W

SUBLANES = 8
LANES = 128

EXPERT_CHUNK = N_EXPERTS // (2 * PEER_HEADS)
EXPERT_PIECE = 2 * N_KEYS
TOKENS_PER_GATE_GROUP = 2 * SUBLANES
GATE_GROUPS_PER_ITER = 8
MAX_CAST_SLAB = 2 * 1024 * 1024
V7X_VMEM_BYTES = 64 * 1024 * 1024
COMPILER_TEMP_VMEM = 12 * 1024 * 1024

_OFFS = np.cumsum([0, CONV_DIM, CONV_DIM, CONV_DIM, Q_DIM, KV_DIM, KV_DIM, D_MODEL, D_MODEL]).tolist()

_NT = (((1,), (1,)), ((), ()))


def _vmem_limit(in_specs, operands, out_specs, out_shapes, scratch_shapes=()):
    nbytes = lambda shape, dtype: math.prod(shape) * jnp.dtype(dtype).itemsize
    windows = sum(nbytes(s.block_shape, a.dtype)
                  for s, a in zip(list(in_specs) + list(out_specs), list(operands) + list(out_shapes))
                  if s.block_shape is not None)
    need = 2 * windows + sum(nbytes(s.shape, s.dtype) for s in scratch_shapes) + COMPILER_TEMP_VMEM
    assert need <= V7X_VMEM_BYTES, need
    return need


def _split_bf16(a):
    hi = a.astype(BF16)
    lo = (a - hi.astype(F32)).astype(BF16)
    return hi, lo


def _proj_kernel(x_ref, g_ref, w_ref, bg_ref, qg_ref, kg_ref, avq_ref, avk_ref, exp_ref, u32_ref, v32_ref,
                 u16_ref, v16_ref, hc_ref, bc_ref, qn_ref, kv_ref, gc_ref, ga_ref):
    u16_ref[...] = u32_ref[...].astype(BF16)
    v16_ref[...] = v32_ref[...].astype(BF16)
    x = x_ref[...]
    ms = jnp.mean(x * x, axis=-1, keepdims=True)
    h = ((x * lax.rsqrt(ms + EPS)) * g_ref[...]).astype(BF16)

    def seg(i):
        return jnp.dot(h, w_ref[:, _OFFS[i]:_OFFS[i + 1]], preferred_element_type=F32)

    u = seg(0)
    hc_ref[...] = seg(2) * u
    bc_ref[...] = seg(1)

    def head_rms(a, av_ref):
        hi, lo = _split_bf16(a * a)
        return (jnp.dot(hi, av_ref[...], preferred_element_type=F32)
                + jnp.dot(lo, av_ref[...], preferred_element_type=F32))

    q = seg(3)
    qn = (q * lax.rsqrt(head_rms(q, avq_ref) + EPS)) * qg_ref[...] * (HEAD_DIM ** -0.5)
    qn_ref[...] = qn.astype(BF16)

    k = seg(4)
    kn = ((k * lax.rsqrt(head_rms(k, avk_ref) + EPS)) * kg_ref[...]).astype(BF16)
    v = seg(5).astype(BF16)
    kv_ref[...] = jnp.dot(jnp.concatenate([kn, v], axis=1), exp_ref[...], preferred_element_type=F32).astype(BF16)

    bg = bg_ref[...]
    gc_ref[...] = jax.nn.sigmoid(seg(6) + bg[:, :D_MODEL]).astype(BF16)
    ga_ref[...] = jax.nn.sigmoid(seg(7) + bg[:, D_MODEL:]).astype(BF16)


def _kv_layout():
    e = np.zeros((2 * KV_DIM, 3 * KV_WIDE), np.float32)
    for g in range(N_KV_HEADS):
        for d in range(HEAD_DIM):
            e[g * HEAD_DIM + d, g * PAIR_W + d] = 1.0
            e[g * HEAD_DIM + d, KV_WIDE + g * PAIR_W + HEAD_DIM + d] = 1.0
            e[KV_DIM + g * HEAD_DIM + d, 2 * KV_WIDE + g * PAIR_W + d] = 1.0
            e[KV_DIM + g * HEAD_DIM + d, 2 * KV_WIDE + g * PAIR_W + HEAD_DIM + d] = 1.0
    return jnp.asarray(e, BF16)


def _proj_call(x2, norm_mix, w_in, b_gate, q_norm, k_norm, expert_u, expert_v, tm):
    t = x2.shape[0]
    in_dim = w_in.shape[1]
    steps = t // tm
    slab_rows = N_EXPERTS // steps
    assert slab_rows * steps == N_EXPERTS and slab_rows % TOKENS_PER_GATE_GROUP == 0
    assert slab_rows * D_MODEL * 4 <= MAX_CAST_SLAB, "too few grid steps to cast the expert tables slice by slice"
    avq = jnp.kron(jnp.eye(N_HEADS, dtype=F32), jnp.full((HEAD_DIM, HEAD_DIM), 1.0 / HEAD_DIM, F32)).astype(BF16)
    avk = jnp.kron(jnp.eye(N_KV_HEADS, dtype=F32), jnp.full((HEAD_DIM, HEAD_DIM), 1.0 / HEAD_DIM, F32)).astype(BF16)
    qg = jnp.tile(q_norm.astype(F32), N_HEADS).reshape(1, Q_DIM)
    kg = jnp.tile(k_norm.astype(F32), N_KV_HEADS).reshape(1, KV_DIM)
    full = lambda shape: pl.BlockSpec(shape, lambda i: (0,) * len(shape))
    row = lambda c: pl.BlockSpec((tm, c), lambda i: (i, 0))
    slab = lambda: pl.BlockSpec((slab_rows, D_MODEL), lambda i: (i, 0))
    in_specs = [row(D_MODEL), full((1, D_MODEL)), full((D_MODEL, in_dim)), full((1, 2 * D_MODEL)),
                full((1, Q_DIM)), full((1, KV_DIM)), full((Q_DIM, Q_DIM)), full((KV_DIM, KV_DIM)),
                full((2 * KV_DIM, 3 * KV_WIDE)), slab(), slab()]
    operands = [x2, norm_mix.reshape(1, D_MODEL).astype(F32), w_in.astype(BF16),
                b_gate.reshape(1, 2 * D_MODEL).astype(F32), qg, kg, avq, avk, _kv_layout(),
                expert_u.astype(F32), expert_v.astype(F32)]
    out_specs = [slab(), slab(), row(CONV_DIM), row(CONV_DIM), row(Q_DIM), row(3 * KV_WIDE), row(D_MODEL),
                 row(D_MODEL)]
    out_shape = [jax.ShapeDtypeStruct((N_EXPERTS, D_MODEL), BF16), jax.ShapeDtypeStruct((N_EXPERTS, D_MODEL), BF16),
                 jax.ShapeDtypeStruct((t, CONV_DIM), F32), jax.ShapeDtypeStruct((t, CONV_DIM), F32),
                 jax.ShapeDtypeStruct((t, Q_DIM), BF16), jax.ShapeDtypeStruct((t, 3 * KV_WIDE), BF16),
                 jax.ShapeDtypeStruct((t, D_MODEL), BF16), jax.ShapeDtypeStruct((t, D_MODEL), BF16)]
    return pl.pallas_call(
        _proj_kernel,
        grid=(t // tm,),
        in_specs=in_specs,
        out_specs=out_specs,
        out_shape=out_shape,
        compiler_params=pltpu.CompilerParams(
            dimension_semantics=("arbitrary",),
            vmem_limit_bytes=_vmem_limit(in_specs, operands, out_specs, out_shape)),
        name="peer_block_proj",
    )(*operands)


def _mixer_kernel(x_ref, hc_ref, hcp_ref, bc_ref, qn_ref, kv_ref, kvp_ref, gc_ref, ga_ref,
                  wc_ref, bias_ref, sink_ref, wco_ref, wao_ref, wo_ref, g2_ref, wq_ref,
                  x1_ref, h2_ref, pq_ref, att_ref, *, ts):
    j = pl.program_id(1)
    first = j == 0

    hc = hc_ref[...]
    prev = jnp.where(first, 0.0, hcp_ref[...])
    rows = lax.broadcasted_iota(jnp.int32, hc.shape, 0)
    s1 = pltpu.roll(hc, 1, axis=0)
    s1 = jnp.where(rows == 0, prev[SUBLANES - 1:SUBLANES, :], s1)
    s2 = pltpu.roll(hc, 2, axis=0)
    s2 = jnp.where(rows == 0, prev[SUBLANES - 2:SUBLANES - 1, :], s2)
    s2 = jnp.where(rows == 1, prev[SUBLANES - 1:SUBLANES, :], s2)
    wc = wc_ref[...]
    conv = s2 * wc[0:1, :] + s1 * wc[1:2, :] + hc * wc[2:3, :]
    yc = (bc_ref[...] * conv).astype(BF16)
    y_conv = jnp.dot(yc, wco_ref[...], preferred_element_type=F32)

    kvfull = jnp.concatenate([kvp_ref[...], kv_ref[...]], axis=0)
    col = lax.broadcasted_iota(jnp.int32, (BLOCK, 2 * BLOCK), 1)
    pen0 = jnp.where(jnp.logical_and(first, col < BLOCK), NEG_BIG, 0.0)
    even_lanes = lax.broadcasted_iota(jnp.int32, (BLOCK, PAIR_W), 1) < HEAD_DIM
    for r in range(ts // BLOCK):
        kvb = kvfull[r * BLOCK:(r + 2) * BLOCK, :]
        for pair in range(N_HEADS // 2):
            g = (2 * pair) // GROUP
            qs = qn_ref[r * BLOCK:(r + 1) * BLOCK, pair * PAIR_W:(pair + 1) * PAIR_W]
            vsel = kvb[:, 2 * KV_WIDE + g * PAIR_W:2 * KV_WIDE + (g + 1) * PAIR_W]
            outs = []
            for par in range(2):
                hd = 2 * pair + par
                ksel = kvb[:, par * KV_WIDE + g * PAIR_W:par * KV_WIDE + (g + 1) * PAIR_W]
                lg = lax.dot_general(qs, ksel, _NT, preferred_element_type=F32) + bias_ref[hd]
                if r == 0:
                    lg = lg + pen0
                sink = sink_ref[hd]
                m = jnp.maximum(jnp.max(lg, axis=-1, keepdims=True), sink)
                p = jnp.exp(lg - m)
                denom = jnp.sum(p, axis=-1, keepdims=True) + jnp.exp(sink - m)
                pv = jnp.dot(p.astype(BF16), vsel, preferred_element_type=F32)
                outs.append(pv / denom)
            att_ref[r * BLOCK:(r + 1) * BLOCK, pair * PAIR_W:(pair + 1) * PAIR_W] = (
                jnp.where(even_lanes, outs[0], outs[1]).astype(BF16))
    y_attn = jnp.dot(att_ref[...], wao_ref[...], preferred_element_type=F32)

    mixed = (gc_ref[...].astype(F32) * y_conv + ga_ref[...].astype(F32) * y_attn).astype(BF16)
    x1 = x_ref[...] + jnp.dot(mixed, wo_ref[...], preferred_element_type=F32)
    x1_ref[...] = x1

    ms = jnp.mean(x1 * x1, axis=-1, keepdims=True)
    h2 = ((x1 * lax.rsqrt(ms + EPS)) * g2_ref[...]).astype(BF16)
    h2_ref[...] = h2
    pq_ref[...] = jnp.dot(h2, wq_ref[...], preferred_element_type=F32)


def _mixer_call(x2, hc, bc, qn, kv, gc, ga, w_conv, biasm, sinks, w_conv_out, w_attn_out, w_out,
                norm_ffn, w_query, bsz, s, ts):
    t = bsz * s
    qd = w_query.shape[1]
    nj = s // ts
    row = lambda c: pl.BlockSpec((ts, c), lambda b, j: (b * nj + j, 0))
    prev_blk = lambda c: pl.BlockSpec((BLOCK, c), lambda b, j: (jnp.maximum((b * nj + j) * (ts // BLOCK) - 1, 0), 0))
    prev8 = pl.BlockSpec((SUBLANES, CONV_DIM),
                         lambda b, j: (jnp.maximum((b * nj + j) * (ts // SUBLANES) - 1, 0), 0))
    full = lambda shape: pl.BlockSpec(shape, lambda b, j: (0,) * len(shape))
    in_specs = [row(D_MODEL), row(CONV_DIM), prev8, row(CONV_DIM), row(Q_DIM),
                row(3 * KV_WIDE), prev_blk(3 * KV_WIDE), row(D_MODEL), row(D_MODEL),
                full((CONV_K, CONV_DIM)), full((N_HEADS, BLOCK, 2 * BLOCK)),
                pl.BlockSpec(memory_space=pltpu.SMEM),
                full((CONV_DIM, D_MODEL)), full((Q_DIM, D_MODEL)), full((D_MODEL, D_MODEL)),
                full((1, D_MODEL)), full((D_MODEL, qd))]
    operands = [x2, hc, hc, bc, qn, kv, kv, gc, ga, w_conv.astype(F32), biasm, sinks.astype(F32),
                w_conv_out.astype(BF16), w_attn_out.astype(BF16), w_out.astype(BF16),
                norm_ffn.reshape(1, D_MODEL).astype(F32), w_query.astype(BF16)]
    out_specs = [row(D_MODEL), row(D_MODEL), row(qd)]
    out_shape = [jax.ShapeDtypeStruct((t, D_MODEL), F32), jax.ShapeDtypeStruct((t, D_MODEL), BF16),
                 jax.ShapeDtypeStruct((t, qd), F32)]
    scratch_shapes = [pltpu.VMEM((ts, Q_DIM), BF16)]
    return pl.pallas_call(
        functools.partial(_mixer_kernel, ts=ts),
        grid=(bsz, nj),
        in_specs=in_specs,
        out_specs=out_specs,
        out_shape=out_shape,
        scratch_shapes=scratch_shapes,
        compiler_params=pltpu.CompilerParams(
            dimension_semantics=("arbitrary", "arbitrary"),
            vmem_limit_bytes=_vmem_limit(in_specs, operands, out_specs, out_shape, scratch_shapes)),
        name="peer_block_mixer",
    )(*operands)


def _band_bias(rel_bias):
    q_loc = jnp.arange(BLOCK, dtype=jnp.int32)[:, None]
    k_loc = jnp.arange(2 * BLOCK, dtype=jnp.int32)[None, :]
    dist = q_loc + BLOCK - k_loc
    max_exact = N_BUCKETS // 2
    d = jnp.maximum(dist, 0)
    df = jnp.maximum(d, 1).astype(F32)
    large = max_exact + (jnp.log(df / max_exact) / math.log(MAX_DISTANCE / max_exact)
                         * (N_BUCKETS - max_exact)).astype(jnp.int32)
    large = jnp.minimum(large, N_BUCKETS - 1)
    bucket = jnp.where(d < max_exact, d, large)
    onehot = (bucket[None] == jnp.arange(N_BUCKETS, dtype=jnp.int32)[:, None, None]).astype(F32)
    bias = jnp.einsum("bh,bqk->hqk", rel_bias.astype(F32), onehot, precision=lax.Precision.HIGHEST)
    valid = (dist >= 0) & (dist < WINDOW)
    return jnp.where(valid[None], bias, NEG_BIG)


def _sorting_network(n):
    def merge(lo, hi, r):
        step = r * 2
        if step < hi - lo:
            yield from merge(lo, hi, step)
            yield from merge(lo + r, hi, step)
            yield from ((i, i + r) for i in range(lo + r, hi - r, step))
        else:
            yield (lo, lo + r)

    def sort(lo, hi):
        if hi > lo:
            mid = lo + (hi - lo) // 2
            yield from sort(lo, mid)
            yield from sort(mid + 1, hi)
            yield from merge(lo, hi, 1)

    return list(sort(0, n - 1))


def _top16(vals, rid, big):
    n = vals.shape[0] // SUBLANES
    x = [vals[k * SUBLANES:(k + 1) * SUBLANES, :] for k in range(n)]
    ids = [rid[k * SUBLANES:(k + 1) * SUBLANES, :] for k in range(n)]
    for i, j in _sorting_network(n):
        a, b, ia, ib = x[i], x[j], ids[i], ids[j]
        first = (a > b) | ((a == b) & (ia < ib))
        x[i], x[j] = jnp.maximum(a, b), jnp.minimum(a, b)
        ids[i], ids[j] = jnp.where(first, ia, ib), jnp.where(first, ib, ia)
    tv, ti = [], []
    for r in range(PEER_TOPK):
        m = jnp.max(x[0], axis=0, keepdims=True)
        idx = jnp.min(jnp.where(x[0] == m, ids[0], big), axis=0, keepdims=True)
        tv.append(m)
        ti.append(idx)
        won = ids[0] == idx
        for k in range(min(n - 1, PEER_TOPK - 1 - r)):
            x[k] = jnp.where(won, x[k + 1], x[k])
            ids[k] = jnp.where(won, ids[k + 1], ids[k])
        if n - 1 < PEER_TOPK - 1 - r:
            x[n - 1] = jnp.where(won, -jnp.inf, x[n - 1])
    return jnp.concatenate(tv, axis=0), jnp.concatenate(ti, axis=0)


def _row_iota(n_rows):
    return lax.broadcasted_iota(jnp.int32, (n_rows, LANES), 0).astype(F32)


def _pair_candidates(v0, v1):
    assert PEER_TOPK == 2 * SUBLANES
    half = SUBLANES // 2
    r8 = _row_iota(SUBLANES)
    r16 = _row_iota(PEER_TOPK)
    vals = [v0[0:1, :] + v1]
    pos = [r16]
    for a in (1, 2, 3):
        vals.append(v0[a:a + 1, :] + v1[0:SUBLANES, :])
        pos.append(r8 + float(a * PEER_TOPK))
    low = r8 < float(half)
    v1dup = jnp.where(low, v1[0:SUBLANES, :], pltpu.roll(v1[0:SUBLANES, :], half, axis=0))
    bdup = jnp.where(low, r8, r8 - float(half))
    for a in (4, 6):
        vals.append(jnp.where(low, v0[a:a + 1, :], v0[a + 1:a + 2, :]) + v1dup)
        pos.append(jnp.where(low, float(a * PEER_TOPK), float((a + 1) * PEER_TOPK)) + bdup)
    vals.append(v0[SUBLANES:, :] + v1[0:1, :])
    pos.append((r8 + float(SUBLANES)) * float(PEER_TOPK))
    return jnp.concatenate(vals, axis=0), jnp.concatenate(pos, axis=0)


def _pick(table, sel):
    out = jnp.zeros_like(sel)
    for a in range(PEER_TOPK):
        out = jnp.where(sel == float(a), table[a:a + 1, :], out)
    return out


def _select_pairs(top0, top1):
    (v0, j0), (v1, j1) = top0, top1
    cand, cpos = _pair_candidates(v0, v1)
    ts_, pos = _top16(cand, cpos, float(PEER_TOPK * PEER_TOPK))
    pa = jnp.floor(pos * (1.0 / PEER_TOPK))
    pb = pos - pa * PEER_TOPK
    e = jnp.exp(ts_ - ts_[0:1, :])
    return _pick(j0, pa), _pick(j1, pb), e / jnp.sum(e, axis=0, keepdims=True)


def _gate_tiles(gd_ref, tg, i0b, i1b, gb, rid):
    zero = jnp.zeros_like(rid)
    one = jnp.ones_like(rid)
    for p in range(SUBLANES):
        tiles = []
        for t in (2 * p, 2 * p + 1):
            rt = jnp.where(rid == i0b[t:t + 1, :], gb[t:t + 1, :], zero)
            ct = jnp.where(rid == i1b[t:t + 1, :], one, zero)
            tiles.append(lax.dot_general(rt, ct, _NT, preferred_element_type=F32))
        lo_, hi_ = (pltpu.bitcast(t_.astype(BF16).astype(F32), jnp.uint32) for t_ in tiles)
        gd_ref[tg, pl.ds(p, N_KEYS, stride=SUBLANES), :] = (
            lax.shift_right_logical(lo_, jnp.uint32(16)) | (hi_ & jnp.uint32(0xFFFF0000)))


def _peer_kernel(x1_ref, h2_ref, q_ref, kh_ref, kl_ref, u_ref, v_ref, out_ref, gd_ref, rs_ref, tok_ref, *, tb, ec):
    ib = pl.program_id(0)
    c = pl.program_id(1)
    half_tok = tb // 2
    n_lg = half_tok // LANES
    n_pieces = ec // EXPERT_PIECE
    grp_per_piece = EXPERT_PIECE // N_KEYS
    w_slot = ib % 2
    r_slot = 1 - w_slot

    @pl.when(jnp.logical_and(ib > 0, c == 0))
    def _build_gate_matrix():
        rid = lax.broadcasted_iota(jnp.int32, (N_KEYS, N_SEL), 0).astype(F32).astype(BF16)

        def body(it, carry):
            for k in range(GATE_GROUPS_PER_ITER):
                tg = it * GATE_GROUPS_PER_ITER + k
                r0 = pl.multiple_of(tg * TOKENS_PER_GATE_GROUP, TOKENS_PER_GATE_GROUP)
                rows = pl.ds(r0, TOKENS_PER_GATE_GROUP)
                _gate_tiles(gd_ref, tg, tok_ref[r_slot, 0, rows, :].astype(BF16),
                            tok_ref[r_slot, 1, rows, :].astype(BF16),
                            (0.5 * tok_ref[r_slot, 2, rows, :]).astype(BF16), rid)
            return carry

        lax.fori_loop(0, tb // (TOKENS_PER_GATE_GROUP * GATE_GROUPS_PER_ITER), body, 0)
        out_ref[...] = x1_ref[...]

    hd = c // 2
    half = c % 2
    base = pl.multiple_of(c * (ec // N_KEYS * SUBLANES), SUBLANES)

    def scores(part):
        qh_, ql_ = _split_bf16(q_ref[:, part * PEER_HALF:(part + 1) * PEER_HALF])
        kh, kl = kh_ref[hd * 2 + part], kl_ref[hd * 2 + part]
        return (lax.dot_general(kh, qh_, _NT, preferred_element_type=F32)
                + lax.dot_general(kh, ql_, _NT, preferred_element_type=F32)
                + lax.dot_general(kl, qh_, _NT, preferred_element_type=F32))

    def weights(j):
        a = lax.dot_general(h2_ref[...], u_ref[j * EXPERT_PIECE:(j + 1) * EXPERT_PIECE, :], _NT,
                            preferred_element_type=F32)
        gd = jnp.concatenate(
            [pltpu.bitcast(gd_ref[:, pl.ds(base + (grp_per_piece * j + g) * SUBLANES, SUBLANES), :]
                           .reshape(tb // 2, N_KEYS), BF16) for g in range(grp_per_piece)], axis=1)
        return gd * (a * (1.0 + lax.erf(a * (2.0 ** -0.5)))).astype(BF16)

    def zero_after(words):
        u = pltpu.bitcast(words, jnp.uint32)
        z = lax.shift_right_logical(lax.shift_right_logical(u, jnp.uint32(16)), jnp.uint32(16))
        return jnp.max(z.astype(F32), axis=0, keepdims=True)

    def step(with_experts):
        tops = [[None] * n_lg, [None] * n_lg]
        w_pieces = []
        want = n_pieces if with_experts else 0
        for part in range(2):
            st = scores(part)
            for lg in range(n_lg):
                blk = st[:, lg * LANES:(lg + 1) * LANES]
                if with_experts and part == 1 and lg == n_lg - 1 and len(w_pieces) == n_pieces:
                    blk = blk + zero_after(w_pieces[-1][0:2 * SUBLANES, 0:LANES])
                tops[part][lg] = _top16(blk, _row_iota(N_KEYS), float(N_KEYS))
                w_pieces.extend(weights(j) for j in range(len(w_pieces), min(len(w_pieces) + 2, want)))
        w_pieces.extend(weights(j) for j in range(len(w_pieces), want))
        zero = 0.0
        if with_experts:
            acc = jnp.dot(jnp.concatenate(w_pieces, axis=1), v_ref[...], preferred_element_type=F32)
            zero = zero_after(acc[0:SUBLANES, 0:LANES])
        sel = [_select_pairs((tops[0][lg][0] + zero, tops[0][lg][1]), tops[1][lg]) for lg in range(n_lg)]
        rows = pl.ds(pl.multiple_of(hd * PEER_TOPK, PEER_TOPK), PEER_TOPK)
        for k in range(3):
            rs_ref[w_slot, half, k, rows, :] = jnp.concatenate([s_[k] for s_ in sel], axis=1)
        if with_experts:
            out_ref[...] += acc

    pl.when(ib == 0)(functools.partial(step, False))
    pl.when(ib > 0)(functools.partial(step, True))

    @pl.when(c == pl.num_programs(1) - 1)
    def _routing_to_token_major():
        for hf in range(2):
            for k in range(3):
                tok_ref[w_slot, k, hf * half_tok:(hf + 1) * half_tok, :] = rs_ref[w_slot, hf, k].T


def _peer_call(x1, h2, pq, sub_keys, expert_u, expert_v, tb, ec):
    t = x1.shape[0]
    nb = t // tb
    nc = N_EXPERTS // ec
    assert nc == 2 * PEER_HEADS and tb % (2 * LANES) == 0 and ec % EXPERT_PIECE == 0
    keys = sub_keys.astype(F32).reshape(PEER_HEADS * 2, N_KEYS, PEER_HALF)
    kh, kl = _split_bf16(keys)
    prev_row = lambda c_: pl.BlockSpec((tb, c_), lambda i, c: (jnp.maximum(i - 1, 0), 0))
    chunk = pl.BlockSpec((ec, D_MODEL), lambda i, c: (c, 0))
    q_spec = pl.BlockSpec((tb // 2, 2 * PEER_HALF), lambda i, c: (jnp.minimum(i, nb - 1) * 2 + c % 2, c // 2))
    full3 = pl.BlockSpec((PEER_HEADS * 2, N_KEYS, PEER_HALF), lambda i, c: (0, 0, 0))
    in_specs = [prev_row(D_MODEL), prev_row(D_MODEL), q_spec, full3, full3, chunk, chunk]
    operands = [x1, h2, pq, kh, kl, expert_u, expert_v]
    out_spec = prev_row(D_MODEL)
    out_shape = jax.ShapeDtypeStruct((t, D_MODEL), F32)
    scratch_shapes = [pltpu.VMEM((tb // TOKENS_PER_GATE_GROUP, N_KEYS * SUBLANES, N_KEYS), jnp.uint32),
                      pltpu.VMEM((2, 2, 3, N_SEL, tb // 2), F32),
                      pltpu.VMEM((2, 3, tb, N_SEL), F32)]
    return pl.pallas_call(
        functools.partial(_peer_kernel, tb=tb, ec=ec),
        grid=(nb + 1, nc),
        in_specs=in_specs,
        out_specs=out_spec,
        out_shape=out_shape,
        scratch_shapes=scratch_shapes,
        compiler_params=pltpu.CompilerParams(
            dimension_semantics=("arbitrary", "arbitrary"),
            vmem_limit_bytes=_vmem_limit(in_specs, operands, [out_spec], [out_shape], scratch_shapes)),
        name="peer_block_experts",
    )(*operands)


def _tile_sizes(bsz, s):
    t = bsz * s
    tm = math.gcd(t, 512)
    ts = math.gcd(s, 512)
    tb = math.gcd(t, 512)
    return tm, ts, tb


def kernel(x, norm_mix, norm_ffn, w_in, b_gate, w_conv, q_norm, k_norm, sinks, rel_bias, w_conv_out, w_attn_out,
           w_out, w_query, sub_keys, expert_u, expert_v):
    bsz, s, d = x.shape
    assert d == D_MODEL and s % BLOCK == 0
    tm, ts, tb = _tile_sizes(bsz, s)
    biasm = _band_bias(rel_bias)
    x2 = x.reshape(bsz * s, d)
    for l in range(norm_mix.shape[0]):
        u16, v16, hc, bc, qn, kv, gc, ga = _proj_call(x2, norm_mix[l], w_in[l], b_gate[l], q_norm[l], k_norm[l],
                                                      expert_u[l], expert_v[l], tm)
        x1, h2, pq = _mixer_call(x2, hc, bc, qn, kv, gc, ga, w_conv[l], biasm, sinks[l], w_conv_out[l],
                                 w_attn_out[l], w_out[l], norm_ffn[l], w_query[l], bsz, s, ts)
        x2 = _peer_call(x1, h2, pq, sub_keys[l], u16, v16, tb, EXPERT_CHUNK)
    return x2.reshape(bsz, s, d)
```

```python
import functools
import math

import jax
import jax.numpy as jnp
import numpy as np
from jax import lax
from jax.experimental import pallas as pl
from jax.experimental.pallas import tpu as pltpu

F32 = jnp.float32
BF16 = jnp.bfloat16

D_MODEL = 1024
CONV_DIM = 512
CONV_K = 3
N_HEADS = 8
N_KV_HEADS = 2
HEAD_DIM = 64
Q_DIM = N_HEADS * HEAD_DIM
KV_DIM = N_KV_HEADS * HEAD_DIM
GROUP = N_HEADS // N_KV_HEADS
WINDOW = 128
BLOCK = 128
N_BUCKETS = 32
MAX_DISTANCE = 128
PEER_HEADS = 8
N_KEYS = 128
N_EXPERTS = N_KEYS * N_KEYS
PEER_QDIM = 256
PEER_HALF = PEER_QDIM // 2
PEER_TOPK = 16
N_SEL = PEER_HEADS * PEER_TOPK
EPS = 1e-6
NEG_BIG = -1e30
PAIR_W = 2 * HEAD_DIM
KV_WIDE = N_KV_HEADS * PAIR_W

SUBLANES = 8
LANES = 128

EXPERT_CHUNK = N_EXPERTS // (2 * PEER_HEADS)
EXPERT_PIECE = 2 * N_KEYS
TOKENS_PER_GATE_GROUP = 2 * SUBLANES
GATE_GROUPS_PER_ITER = 16
MAX_CAST_SLAB = 2 * 1024 * 1024
V7X_VMEM_BYTES = 64 * 1024 * 1024
COMPILER_TEMP_VMEM = 12 * 1024 * 1024

_OFFS = np.cumsum([0, CONV_DIM, CONV_DIM, CONV_DIM, Q_DIM, KV_DIM, KV_DIM, D_MODEL, D_MODEL]).tolist()

_NT = (((1,), (1,)), ((), ()))


def _vmem_limit(in_specs, operands, out_specs, out_shapes, scratch_shapes=()):
    nbytes = lambda shape, dtype: math.prod(shape) * jnp.dtype(dtype).itemsize
    windows = sum(nbytes(s.block_shape, a.dtype)
                  for s, a in zip(list(in_specs) + list(out_specs), list(operands) + list(out_shapes))
                  if s.block_shape is not None)
    need = 2 * windows + sum(nbytes(s.shape, s.dtype) for s in scratch_shapes) + COMPILER_TEMP_VMEM
    assert need <= V7X_VMEM_BYTES, need
    return need


def _split_bf16(a):
    hi = a.astype(BF16)
    lo = (a - hi.astype(F32)).astype(BF16)
    return hi, lo


def _proj_kernel(x_ref, g_ref, w_ref, bg_ref, qg_ref, kg_ref, avq_ref, avk_ref, exp_ref, u32_ref, v32_ref,
                 u16_ref, v16_ref, hc_ref, bc_ref, qn_ref, kv_ref, gc_ref, ga_ref):
    u16_ref[...] = u32_ref[...].astype(BF16)
    v16_ref[...] = v32_ref[...].astype(BF16)
    x = x_ref[...]
    ms = jnp.mean(x * x, axis=-1, keepdims=True)
    h = ((x * lax.rsqrt(ms + EPS)) * g_ref[...]).astype(BF16)

    def seg(i):
        return jnp.dot(h, w_ref[:, _OFFS[i]:_OFFS[i + 1]], preferred_element_type=F32)

    u = seg(0)
    hc_ref[...] = seg(2) * u
    bc_ref[...] = seg(1)

    def head_rms(a, av_ref):
        hi, lo = _split_bf16(a * a)
        return (jnp.dot(hi, av_ref[...], preferred_element_type=F32)
                + jnp.dot(lo, av_ref[...], preferred_element_type=F32))

    q = seg(3)
    qn = (q * lax.rsqrt(head_rms(q, avq_ref) + EPS)) * qg_ref[...] * (HEAD_DIM ** -0.5)
    qn_ref[...] = qn.astype(BF16)

    k = seg(4)
    kn = ((k * lax.rsqrt(head_rms(k, avk_ref) + EPS)) * kg_ref[...]).astype(BF16)
    v = seg(5).astype(BF16)
    kv_ref[...] = jnp.dot(jnp.concatenate([kn, v], axis=1), exp_ref[...], preferred_element_type=F32).astype(BF16)

    bg = bg_ref[...]
    gc_ref[...] = jax.nn.sigmoid(seg(6) + bg[:, :D_MODEL]).astype(BF16)
    ga_ref[...] = jax.nn.sigmoid(seg(7) + bg[:, D_MODEL:]).astype(BF16)


def _kv_layout():
    e = np.zeros((2 * KV_DIM, 3 * KV_WIDE), np.float32)
    for g in range(N_KV_HEADS):
        for d in range(HEAD_DIM):
            e[g * HEAD_DIM + d, g * PAIR_W + d] = 1.0
            e[g * HEAD_DIM + d, KV_WIDE + g * PAIR_W + HEAD_DIM + d] = 1.0
            e[KV_DIM + g * HEAD_DIM + d, 2 * KV_WIDE + g * PAIR_W + d] = 1.0
            e[KV_DIM + g * HEAD_DIM + d, 2 * KV_WIDE + g * PAIR_W + HEAD_DIM + d] = 1.0
    return jnp.asarray(e, BF16)


def _proj_call(x2, norm_mix, w_in, b_gate, q_norm, k_norm, expert_u, expert_v, tm):
    t = x2.shape[0]
    in_dim = w_in.shape[1]
    steps = t // tm
    slab_rows = N_EXPERTS // steps
    assert slab_rows * steps == N_EXPERTS and slab_rows % TOKENS_PER_GATE_GROUP == 0
    assert slab_rows * D_MODEL * 4 <= MAX_CAST_SLAB, "too few grid steps to cast the expert tables slice by slice"
    avq = jnp.kron(jnp.eye(N_HEADS, dtype=F32), jnp.full((HEAD_DIM, HEAD_DIM), 1.0 / HEAD_DIM, F32)).astype(BF16)
    avk = jnp.kron(jnp.eye(N_KV_HEADS, dtype=F32), jnp.full((HEAD_DIM, HEAD_DIM), 1.0 / HEAD_DIM, F32)).astype(BF16)
    qg = jnp.tile(q_norm.astype(F32), N_HEADS).reshape(1, Q_DIM)
    kg = jnp.tile(k_norm.astype(F32), N_KV_HEADS).reshape(1, KV_DIM)
    full = lambda shape: pl.BlockSpec(shape, lambda i: (0,) * len(shape))
    row = lambda c: pl.BlockSpec((tm, c), lambda i: (i, 0))
    slab = lambda: pl.BlockSpec((slab_rows, D_MODEL), lambda i: (i, 0))
    in_specs = [row(D_MODEL), full((1, D_MODEL)), full((D_MODEL, in_dim)), full((1, 2 * D_MODEL)),
                full((1, Q_DIM)), full((1, KV_DIM)), full((Q_DIM, Q_DIM)), full((KV_DIM, KV_DIM)),
                full((2 * KV_DIM, 3 * KV_WIDE)), slab(), slab()]
    operands = [x2, norm_mix.reshape(1, D_MODEL).astype(F32), w_in.astype(BF16),
                b_gate.reshape(1, 2 * D_MODEL).astype(F32), qg, kg, avq, avk, _kv_layout(),
                expert_u.astype(F32), expert_v.astype(F32)]
    out_specs = [slab(), slab(), row(CONV_DIM), row(CONV_DIM), row(Q_DIM), row(3 * KV_WIDE), row(D_MODEL),
                 row(D_MODEL)]
    out_shape = [jax.ShapeDtypeStruct((N_EXPERTS, D_MODEL), BF16), jax.ShapeDtypeStruct((N_EXPERTS, D_MODEL), BF16),
                 jax.ShapeDtypeStruct((t, CONV_DIM), F32), jax.ShapeDtypeStruct((t, CONV_DIM), F32),
                 jax.ShapeDtypeStruct((t, Q_DIM), BF16), jax.ShapeDtypeStruct((t, 3 * KV_WIDE), BF16),
                 jax.ShapeDtypeStruct((t, D_MODEL), BF16), jax.ShapeDtypeStruct((t, D_MODEL), BF16)]
    return pl.pallas_call(
        _proj_kernel,
        grid=(t // tm,),
        in_specs=in_specs,
        out_specs=out_specs,
        out_shape=out_shape,
        compiler_params=pltpu.CompilerParams(
            dimension_semantics=("arbitrary",),
            vmem_limit_bytes=_vmem_limit(in_specs, operands, out_specs, out_shape)),
        name="peer_block_proj",
    )(*operands)


def _mixer_kernel(x_ref, hc_ref, hcp_ref, bc_ref, qn_ref, kv_ref, kvp_ref, gc_ref, ga_ref,
                  wc_ref, bias_ref, sink_ref, wco_ref, wao_ref, wo_ref, g2_ref, wq_ref,
                  x1_ref, h2_ref, pq_ref, att_ref, *, ts):
    j = pl.program_id(1)
    first = j == 0

    hc = hc_ref[...]
    prev = jnp.where(first, 0.0, hcp_ref[...])
    rows = lax.broadcasted_iota(jnp.int32, hc.shape, 0)
    s1 = pltpu.roll(hc, 1, axis=0)
    s1 = jnp.where(rows == 0, prev[SUBLANES - 1:SUBLANES, :], s1)
    s2 = pltpu.roll(hc, 2, axis=0)
    s2 = jnp.where(rows == 0, prev[SUBLANES - 2:SUBLANES - 1, :], s2)
    s2 = jnp.where(rows == 1, prev[SUBLANES - 1:SUBLANES, :], s2)
    wc = wc_ref[...]
    conv = s2 * wc[0:1, :] + s1 * wc[1:2, :] + hc * wc[2:3, :]
    yc = (bc_ref[...] * conv).astype(BF16)
    y_conv = jnp.dot(yc, wco_ref[...], preferred_element_type=F32)

    kvfull = jnp.concatenate([kvp_ref[...], kv_ref[...]], axis=0)
    col = lax.broadcasted_iota(jnp.int32, (BLOCK, 2 * BLOCK), 1)
    pen0 = jnp.where(jnp.logical_and(first, col < BLOCK), NEG_BIG, 0.0)
    even_lanes = lax.broadcasted_iota(jnp.int32, (BLOCK, PAIR_W), 1) < HEAD_DIM
    for r in range(ts // BLOCK):
        kvb = kvfull[r * BLOCK:(r + 2) * BLOCK, :]
        for pair in range(N_HEADS // 2):
            g = (2 * pair) // GROUP
            qs = qn_ref[r * BLOCK:(r + 1) * BLOCK, pair * PAIR_W:(pair + 1) * PAIR_W]
            vsel = kvb[:, 2 * KV_WIDE + g * PAIR_W:2 * KV_WIDE + (g + 1) * PAIR_W]
            outs = []
            for par in range(2):
                hd = 2 * pair + par
                ksel = kvb[:, par * KV_WIDE + g * PAIR_W:par * KV_WIDE + (g + 1) * PAIR_W]
                lg = lax.dot_general(qs, ksel, _NT, preferred_element_type=F32) + bias_ref[hd]
                if r == 0:
                    lg = lg + pen0
                sink = sink_ref[hd]
                m = jnp.maximum(jnp.max(lg, axis=-1, keepdims=True), sink)
                p = jnp.exp(lg - m)
                denom = jnp.sum(p, axis=-1, keepdims=True) + jnp.exp(sink - m)
                pv = jnp.dot(p.astype(BF16), vsel, preferred_element_type=F32)
                outs.append(pv / denom)
            att_ref[r * BLOCK:(r + 1) * BLOCK, pair * PAIR_W:(pair + 1) * PAIR_W] = (
                jnp.where(even_lanes, outs[0], outs[1]).astype(BF16))
    y_attn = jnp.dot(att_ref[...], wao_ref[...], preferred_element_type=F32)

    mixed = (gc_ref[...].astype(F32) * y_conv + ga_ref[...].astype(F32) * y_attn).astype(BF16)
    x1 = x_ref[...] + jnp.dot(mixed, wo_ref[...], preferred_element_type=F32)
    x1_ref[...] = x1

    ms = jnp.mean(x1 * x1, axis=-1, keepdims=True)
    h2 = ((x1 * lax.rsqrt(ms + EPS)) * g2_ref[...]).astype(BF16)
    h2_ref[...] = h2
    pq_ref[...] = jnp.dot(h2, wq_ref[...], preferred_element_type=F32)


def _mixer_call(x2, hc, bc, qn, kv, gc, ga, w_conv, biasm, sinks, w_conv_out, w_attn_out, w_out,
                norm_ffn, w_query, bsz, s, ts):
    t = bsz * s
    qd = w_query.shape[1]
    nj = s // ts
    row = lambda c: pl.BlockSpec((ts, c), lambda b, j: (b * nj + j, 0))
    prev_blk = lambda c: pl.BlockSpec((BLOCK, c), lambda b, j: (jnp.maximum((b * nj + j) * (ts // BLOCK) - 1, 0), 0))
    prev8 = pl.BlockSpec((SUBLANES, CONV_DIM),
                         lambda b, j: (jnp.maximum((b * nj + j) * (ts // SUBLANES) - 1, 0), 0))
    full = lambda shape: pl.BlockSpec(shape, lambda b, j: (0,) * len(shape))
    in_specs = [row(D_MODEL), row(CONV_DIM), prev8, row(CONV_DIM), row(Q_DIM),
                row(3 * KV_WIDE), prev_blk(3 * KV_WIDE), row(D_MODEL), row(D_MODEL),
                full((CONV_K, CONV_DIM)), full((N_HEADS, BLOCK, 2 * BLOCK)),
                pl.BlockSpec(memory_space=pltpu.SMEM),
                full((CONV_DIM, D_MODEL)), full((Q_DIM, D_MODEL)), full((D_MODEL, D_MODEL)),
                full((1, D_MODEL)), full((D_MODEL, qd))]
    operands = [x2, hc, hc, bc, qn, kv, kv, gc, ga, w_conv.astype(F32), biasm, sinks.astype(F32),
                w_conv_out.astype(BF16), w_attn_out.astype(BF16), w_out.astype(BF16),
                norm_ffn.reshape(1, D_MODEL).astype(F32), w_query.astype(BF16)]
    out_specs = [row(D_MODEL), row(D_MODEL), row(qd)]
    out_shape = [jax.ShapeDtypeStruct((t, D_MODEL), F32), jax.ShapeDtypeStruct((t, D_MODEL), BF16),
                 jax.ShapeDtypeStruct((t, qd), F32)]
    scratch_shapes = [pltpu.VMEM((ts, Q_DIM), BF16)]
    return pl.pallas_call(
        functools.partial(_mixer_kernel, ts=ts),
        grid=(bsz, nj),
        in_specs=in_specs,
        out_specs=out_specs,
        out_shape=out_shape,
        scratch_shapes=scratch_shapes,
        compiler_params=pltpu.CompilerParams(
            dimension_semantics=("arbitrary", "arbitrary"),
            vmem_limit_bytes=_vmem_limit(in_specs, operands, out_specs, out_shape, scratch_shapes)),
        name="peer_block_mixer",
    )(*operands)


def _band_bias(rel_bias):
    q_loc = jnp.arange(BLOCK, dtype=jnp.int32)[:, None]
    k_loc = jnp.arange(2 * BLOCK, dtype=jnp.int32)[None, :]
    dist = q_loc + BLOCK - k_loc
    max_exact = N_BUCKETS // 2
    d = jnp.maximum(dist, 0)
    df = jnp.maximum(d, 1).astype(F32)
    large = max_exact + (jnp.log(df / max_exact) / math.log(MAX_DISTANCE / max_exact)
                         * (N_BUCKETS - max_exact)).astype(jnp.int32)
    large = jnp.minimum(large, N_BUCKETS - 1)
    bucket = jnp.where(d < max_exact, d, large)
    onehot = (bucket[None] == jnp.arange(N_BUCKETS, dtype=jnp.int32)[:, None, None]).astype(F32)
    bias = jnp.einsum("bh,bqk->hqk", rel_bias.astype(F32), onehot, precision=lax.Precision.HIGHEST)
    valid = (dist >= 0) & (dist < WINDOW)
    return jnp.where(valid[None], bias, NEG_BIG)


def _sorting_network(n):
    def merge(lo, hi, r):
        step = r * 2
        if step < hi - lo:
            yield from merge(lo, hi, step)
            yield from merge(lo + r, hi, step)
            yield from ((i, i + r) for i in range(lo + r, hi - r, step))
        else:
            yield (lo, lo + r)

    def sort(lo, hi):
        if hi > lo:
            mid = lo + (hi - lo) // 2
            yield from sort(lo, mid)
            yield from sort(mid + 1, hi)
            yield from merge(lo, hi, 1)

    return list(sort(0, n - 1))


def _top16(vals, rid, big):
    n = vals.shape[0] // SUBLANES
    x = [vals[k * SUBLANES:(k + 1) * SUBLANES, :] for k in range(n)]
    ids = [rid[k * SUBLANES:(k + 1) * SUBLANES, :] for k in range(n)]
    for i, j in _sorting_network(n):
        a, b, ia, ib = x[i], x[j], ids[i], ids[j]
        first = (a > b) | ((a == b) & (ia < ib))
        x[i], x[j] = jnp.maximum(a, b), jnp.minimum(a, b)
        ids[i], ids[j] = jnp.where(first, ia, ib), jnp.where(first, ib, ia)
    tv, ti = [], []
    for r in range(PEER_TOPK):
        m = jnp.max(x[0], axis=0, keepdims=True)
        idx = jnp.min(jnp.where(x[0] == m, ids[0], big), axis=0, keepdims=True)
        tv.append(m)
        ti.append(idx)
        won = ids[0] == idx
        for k in range(min(n - 1, PEER_TOPK - 1 - r)):
            x[k] = jnp.where(won, x[k + 1], x[k])
            ids[k] = jnp.where(won, ids[k + 1], ids[k])
        if n - 1 < PEER_TOPK - 1 - r:
            x[n - 1] = jnp.where(won, -jnp.inf, x[n - 1])
    return jnp.concatenate(tv, axis=0), jnp.concatenate(ti, axis=0)


def _row_iota(n_rows):
    return lax.broadcasted_iota(jnp.int32, (n_rows, LANES), 0).astype(F32)


def _pair_candidates(v0, v1):
    assert PEER_TOPK == 2 * SUBLANES
    half = SUBLANES // 2
    r8 = _row_iota(SUBLANES)
    r16 = _row_iota(PEER_TOPK)
    vals = [v0[0:1, :] + v1]
    pos = [r16]
    for a in (1, 2, 3):
        vals.append(v0[a:a + 1, :] + v1[0:SUBLANES, :])
        pos.append(r8 + float(a * PEER_TOPK))
    low = r8 < float(half)
    v1dup = jnp.where(low, v1[0:SUBLANES, :], pltpu.roll(v1[0:SUBLANES, :], half, axis=0))
    bdup = jnp.where(low, r8, r8 - float(half))
    for a in (4, 6):
        vals.append(jnp.where(low, v0[a:a + 1, :], v0[a + 1:a + 2, :]) + v1dup)
        pos.append(jnp.where(low, float(a * PEER_TOPK), float((a + 1) * PEER_TOPK)) + bdup)
    vals.append(v0[SUBLANES:, :] + v1[0:1, :])
    pos.append((r8 + float(SUBLANES)) * float(PEER_TOPK))
    return jnp.concatenate(vals, axis=0), jnp.concatenate(pos, axis=0)


def _pick(table, sel):
    out = jnp.zeros_like(sel)
    for a in range(PEER_TOPK):
        out = jnp.where(sel == float(a), table[a:a + 1, :], out)
    return out


def _select_pairs(top0, top1):
    (v0, j0), (v1, j1) = top0, top1
    cand, cpos = _pair_candidates(v0, v1)
    ts_, pos = _top16(cand, cpos, float(PEER_TOPK * PEER_TOPK))
    pa = jnp.floor(pos * (1.0 / PEER_TOPK))
    pb = pos - pa * PEER_TOPK
    e = jnp.exp(ts_ - ts_[0:1, :])
    return _pick(j0, pa), _pick(j1, pb), e / jnp.sum(e, axis=0, keepdims=True)


def _gate_tiles(gd_ref, tg, i0b, i1b, gb, rid):
    zero = jnp.zeros_like(rid)
    one = jnp.ones_like(rid)
    for p in range(SUBLANES):
        tiles = []
        for t in (2 * p, 2 * p + 1):
            rt = jnp.where(rid == i0b[t:t + 1, :], gb[t:t + 1, :], zero)
            ct = jnp.where(rid == i1b[t:t + 1, :], one, zero)
            tiles.append(lax.dot_general(rt, ct, _NT, preferred_element_type=F32))
        lo_, hi_ = (pltpu.bitcast(t_.astype(BF16).astype(F32), jnp.uint32) for t_ in tiles)
        gd_ref[tg, pl.ds(p, N_KEYS, stride=SUBLANES), :] = (
            lax.shift_right_logical(lo_, jnp.uint32(16)) | (hi_ & jnp.uint32(0xFFFF0000)))


def _peer_kernel(x1_ref, h2_ref, q_ref, kh_ref, kl_ref, u_ref, v_ref, out_ref, gd_ref, rs_ref, tok_ref, *, tb, ec):
    ib = pl.program_id(0)
    c = pl.program_id(1)
    half_tok = tb // 2
    n_lg = half_tok // LANES
    n_pieces = ec // EXPERT_PIECE
    grp_per_piece = EXPERT_PIECE // N_KEYS
    w_slot = ib % 2
    r_slot = 1 - w_slot

    @pl.when(jnp.logical_and(ib > 0, c == 0))
    def _build_gate_matrix():
        rid = lax.broadcasted_iota(jnp.int32, (N_KEYS, N_SEL), 0).astype(F32).astype(BF16)

        def body(it, carry):
            for k in range(GATE_GROUPS_PER_ITER):
                tg = it * GATE_GROUPS_PER_ITER + k
                r0 = pl.multiple_of(tg * TOKENS_PER_GATE_GROUP, TOKENS_PER_GATE_GROUP)
                rows = pl.ds(r0, TOKENS_PER_GATE_GROUP)
                _gate_tiles(gd_ref, tg, tok_ref[r_slot, 0, rows, :].astype(BF16),
                            tok_ref[r_slot, 1, rows, :].astype(BF16),
                            (0.5 * tok_ref[r_slot, 2, rows, :]).astype(BF16), rid)
            return carry

        lax.fori_loop(0, tb // (TOKENS_PER_GATE_GROUP * GATE_GROUPS_PER_ITER), body, 0)
        out_ref[...] = x1_ref[...]

    hd = c // 2
    half = c % 2
    base = pl.multiple_of(c * (ec // N_KEYS * SUBLANES), SUBLANES)

    def scores(part):
        qh_, ql_ = _split_bf16(q_ref[:, part * PEER_HALF:(part + 1) * PEER_HALF])
        kh, kl = kh_ref[hd * 2 + part], kl_ref[hd * 2 + part]
        return (lax.dot_general(kh, qh_, _NT, preferred_element_type=F32)
                + lax.dot_general(kh, ql_, _NT, preferred_element_type=F32)
                + lax.dot_general(kl, qh_, _NT, preferred_element_type=F32))

    def weights(j):
        a = lax.dot_general(h2_ref[...], u_ref[j * EXPERT_PIECE:(j + 1) * EXPERT_PIECE, :], _NT,
                            preferred_element_type=F32)
        gd = jnp.concatenate(
            [pltpu.bitcast(gd_ref[:, pl.ds(base + (grp_per_piece * j + g) * SUBLANES, SUBLANES), :]
                           .reshape(tb // 2, N_KEYS), BF16) for g in range(grp_per_piece)], axis=1)
        return gd * (a * (1.0 + lax.erf(a * (2.0 ** -0.5)))).astype(BF16)

    def zero_after(words):
        u = pltpu.bitcast(words, jnp.uint32)
        z = lax.shift_right_logical(lax.shift_right_logical(u, jnp.uint32(16)), jnp.uint32(16))
        return jnp.max(z.astype(F32), axis=0, keepdims=True)

    def step(with_experts):
        tops = [[None] * n_lg, [None] * n_lg]
        w_pieces = []
        want = n_pieces if with_experts else 0
        for part in range(2):
            st = scores(part)
            for lg in range(n_lg):
                blk = st[:, lg * LANES:(lg + 1) * LANES]
                if with_experts and part == 1 and lg == n_lg - 1 and len(w_pieces) == n_pieces:
                    blk = blk + zero_after(w_pieces[-1][0:2 * SUBLANES, 0:LANES])
                tops[part][lg] = _top16(blk, _row_iota(N_KEYS), float(N_KEYS))
                w_pieces.extend(weights(j) for j in range(len(w_pieces), min(len(w_pieces) + 2, want)))
        w_pieces.extend(weights(j) for j in range(len(w_pieces), want))
        zero = 0.0
        if with_experts:
            hk = ec // 2
            acc = (jnp.dot(jnp.concatenate(w_pieces[:n_pieces // 2], axis=1), v_ref[0:hk, :], preferred_element_type=F32)
                   + jnp.dot(jnp.concatenate(w_pieces[n_pieces // 2:], axis=1), v_ref[hk:, :],
                             preferred_element_type=F32))
            zero = zero_after(acc[0:SUBLANES, 0:LANES])
        sel = [_select_pairs((tops[0][lg][0] + zero, tops[0][lg][1]), tops[1][lg]) for lg in range(n_lg)]
        rows = pl.ds(pl.multiple_of(hd * PEER_TOPK, PEER_TOPK), PEER_TOPK)
        for k in range(3):
            rs_ref[w_slot, half, k, rows, :] = jnp.concatenate([s_[k] for s_ in sel], axis=1)
        if with_experts:
            out_ref[...] += acc

    pl.when(ib == 0)(functools.partial(step, False))
    pl.when(ib > 0)(functools.partial(step, True))

    @pl.when(c == pl.num_programs(1) - 1)
    def _routing_to_token_major():
        for hf in range(2):
            for k in range(3):
                tok_ref[w_slot, k, hf * half_tok:(hf + 1) * half_tok, :] = rs_ref[w_slot, hf, k].T


def _peer_call(x1, h2, pq, sub_keys, expert_u, expert_v, tb, ec):
    t = x1.shape[0]
    nb = t // tb
    nc = N_EXPERTS // ec
    assert nc == 2 * PEER_HEADS and tb % (2 * LANES) == 0 and ec % EXPERT_PIECE == 0
    keys = sub_keys.astype(F32).reshape(PEER_HEADS * 2, N_KEYS, PEER_HALF)
    kh, kl = _split_bf16(keys)
    prev_row = lambda c_: pl.BlockSpec((tb, c_), lambda i, c: (jnp.maximum(i - 1, 0), 0))
    chunk = pl.BlockSpec((ec, D_MODEL), lambda i, c: (c, 0))
    q_spec = pl.BlockSpec((tb // 2, 2 * PEER_HALF), lambda i, c: (jnp.minimum(i, nb - 1) * 2 + c % 2, c // 2))
    full3 = pl.BlockSpec((PEER_HEADS * 2, N_KEYS, PEER_HALF), lambda i, c: (0, 0, 0))
    in_specs = [prev_row(D_MODEL), prev_row(D_MODEL), q_spec, full3, full3, chunk, chunk]
    operands = [x1, h2, pq, kh, kl, expert_u, expert_v]
    out_spec = prev_row(D_MODEL)
    out_shape = jax.ShapeDtypeStruct((t, D_MODEL), F32)
    scratch_shapes = [pltpu.VMEM((tb // TOKENS_PER_GATE_GROUP, N_KEYS * SUBLANES, N_KEYS), jnp.uint32),
                      pltpu.VMEM((2, 2, 3, N_SEL, tb // 2), F32),
                      pltpu.VMEM((2, 3, tb, N_SEL), F32)]
    return pl.pallas_call(
        functools.partial(_peer_kernel, tb=tb, ec=ec),
        grid=(nb + 1, nc),
        in_specs=in_specs,
        out_specs=out_spec,
        out_shape=out_shape,
        scratch_shapes=scratch_shapes,
        compiler_params=pltpu.CompilerParams(
            dimension_semantics=("arbitrary", "arbitrary"),
            vmem_limit_bytes=_vmem_limit(in_specs, operands, [out_spec], [out_shape], scratch_shapes)),
        name="peer_block_experts",
    )(*operands)


def _tile_sizes(bsz, s):
    t = bsz * s
    tm = math.gcd(t, 512)
    ts = math.gcd(s, 512)
    tb = math.gcd(t, 512)
    return tm, ts, tb


def kernel(x, norm_mix, norm_ffn, w_in, b_gate, w_conv, q_norm, k_norm, sinks, rel_bias, w_conv_out, w_attn_out,
           w_out, w_query, sub_keys, expert_u, expert_v):
    bsz, s, d = x.shape
    assert d == D_MODEL and s % BLOCK == 0
    tm, ts, tb = _tile_sizes(bsz, s)
    biasm = _band_bias(rel_bias)
    x2 = x.reshape(bsz * s, d)
    for l in range(norm_mix.shape[0]):
        u16, v16, hc, bc, qn, kv, gc, ga = _proj_call(x2, norm_mix[l], w_in[l], b_gate[l], q_norm[l], k_norm[l],
                                                      expert_u[l], expert_v[l], tm)
        x1, h2, pq = _mixer_call(x2, hc, bc, qn, kv, gc, ga, w_conv[l], biasm, sinks[l], w_conv_out[l],
                                 w_attn_out[l], w_out[l], norm_ffn[l], w_query[l], bsz, s, ts)
        x2 = _peer_call(x1, h2, pq, sub_keys[l], u16, v16, tb, EXPERT_CHUNK)
    return x2.reshape(bsz, s, d)
```

```python
import functools
import math

import jax
import jax.numpy as jnp
import numpy as np
from jax import lax
from jax.experimental import pallas as pl
from jax.experimental.pallas import tpu as pltpu

F32 = jnp.float32
BF16 = jnp.bfloat16

D_MODEL = 1024
CONV_DIM = 512
CONV_K = 3
N_HEADS = 8
N_KV_HEADS = 2
HEAD_DIM = 64
Q_DIM = N_HEADS * HEAD_DIM
KV_DIM = N_KV_HEADS * HEAD_DIM
GROUP = N_HEADS // N_KV_HEADS
WINDOW = 128
BLOCK = 128
N_BUCKETS = 32
MAX_DISTANCE = 128
PEER_HEADS = 8
N_KEYS = 128
N_EXPERTS = N_KEYS * N_KEYS
PEER_QDIM = 256
PEER_HALF = PEER_QDIM // 2
PEER_TOPK = 16
N_SEL = PEER_HEADS * PEER_TOPK
EPS = 1e-6
NEG_BIG = -1e30
PAIR_W = 2 * HEAD_DIM
KV_WIDE = N_KV_HEADS * PAIR_W

SUBLANES = 8
LANES = 128

EXPERT_CHUNK = N_EXPERTS // (2 * PEER_HEADS)
EXPERT_PIECE = 2 * N_KEYS
TOKENS_PER_GATE_GROUP = 2 * SUBLANES
GATE_GROUPS_PER_ITER = 16
MAX_CAST_SLAB = 2 * 1024 * 1024
V7X_VMEM_BYTES = 64 * 1024 * 1024
COMPILER_TEMP_VMEM = 12 * 1024 * 1024

_OFFS = np.cumsum([0, CONV_DIM, CONV_DIM, CONV_DIM, Q_DIM, KV_DIM, KV_DIM, D_MODEL, D_MODEL]).tolist()

_NT = (((1,), (1,)), ((), ()))


def _vmem_limit(in_specs, operands, out_specs, out_shapes, scratch_shapes=()):
    nbytes = lambda shape, dtype: math.prod(shape) * jnp.dtype(dtype).itemsize
    windows = sum(nbytes(s.block_shape, a.dtype)
                  for s, a in zip(list(in_specs) + list(out_specs), list(operands) + list(out_shapes))
                  if s.block_shape is not None)
    need = 2 * windows + sum(nbytes(s.shape, s.dtype) for s in scratch_shapes) + COMPILER_TEMP_VMEM
    assert need <= V7X_VMEM_BYTES, need
    return need


def _split_bf16(a):
    hi = a.astype(BF16)
    lo = (a - hi.astype(F32)).astype(BF16)
    return hi, lo


def _proj_kernel(x_ref, g_ref, w_ref, bg_ref, qg_ref, kg_ref, avq_ref, avk_ref, exp_ref, u32_ref, v32_ref,
                 u16_ref, v16_ref, hc_ref, bc_ref, qn_ref, kv_ref, gc_ref, ga_ref):
    u16_ref[...] = u32_ref[...].astype(BF16)
    v16_ref[...] = v32_ref[...].astype(BF16)
    x = x_ref[...]
    ms = jnp.mean(x * x, axis=-1, keepdims=True)
    h = ((x * lax.rsqrt(ms + EPS)) * g_ref[...]).astype(BF16)

    def seg(i):
        return jnp.dot(h, w_ref[:, _OFFS[i]:_OFFS[i + 1]], preferred_element_type=F32)

    u = seg(0)
    hc_ref[...] = seg(2) * u
    bc_ref[...] = seg(1)

    def head_rms(a, av_ref):
        hi, lo = _split_bf16(a * a)
        return (jnp.dot(hi, av_ref[...], preferred_element_type=F32)
                + jnp.dot(lo, av_ref[...], preferred_element_type=F32))

    q = seg(3)
    qn = (q * lax.rsqrt(head_rms(q, avq_ref) + EPS)) * qg_ref[...] * (HEAD_DIM ** -0.5)
    qn_ref[...] = qn.astype(BF16)

    k = seg(4)
    kn = ((k * lax.rsqrt(head_rms(k, avk_ref) + EPS)) * kg_ref[...]).astype(BF16)
    v = seg(5).astype(BF16)
    kv_ref[...] = jnp.dot(jnp.concatenate([kn, v], axis=1), exp_ref[...], preferred_element_type=F32).astype(BF16)

    bg = bg_ref[...]
    gc_ref[...] = jax.nn.sigmoid(seg(6) + bg[:, :D_MODEL]).astype(BF16)
    ga_ref[...] = jax.nn.sigmoid(seg(7) + bg[:, D_MODEL:]).astype(BF16)


def _kv_layout():
    e = np.zeros((2 * KV_DIM, 3 * KV_WIDE), np.float32)
    for g in range(N_KV_HEADS):
        for d in range(HEAD_DIM):
            e[g * HEAD_DIM + d, g * PAIR_W + d] = 1.0
            e[g * HEAD_DIM + d, KV_WIDE + g * PAIR_W + HEAD_DIM + d] = 1.0
            e[KV_DIM + g * HEAD_DIM + d, 2 * KV_WIDE + g * PAIR_W + d] = 1.0
            e[KV_DIM + g * HEAD_DIM + d, 2 * KV_WIDE + g * PAIR_W + HEAD_DIM + d] = 1.0
    return jnp.asarray(e, BF16)


def _proj_call(x2, norm_mix, w_in, b_gate, q_norm, k_norm, expert_u, expert_v, tm):
    t = x2.shape[0]
    in_dim = w_in.shape[1]
    steps = t // tm
    slab_rows = N_EXPERTS // steps
    assert slab_rows * steps == N_EXPERTS and slab_rows % TOKENS_PER_GATE_GROUP == 0
    assert slab_rows * D_MODEL * 4 <= MAX_CAST_SLAB, "too few grid steps to cast the expert tables slice by slice"
    avq = jnp.kron(jnp.eye(N_HEADS, dtype=F32), jnp.full((HEAD_DIM, HEAD_DIM), 1.0 / HEAD_DIM, F32)).astype(BF16)
    avk = jnp.kron(jnp.eye(N_KV_HEADS, dtype=F32), jnp.full((HEAD_DIM, HEAD_DIM), 1.0 / HEAD_DIM, F32)).astype(BF16)
    qg = jnp.tile(q_norm.astype(F32), N_HEADS).reshape(1, Q_DIM)
    kg = jnp.tile(k_norm.astype(F32), N_KV_HEADS).reshape(1, KV_DIM)
    full = lambda shape: pl.BlockSpec(shape, lambda i: (0,) * len(shape))
    row = lambda c: pl.BlockSpec((tm, c), lambda i: (i, 0))
    slab = lambda: pl.BlockSpec((slab_rows, D_MODEL), lambda i: (i, 0))
    in_specs = [row(D_MODEL), full((1, D_MODEL)), full((D_MODEL, in_dim)), full((1, 2 * D_MODEL)),
                full((1, Q_DIM)), full((1, KV_DIM)), full((Q_DIM, Q_DIM)), full((KV_DIM, KV_DIM)),
                full((2 * KV_DIM, 3 * KV_WIDE)), slab(), slab()]
    operands = [x2, norm_mix.reshape(1, D_MODEL).astype(F32), w_in.astype(BF16),
                b_gate.reshape(1, 2 * D_MODEL).astype(F32), qg, kg, avq, avk, _kv_layout(),
                expert_u.astype(F32), expert_v.astype(F32)]
    out_specs = [slab(), slab(), row(CONV_DIM), row(CONV_DIM), row(Q_DIM), row(3 * KV_WIDE), row(D_MODEL),
                 row(D_MODEL)]
    out_shape = [jax.ShapeDtypeStruct((N_EXPERTS, D_MODEL), BF16), jax.ShapeDtypeStruct((N_EXPERTS, D_MODEL), BF16),
                 jax.ShapeDtypeStruct((t, CONV_DIM), F32), jax.ShapeDtypeStruct((t, CONV_DIM), F32),
                 jax.ShapeDtypeStruct((t, Q_DIM), BF16), jax.ShapeDtypeStruct((t, 3 * KV_WIDE), BF16),
                 jax.ShapeDtypeStruct((t, D_MODEL), BF16), jax.ShapeDtypeStruct((t, D_MODEL), BF16)]
    return pl.pallas_call(
        _proj_kernel,
        grid=(t // tm,),
        in_specs=in_specs,
        out_specs=out_specs,
        out_shape=out_shape,
        compiler_params=pltpu.CompilerParams(
            dimension_semantics=("arbitrary",),
            vmem_limit_bytes=_vmem_limit(in_specs, operands, out_specs, out_shape)),
        name="peer_block_proj",
    )(*operands)


def _mixer_kernel(x_ref, hc_ref, hcp_ref, bc_ref, qn_ref, kv_ref, kvp_ref, gc_ref, ga_ref,
                  wc_ref, bias_ref, sink_ref, wco_ref, wao_ref, wo_ref, g2_ref, wq_ref,
                  x1_ref, h2_ref, pq_ref, att_ref, *, ts):
    j = pl.program_id(1)
    first = j == 0

    hc = hc_ref[...]
    prev = jnp.where(first, 0.0, hcp_ref[...])
    rows = lax.broadcasted_iota(jnp.int32, hc.shape, 0)
    s1 = pltpu.roll(hc, 1, axis=0)
    s1 = jnp.where(rows == 0, prev[SUBLANES - 1:SUBLANES, :], s1)
    s2 = pltpu.roll(hc, 2, axis=0)
    s2 = jnp.where(rows == 0, prev[SUBLANES - 2:SUBLANES - 1, :], s2)
    s2 = jnp.where(rows == 1, prev[SUBLANES - 1:SUBLANES, :], s2)
    wc = wc_ref[...]
    conv = s2 * wc[0:1, :] + s1 * wc[1:2, :] + hc * wc[2:3, :]
    yc = (bc_ref[...] * conv).astype(BF16)
    y_conv = jnp.dot(yc, wco_ref[...], preferred_element_type=F32)

    kvfull = jnp.concatenate([kvp_ref[...], kv_ref[...]], axis=0)
    col = lax.broadcasted_iota(jnp.int32, (BLOCK, 2 * BLOCK), 1)
    pen0 = jnp.where(jnp.logical_and(first, col < BLOCK), NEG_BIG, 0.0)
    even_lanes = lax.broadcasted_iota(jnp.int32, (BLOCK, PAIR_W), 1) < HEAD_DIM
    for r in range(ts // BLOCK):
        kvb = kvfull[r * BLOCK:(r + 2) * BLOCK, :]
        for pair in range(N_HEADS // 2):
            g = (2 * pair) // GROUP
            qs = qn_ref[r * BLOCK:(r + 1) * BLOCK, pair * PAIR_W:(pair + 1) * PAIR_W]
            vsel = kvb[:, 2 * KV_WIDE + g * PAIR_W:2 * KV_WIDE + (g + 1) * PAIR_W]
            outs = []
            for par in range(2):
                hd = 2 * pair + par
                ksel = kvb[:, par * KV_WIDE + g * PAIR_W:par * KV_WIDE + (g + 1) * PAIR_W]
                lg = lax.dot_general(qs, ksel, _NT, preferred_element_type=F32) + bias_ref[hd]
                if r == 0:
                    lg = lg + pen0
                sink = sink_ref[hd]
                m = jnp.maximum(jnp.max(lg, axis=-1, keepdims=True), sink)
                p = jnp.exp(lg - m)
                denom = jnp.sum(p, axis=-1, keepdims=True) + jnp.exp(sink - m)
                pv = jnp.dot(p.astype(BF16), vsel, preferred_element_type=F32)
                outs.append(pv / denom)
            att_ref[r * BLOCK:(r + 1) * BLOCK, pair * PAIR_W:(pair + 1) * PAIR_W] = (
                jnp.where(even_lanes, outs[0], outs[1]).astype(BF16))
    y_attn = jnp.dot(att_ref[...], wao_ref[...], preferred_element_type=F32)

    mixed = (gc_ref[...].astype(F32) * y_conv + ga_ref[...].astype(F32) * y_attn).astype(BF16)
    x1 = x_ref[...] + jnp.dot(mixed, wo_ref[...], preferred_element_type=F32)
    x1_ref[...] = x1

    ms = jnp.mean(x1 * x1, axis=-1, keepdims=True)
    h2 = ((x1 * lax.rsqrt(ms + EPS)) * g2_ref[...]).astype(BF16)
    h2_ref[...] = h2
    pq_ref[...] = jnp.dot(h2, wq_ref[...], preferred_element_type=F32)


def _mixer_call(x2, hc, bc, qn, kv, gc, ga, w_conv, biasm, sinks, w_conv_out, w_attn_out, w_out,
                norm_ffn, w_query, bsz, s, ts):
    t = bsz * s
    qd = w_query.shape[1]
    nj = s // ts
    row = lambda c: pl.BlockSpec((ts, c), lambda b, j: (b * nj + j, 0))
    prev_blk = lambda c: pl.BlockSpec((BLOCK, c), lambda b, j: (jnp.maximum((b * nj + j) * (ts // BLOCK) - 1, 0), 0))
    prev8 = pl.BlockSpec((SUBLANES, CONV_DIM),
                         lambda b, j: (jnp.maximum((b * nj + j) * (ts // SUBLANES) - 1, 0), 0))
    full = lambda shape: pl.BlockSpec(shape, lambda b, j: (0,) * len(shape))
    in_specs = [row(D_MODEL), row(CONV_DIM), prev8, row(CONV_DIM), row(Q_DIM),
                row(3 * KV_WIDE), prev_blk(3 * KV_WIDE), row(D_MODEL), row(D_MODEL),
                full((CONV_K, CONV_DIM)), full((N_HEADS, BLOCK, 2 * BLOCK)),
                pl.BlockSpec(memory_space=pltpu.SMEM),
                full((CONV_DIM, D_MODEL)), full((Q_DIM, D_MODEL)), full((D_MODEL, D_MODEL)),
                full((1, D_MODEL)), full((D_MODEL, qd))]
    operands = [x2, hc, hc, bc, qn, kv, kv, gc, ga, w_conv.astype(F32), biasm, sinks.astype(F32),
                w_conv_out.astype(BF16), w_attn_out.astype(BF16), w_out.astype(BF16),
                norm_ffn.reshape(1, D_MODEL).astype(F32), w_query.astype(BF16)]
    out_specs = [row(D_MODEL), row(D_MODEL), row(qd)]
    out_shape = [jax.ShapeDtypeStruct((t, D_MODEL), F32), jax.ShapeDtypeStruct((t, D_MODEL), BF16),
                 jax.ShapeDtypeStruct((t, qd), F32)]
    scratch_shapes = [pltpu.VMEM((ts, Q_DIM), BF16)]
    return pl.pallas_call(
        functools.partial(_mixer_kernel, ts=ts),
        grid=(bsz, nj),
        in_specs=in_specs,
        out_specs=out_specs,
        out_shape=out_shape,
        scratch_shapes=scratch_shapes,
        compiler_params=pltpu.CompilerParams(
            dimension_semantics=("arbitrary", "arbitrary"),
            vmem_limit_bytes=_vmem_limit(in_specs, operands, out_specs, out_shape, scratch_shapes)),
        name="peer_block_mixer",
    )(*operands)


def _band_bias(rel_bias):
    q_loc = jnp.arange(BLOCK, dtype=jnp.int32)[:, None]
    k_loc = jnp.arange(2 * BLOCK, dtype=jnp.int32)[None, :]
    dist = q_loc + BLOCK - k_loc
    max_exact = N_BUCKETS // 2
    d = jnp.maximum(dist, 0)
    df = jnp.maximum(d, 1).astype(F32)
    large = max_exact + (jnp.log(df / max_exact) / math.log(MAX_DISTANCE / max_exact)
                         * (N_BUCKETS - max_exact)).astype(jnp.int32)
    large = jnp.minimum(large, N_BUCKETS - 1)
    bucket = jnp.where(d < max_exact, d, large)
    onehot = (bucket[None] == jnp.arange(N_BUCKETS, dtype=jnp.int32)[:, None, None]).astype(F32)
    bias = jnp.einsum("bh,bqk->hqk", rel_bias.astype(F32), onehot, precision=lax.Precision.HIGHEST)
    valid = (dist >= 0) & (dist < WINDOW)
    return jnp.where(valid[None], bias, NEG_BIG)


def _sorting_network(n):
    def merge(lo, hi, r):
        step = r * 2
        if step < hi - lo:
            yield from merge(lo, hi, step)
            yield from merge(lo + r, hi, step)
            yield from ((i, i + r) for i in range(lo + r, hi - r, step))
        else:
            yield (lo, lo + r)

    def sort(lo, hi):
        if hi > lo:
            mid = lo + (hi - lo) // 2
            yield from sort(lo, mid)
            yield from sort(mid + 1, hi)
            yield from merge(lo, hi, 1)

    return list(sort(0, n - 1))


def _top16(vals, rid, big):
    n = vals.shape[0] // SUBLANES
    x = [vals[k * SUBLANES:(k + 1) * SUBLANES, :] for k in range(n)]
    ids = [rid[k * SUBLANES:(k + 1) * SUBLANES, :] for k in range(n)]
    for i, j in _sorting_network(n):
        a, b, ia, ib = x[i], x[j], ids[i], ids[j]
        first = (a > b) | ((a == b) & (ia < ib))
        x[i], x[j] = jnp.maximum(a, b), jnp.minimum(a, b)
        ids[i], ids[j] = jnp.where(first, ia, ib), jnp.where(first, ib, ia)
    tv, ti = [], []
    for r in range(PEER_TOPK):
        m = jnp.max(x[0], axis=0, keepdims=True)
        idx = jnp.min(jnp.where(x[0] == m, ids[0], big), axis=0, keepdims=True)
        tv.append(m)
        ti.append(idx)
        won = ids[0] == idx
        for k in range(min(n - 1, PEER_TOPK - 1 - r)):
            x[k] = jnp.where(won, x[k + 1], x[k])
            ids[k] = jnp.where(won, ids[k + 1], ids[k])
        if n - 1 < PEER_TOPK - 1 - r:
            x[n - 1] = jnp.where(won, -jnp.inf, x[n - 1])
    return jnp.concatenate(tv, axis=0), jnp.concatenate(ti, axis=0)


def _row_iota(n_rows):
    return lax.broadcasted_iota(jnp.int32, (n_rows, LANES), 0).astype(F32)


def _pair_candidates(v0, v1):
    assert PEER_TOPK == 2 * SUBLANES
    half = SUBLANES // 2
    r8 = _row_iota(SUBLANES)
    r16 = _row_iota(PEER_TOPK)
    vals = [v0[0:1, :] + v1]
    pos = [r16]
    for a in (1, 2, 3):
        vals.append(v0[a:a + 1, :] + v1[0:SUBLANES, :])
        pos.append(r8 + float(a * PEER_TOPK))
    low = r8 < float(half)
    v1dup = jnp.where(low, v1[0:SUBLANES, :], pltpu.roll(v1[0:SUBLANES, :], half, axis=0))
    bdup = jnp.where(low, r8, r8 - float(half))
    for a in (4, 6):
        vals.append(jnp.where(low, v0[a:a + 1, :], v0[a + 1:a + 2, :]) + v1dup)
        pos.append(jnp.where(low, float(a * PEER_TOPK), float((a + 1) * PEER_TOPK)) + bdup)
    vals.append(v0[SUBLANES:, :] + v1[0:1, :])
    pos.append((r8 + float(SUBLANES)) * float(PEER_TOPK))
    return jnp.concatenate(vals, axis=0), jnp.concatenate(pos, axis=0)


def _pick(table, sel):
    out = jnp.zeros_like(sel)
    for a in range(PEER_TOPK):
        out = jnp.where(sel == float(a), table[a:a + 1, :], out)
    return out


def _select_pairs(top0, top1):
    (v0, j0), (v1, j1) = top0, top1
    cand, cpos = _pair_candidates(v0, v1)
    ts_, pos = _top16(cand, cpos, float(PEER_TOPK * PEER_TOPK))
    pa = jnp.floor(pos * (1.0 / PEER_TOPK))
    pb = pos - pa * PEER_TOPK
    e = jnp.exp(ts_ - ts_[0:1, :])
    return _pick(j0, pa), _pick(j1, pb), e / jnp.sum(e, axis=0, keepdims=True)


def _gate_tiles(gd_ref, tg, i0b, i1b, gb, rid):
    zero = jnp.zeros_like(rid)
    one = jnp.ones_like(rid)
    for p in range(SUBLANES):
        tiles = []
        for t in (2 * p, 2 * p + 1):
            rt = jnp.where(rid == i0b[t:t + 1, :], gb[t:t + 1, :], zero)
            ct = jnp.where(rid == i1b[t:t + 1, :], one, zero)
            tiles.append(lax.dot_general(rt, ct, _NT, preferred_element_type=F32))
        lo_, hi_ = (pltpu.bitcast(t_.astype(BF16).astype(F32), jnp.uint32) for t_ in tiles)
        gd_ref[tg, pl.ds(p, N_KEYS, stride=SUBLANES), :] = (
            lax.shift_right_logical(lo_, jnp.uint32(16)) | (hi_ & jnp.uint32(0xFFFF0000)))


def _peer_kernel(x1_ref, h2_ref, q_ref, kh_ref, kl_ref, u_ref, v_ref, out_ref, gd_ref, rs_ref, tok_ref, *, tb, ec):
    ib = pl.program_id(0)
    c = pl.program_id(1)
    half_tok = tb // 2
    n_lg = half_tok // LANES
    n_pieces = ec // EXPERT_PIECE
    grp_per_piece = EXPERT_PIECE // N_KEYS
    w_slot = ib % 2
    r_slot = 1 - w_slot

    @pl.when(jnp.logical_and(ib > 0, c == 0))
    def _build_gate_matrix():
        rid = lax.broadcasted_iota(jnp.int32, (N_KEYS, N_SEL), 0).astype(F32).astype(BF16)

        def body(it, carry):
            for k in range(GATE_GROUPS_PER_ITER):
                tg = it * GATE_GROUPS_PER_ITER + k
                r0 = pl.multiple_of(tg * TOKENS_PER_GATE_GROUP, TOKENS_PER_GATE_GROUP)
                rows = pl.ds(r0, TOKENS_PER_GATE_GROUP)
                _gate_tiles(gd_ref, tg, tok_ref[r_slot, 0, rows, :].astype(BF16),
                            tok_ref[r_slot, 1, rows, :].astype(BF16),
                            (0.5 * tok_ref[r_slot, 2, rows, :]).astype(BF16), rid)
            return carry

        lax.fori_loop(0, tb // (TOKENS_PER_GATE_GROUP * GATE_GROUPS_PER_ITER), body, 0)
        out_ref[...] = x1_ref[...]

    hd = c // 2
    half = c % 2
    base = pl.multiple_of(c * (ec // N_KEYS * SUBLANES), SUBLANES)

    def scores(part):
        qh_, ql_ = _split_bf16(q_ref[:, part * PEER_HALF:(part + 1) * PEER_HALF])
        kh, kl = kh_ref[hd * 2 + part], kl_ref[hd * 2 + part]
        return (lax.dot_general(kh, qh_, _NT, preferred_element_type=F32)
                + lax.dot_general(kh, ql_, _NT, preferred_element_type=F32)
                + lax.dot_general(kl, qh_, _NT, preferred_element_type=F32))

    def weights(j):
        a = lax.dot_general(h2_ref[...], u_ref[j * EXPERT_PIECE:(j + 1) * EXPERT_PIECE, :], _NT,
                            preferred_element_type=F32)
        gd = jnp.concatenate(
            [pltpu.bitcast(gd_ref[:, pl.ds(base + (grp_per_piece * j + g) * SUBLANES, SUBLANES), :]
                           .reshape(tb // 2, N_KEYS), BF16) for g in range(grp_per_piece)], axis=1)
        return gd * (a * (1.0 + lax.erf(a * (2.0 ** -0.5)))).astype(BF16)

    def zero_after(words):
        u = pltpu.bitcast(words, jnp.uint32)
        z = lax.shift_right_logical(lax.shift_right_logical(u, jnp.uint32(16)), jnp.uint32(16))
        return jnp.max(z.astype(F32), axis=0, keepdims=True)

    def step(with_experts):
        tops = [[None] * n_lg, [None] * n_lg]
        w_pieces = []
        want = n_pieces if with_experts else 0
        for part in range(2):
            st = scores(part)
            for lg in range(n_lg):
                blk = st[:, lg * LANES:(lg + 1) * LANES]
                if with_experts and part == 1 and lg == n_lg - 1 and len(w_pieces) == n_pieces:
                    blk = blk + zero_after(w_pieces[-1][0:2 * SUBLANES, 0:LANES])
                tops[part][lg] = _top16(blk, _row_iota(N_KEYS), float(N_KEYS))
                w_pieces.extend(weights(j) for j in range(len(w_pieces), min(len(w_pieces) + 2, want)))
        w_pieces.extend(weights(j) for j in range(len(w_pieces), want))
        zero = 0.0
        if with_experts:
            hk = ec // 2
            acc = (jnp.dot(jnp.concatenate(w_pieces[:n_pieces // 2], axis=1), v_ref[0:hk, :], preferred_element_type=F32)
                   + jnp.dot(jnp.concatenate(w_pieces[n_pieces // 2:], axis=1), v_ref[hk:, :],
                             preferred_element_type=F32))
            zero = zero_after(acc[0:SUBLANES, 0:LANES])
        sel = [_select_pairs((tops[0][lg][0] + zero, tops[0][lg][1]), tops[1][lg]) for lg in range(n_lg)]
        rows = pl.ds(pl.multiple_of(hd * PEER_TOPK, PEER_TOPK), PEER_TOPK)
        for k in range(3):
            rs_ref[w_slot, half, k, rows, :] = jnp.concatenate([s_[k] for s_ in sel], axis=1)
        if with_experts:
            out_ref[...] += acc

    pl.when(ib == 0)(functools.partial(step, False))
    pl.when(ib > 0)(functools.partial(step, True))

    @pl.when(c == pl.num_programs(1) - 1)
    def _routing_to_token_major():
        for hf in range(2):
            for k in range(3):
                tok_ref[w_slot, k, hf * half_tok:(hf + 1) * half_tok, :] = rs_ref[w_slot, hf, k].T


def _peer_call(x1, h2, pq, sub_keys, expert_u, expert_v, tb, ec):
    t = x1.shape[0]
    nb = t // tb
    nc = N_EXPERTS // ec
    assert nc == 2 * PEER_HEADS and tb % (2 * LANES) == 0 and ec % EXPERT_PIECE == 0
    keys = sub_keys.astype(F32).reshape(PEER_HEADS * 2, N_KEYS, PEER_HALF)
    kh, kl = _split_bf16(keys)
    prev_row = lambda c_: pl.BlockSpec((tb, c_), lambda i, c: (jnp.maximum(i - 1, 0), 0))
    chunk = pl.BlockSpec((ec, D_MODEL), lambda i, c: (c, 0))
    q_spec = pl.BlockSpec((tb // 2, 2 * PEER_HALF), lambda i, c: (jnp.minimum(i, nb - 1) * 2 + c % 2, c // 2))
    full3 = pl.BlockSpec((PEER_HEADS * 2, N_KEYS, PEER_HALF), lambda i, c: (0, 0, 0))
    in_specs = [prev_row(D_MODEL), prev_row(D_MODEL), q_spec, full3, full3, chunk, chunk]
    operands = [x1, h2, pq, kh, kl, expert_u, expert_v]
    out_spec = prev_row(D_MODEL)
    out_shape = jax.ShapeDtypeStruct((t, D_MODEL), F32)
    scratch_shapes = [pltpu.VMEM((tb // TOKENS_PER_GATE_GROUP, N_KEYS * SUBLANES, N_KEYS), jnp.uint32),
                      pltpu.VMEM((2, 2, 3, N_SEL, tb // 2), F32),
                      pltpu.VMEM((2, 3, tb, N_SEL), F32)]
    def pipelined(*refs):
        *hbm, gd_ref, rs_ref, tok_ref = refs
        pltpu.emit_pipeline(
            functools.partial(_peer_kernel, tb=tb, ec=ec),
            grid=(nb + 1, nc),
            in_specs=in_specs,
            out_specs=[out_spec],
        )(*hbm, scratches=(gd_ref, rs_ref, tok_ref))

    any_spec = pl.BlockSpec(memory_space=pl.ANY)
    return pl.pallas_call(
        pipelined,
        in_specs=[any_spec] * len(operands),
        out_specs=any_spec,
        out_shape=out_shape,
        scratch_shapes=scratch_shapes,
        compiler_params=pltpu.CompilerParams(
            vmem_limit_bytes=_vmem_limit(in_specs, operands, [out_spec], [out_shape], scratch_shapes)),
        name="peer_block_experts",
    )(*operands)


def _tile_sizes(bsz, s):
    t = bsz * s
    tm = math.gcd(t, 512)
    ts = math.gcd(s, 512)
    tb = math.gcd(t, 512)
    return tm, ts, tb


def kernel(x, norm_mix, norm_ffn, w_in, b_gate, w_conv, q_norm, k_norm, sinks, rel_bias, w_conv_out, w_attn_out,
           w_out, w_query, sub_keys, expert_u, expert_v):
    bsz, s, d = x.shape
    assert d == D_MODEL and s % BLOCK == 0
    tm, ts, tb = _tile_sizes(bsz, s)
    biasm = _band_bias(rel_bias)
    x2 = x.reshape(bsz * s, d)
    for l in range(norm_mix.shape[0]):
        u16, v16, hc, bc, qn, kv, gc, ga = _proj_call(x2, norm_mix[l], w_in[l], b_gate[l], q_norm[l], k_norm[l],
                                                      expert_u[l], expert_v[l], tm)
        x1, h2, pq = _mixer_call(x2, hc, bc, qn, kv, gc, ga, w_conv[l], biasm, sinks[l], w_conv_out[l],
                                 w_attn_out[l], w_out[l], norm_ffn[l], w_query[l], bsz, s, ts)
        x2 = _peer_call(x1, h2, pq, sub_keys[l], u16, v16, tb, EXPERT_CHUNK)
    return x2.reshape(bsz, s, d)
```

```python
import functools
import math

import jax
import jax.numpy as jnp
import numpy as np
from jax import lax
from jax.experimental import pallas as pl
from jax.experimental.pallas import tpu as pltpu

F32 = jnp.float32
BF16 = jnp.bfloat16

D_MODEL = 1024
CONV_DIM = 512
CONV_K = 3
N_HEADS = 8
N_KV_HEADS = 2
HEAD_DIM = 64
Q_DIM = N_HEADS * HEAD_DIM
KV_DIM = N_KV_HEADS * HEAD_DIM
GROUP = N_HEADS // N_KV_HEADS
WINDOW = 128
BLOCK = 128
N_BUCKETS = 32
MAX_DISTANCE = 128
PEER_HEADS = 8
N_KEYS = 128
N_EXPERTS = N_KEYS * N_KEYS
PEER_QDIM = 256
PEER_HALF = PEER_QDIM // 2
PEER_TOPK = 16
N_SEL = PEER_HEADS * PEER_TOPK
EPS = 1e-6
NEG_BIG = -1e30
PAIR_W = 2 * HEAD_DIM
KV_WIDE = N_KV_HEADS * PAIR_W

SUBLANES = 8
LANES = 128

EXPERT_CHUNK = N_EXPERTS // (2 * PEER_HEADS)
EXPERT_PIECE = 2 * N_KEYS
TOKENS_PER_GATE_GROUP = 2 * SUBLANES
GATE_GROUPS_PER_ITER = 16
MAX_CAST_SLAB = 2 * 1024 * 1024
V7X_VMEM_BYTES = 64 * 1024 * 1024
COMPILER_TEMP_VMEM = 12 * 1024 * 1024

_OFFS = np.cumsum([0, CONV_DIM, CONV_DIM, CONV_DIM, Q_DIM, KV_DIM, KV_DIM, D_MODEL, D_MODEL]).tolist()

_NT = (((1,), (1,)), ((), ()))


def _vmem_limit(in_specs, operands, out_specs, out_shapes, scratch_shapes=()):
    nbytes = lambda shape, dtype: math.prod(shape) * jnp.dtype(dtype).itemsize
    windows = sum(nbytes(s.block_shape, a.dtype)
                  for s, a in zip(list(in_specs) + list(out_specs), list(operands) + list(out_shapes))
                  if s.block_shape is not None)
    need = 2 * windows + sum(nbytes(s.shape, s.dtype) for s in scratch_shapes) + COMPILER_TEMP_VMEM
    assert need <= V7X_VMEM_BYTES, need
    return need


def _split_bf16(a):
    hi = a.astype(BF16)
    lo = (a - hi.astype(F32)).astype(BF16)
    return hi, lo


def _proj_kernel(x_ref, g_ref, w_ref, bg_ref, qg_ref, kg_ref, avq_ref, avk_ref, exp_ref, u32_ref, v32_ref,
                 u16_ref, v16_ref, hc_ref, bc_ref, qn_ref, kv_ref, gc_ref, ga_ref):
    u16_ref[...] = u32_ref[...].astype(BF16)
    v16_ref[...] = v32_ref[...].astype(BF16)
    x = x_ref[...]
    ms = jnp.mean(x * x, axis=-1, keepdims=True)
    h = ((x * lax.rsqrt(ms + EPS)) * g_ref[...]).astype(BF16)

    def seg(i):
        return jnp.dot(h, w_ref[:, _OFFS[i]:_OFFS[i + 1]], preferred_element_type=F32)

    u = seg(0)
    hc_ref[...] = seg(2) * u
    bc_ref[...] = seg(1)

    def head_rms(a, av_ref):
        hi, lo = _split_bf16(a * a)
        return (jnp.dot(hi, av_ref[...], preferred_element_type=F32)
                + jnp.dot(lo, av_ref[...], preferred_element_type=F32))

    q = seg(3)
    qn = (q * lax.rsqrt(head_rms(q, avq_ref) + EPS)) * qg_ref[...] * (HEAD_DIM ** -0.5)
    qn_ref[...] = qn.astype(BF16)

    k = seg(4)
    kn = ((k * lax.rsqrt(head_rms(k, avk_ref) + EPS)) * kg_ref[...]).astype(BF16)
    v = seg(5).astype(BF16)
    kv_ref[...] = jnp.dot(jnp.concatenate([kn, v], axis=1), exp_ref[...], preferred_element_type=F32).astype(BF16)

    bg = bg_ref[...]
    gc_ref[...] = jax.nn.sigmoid(seg(6) + bg[:, :D_MODEL]).astype(BF16)
    ga_ref[...] = jax.nn.sigmoid(seg(7) + bg[:, D_MODEL:]).astype(BF16)


def _kv_layout():
    e = np.zeros((2 * KV_DIM, 3 * KV_WIDE), np.float32)
    for g in range(N_KV_HEADS):
        for d in range(HEAD_DIM):
            e[g * HEAD_DIM + d, g * PAIR_W + d] = 1.0
            e[g * HEAD_DIM + d, KV_WIDE + g * PAIR_W + HEAD_DIM + d] = 1.0
            e[KV_DIM + g * HEAD_DIM + d, 2 * KV_WIDE + g * PAIR_W + d] = 1.0
            e[KV_DIM + g * HEAD_DIM + d, 2 * KV_WIDE + g * PAIR_W + HEAD_DIM + d] = 1.0
    return jnp.asarray(e, BF16)


def _proj_call(x2, norm_mix, w_in, b_gate, q_norm, k_norm, expert_u, expert_v, tm):
    t = x2.shape[0]
    in_dim = w_in.shape[1]
    steps = t // tm
    slab_rows = N_EXPERTS // steps
    assert slab_rows * steps == N_EXPERTS and slab_rows % TOKENS_PER_GATE_GROUP == 0
    assert slab_rows * D_MODEL * 4 <= MAX_CAST_SLAB, "too few grid steps to cast the expert tables slice by slice"
    avq = jnp.kron(jnp.eye(N_HEADS, dtype=F32), jnp.full((HEAD_DIM, HEAD_DIM), 1.0 / HEAD_DIM, F32)).astype(BF16)
    avk = jnp.kron(jnp.eye(N_KV_HEADS, dtype=F32), jnp.full((HEAD_DIM, HEAD_DIM), 1.0 / HEAD_DIM, F32)).astype(BF16)
    qg = jnp.tile(q_norm.astype(F32), N_HEADS).reshape(1, Q_DIM)
    kg = jnp.tile(k_norm.astype(F32), N_KV_HEADS).reshape(1, KV_DIM)
    full = lambda shape: pl.BlockSpec(shape, lambda i: (0,) * len(shape))
    row = lambda c: pl.BlockSpec((tm, c), lambda i: (i, 0))
    slab = lambda: pl.BlockSpec((slab_rows, D_MODEL), lambda i: (i, 0))
    in_specs = [row(D_MODEL), full((1, D_MODEL)), full((D_MODEL, in_dim)), full((1, 2 * D_MODEL)),
                full((1, Q_DIM)), full((1, KV_DIM)), full((Q_DIM, Q_DIM)), full((KV_DIM, KV_DIM)),
                full((2 * KV_DIM, 3 * KV_WIDE)), slab(), slab()]
    operands = [x2, norm_mix.reshape(1, D_MODEL).astype(F32), w_in.astype(BF16),
                b_gate.reshape(1, 2 * D_MODEL).astype(F32), qg, kg, avq, avk, _kv_layout(),
                expert_u.astype(F32), expert_v.astype(F32)]
    out_specs = [slab(), slab(), row(CONV_DIM), row(CONV_DIM), row(Q_DIM), row(3 * KV_WIDE), row(D_MODEL),
                 row(D_MODEL)]
    out_shape = [jax.ShapeDtypeStruct((N_EXPERTS, D_MODEL), BF16), jax.ShapeDtypeStruct((N_EXPERTS, D_MODEL), BF16),
                 jax.ShapeDtypeStruct((t, CONV_DIM), F32), jax.ShapeDtypeStruct((t, CONV_DIM), F32),
                 jax.ShapeDtypeStruct((t, Q_DIM), BF16), jax.ShapeDtypeStruct((t, 3 * KV_WIDE), BF16),
                 jax.ShapeDtypeStruct((t, D_MODEL), BF16), jax.ShapeDtypeStruct((t, D_MODEL), BF16)]
    return pl.pallas_call(
        _proj_kernel,
        grid=(t // tm,),
        in_specs=in_specs,
        out_specs=out_specs,
        out_shape=out_shape,
        compiler_params=pltpu.CompilerParams(
            dimension_semantics=("arbitrary",),
            vmem_limit_bytes=_vmem_limit(in_specs, operands, out_specs, out_shape)),
        name="peer_block_proj",
    )(*operands)


def _mixer_kernel(x_ref, hc_ref, hcp_ref, bc_ref, qn_ref, kv_ref, kvp_ref, gc_ref, ga_ref,
                  wc_ref, bias_ref, sink_ref, wco_ref, wao_ref, wo_ref, g2_ref, wq_ref,
                  x1_ref, h2_ref, pq_ref, att_ref, *, ts):
    j = pl.program_id(1)
    first = j == 0

    hc = hc_ref[...]
    prev = jnp.where(first, 0.0, hcp_ref[...])
    rows = lax.broadcasted_iota(jnp.int32, hc.shape, 0)
    s1 = pltpu.roll(hc, 1, axis=0)
    s1 = jnp.where(rows == 0, prev[SUBLANES - 1:SUBLANES, :], s1)
    s2 = pltpu.roll(hc, 2, axis=0)
    s2 = jnp.where(rows == 0, prev[SUBLANES - 2:SUBLANES - 1, :], s2)
    s2 = jnp.where(rows == 1, prev[SUBLANES - 1:SUBLANES, :], s2)
    wc = wc_ref[...]
    conv = s2 * wc[0:1, :] + s1 * wc[1:2, :] + hc * wc[2:3, :]
    yc = (bc_ref[...] * conv).astype(BF16)
    y_conv = jnp.dot(yc, wco_ref[...], preferred_element_type=F32)

    kvfull = jnp.concatenate([kvp_ref[...], kv_ref[...]], axis=0)
    col = lax.broadcasted_iota(jnp.int32, (BLOCK, 2 * BLOCK), 1)
    pen0 = jnp.where(jnp.logical_and(first, col < BLOCK), NEG_BIG, 0.0)
    even_lanes = lax.broadcasted_iota(jnp.int32, (BLOCK, PAIR_W), 1) < HEAD_DIM
    for r in range(ts // BLOCK):
        kvb = kvfull[r * BLOCK:(r + 2) * BLOCK, :]
        for pair in range(N_HEADS // 2):
            g = (2 * pair) // GROUP
            qs = qn_ref[r * BLOCK:(r + 1) * BLOCK, pair * PAIR_W:(pair + 1) * PAIR_W]
            vsel = kvb[:, 2 * KV_WIDE + g * PAIR_W:2 * KV_WIDE + (g + 1) * PAIR_W]
            outs = []
            for par in range(2):
                hd = 2 * pair + par
                ksel = kvb[:, par * KV_WIDE + g * PAIR_W:par * KV_WIDE + (g + 1) * PAIR_W]
                lg = lax.dot_general(qs, ksel, _NT, preferred_element_type=F32) + bias_ref[hd]
                if r == 0:
                    lg = lg + pen0
                sink = sink_ref[hd]
                m = jnp.maximum(jnp.max(lg, axis=-1, keepdims=True), sink)
                p = jnp.exp(lg - m)
                denom = jnp.sum(p, axis=-1, keepdims=True) + jnp.exp(sink - m)
                pv = jnp.dot(p.astype(BF16), vsel, preferred_element_type=F32)
                outs.append(pv / denom)
            att_ref[r * BLOCK:(r + 1) * BLOCK, pair * PAIR_W:(pair + 1) * PAIR_W] = (
                jnp.where(even_lanes, outs[0], outs[1]).astype(BF16))
    y_attn = jnp.dot(att_ref[...], wao_ref[...], preferred_element_type=F32)

    mixed = (gc_ref[...].astype(F32) * y_conv + ga_ref[...].astype(F32) * y_attn).astype(BF16)
    x1 = x_ref[...] + jnp.dot(mixed, wo_ref[...], preferred_element_type=F32)
    x1_ref[...] = x1

    ms = jnp.mean(x1 * x1, axis=-1, keepdims=True)
    h2 = ((x1 * lax.rsqrt(ms + EPS)) * g2_ref[...]).astype(BF16)
    h2_ref[...] = h2
    pq_ref[...] = jnp.dot(h2, wq_ref[...], preferred_element_type=F32)


def _mixer_call(x2, hc, bc, qn, kv, gc, ga, w_conv, biasm, sinks, w_conv_out, w_attn_out, w_out,
                norm_ffn, w_query, bsz, s, ts):
    t = bsz * s
    qd = w_query.shape[1]
    nj = s // ts
    row = lambda c: pl.BlockSpec((ts, c), lambda b, j: (b * nj + j, 0))
    prev_blk = lambda c: pl.BlockSpec((BLOCK, c), lambda b, j: (jnp.maximum((b * nj + j) * (ts // BLOCK) - 1, 0), 0))
    prev8 = pl.BlockSpec((SUBLANES, CONV_DIM),
                         lambda b, j: (jnp.maximum((b * nj + j) * (ts // SUBLANES) - 1, 0), 0))
    full = lambda shape: pl.BlockSpec(shape, lambda b, j: (0,) * len(shape))
    in_specs = [row(D_MODEL), row(CONV_DIM), prev8, row(CONV_DIM), row(Q_DIM),
                row(3 * KV_WIDE), prev_blk(3 * KV_WIDE), row(D_MODEL), row(D_MODEL),
                full((CONV_K, CONV_DIM)), full((N_HEADS, BLOCK, 2 * BLOCK)),
                pl.BlockSpec(memory_space=pltpu.SMEM),
                full((CONV_DIM, D_MODEL)), full((Q_DIM, D_MODEL)), full((D_MODEL, D_MODEL)),
                full((1, D_MODEL)), full((D_MODEL, qd))]
    operands = [x2, hc, hc, bc, qn, kv, kv, gc, ga, w_conv.astype(F32), biasm, sinks.astype(F32),
                w_conv_out.astype(BF16), w_attn_out.astype(BF16), w_out.astype(BF16),
                norm_ffn.reshape(1, D_MODEL).astype(F32), w_query.astype(BF16)]
    out_specs = [row(D_MODEL), row(D_MODEL), row(qd)]
    out_shape = [jax.ShapeDtypeStruct((t, D_MODEL), F32), jax.ShapeDtypeStruct((t, D_MODEL), BF16),
                 jax.ShapeDtypeStruct((t, qd), F32)]
    scratch_shapes = [pltpu.VMEM((ts, Q_DIM), BF16)]
    return pl.pallas_call(
        functools.partial(_mixer_kernel, ts=ts),
        grid=(bsz, nj),
        in_specs=in_specs,
        out_specs=out_specs,
        out_shape=out_shape,
        scratch_shapes=scratch_shapes,
        compiler_params=pltpu.CompilerParams(
            dimension_semantics=("arbitrary", "arbitrary"),
            vmem_limit_bytes=_vmem_limit(in_specs, operands, out_specs, out_shape, scratch_shapes)),
        name="peer_block_mixer",
    )(*operands)


def _band_bias(rel_bias):
    q_loc = jnp.arange(BLOCK, dtype=jnp.int32)[:, None]
    k_loc = jnp.arange(2 * BLOCK, dtype=jnp.int32)[None, :]
    dist = q_loc + BLOCK - k_loc
    max_exact = N_BUCKETS // 2
    d = jnp.maximum(dist, 0)
    df = jnp.maximum(d, 1).astype(F32)
    large = max_exact + (jnp.log(df / max_exact) / math.log(MAX_DISTANCE / max_exact)
                         * (N_BUCKETS - max_exact)).astype(jnp.int32)
    large = jnp.minimum(large, N_BUCKETS - 1)
    bucket = jnp.where(d < max_exact, d, large)
    onehot = (bucket[None] == jnp.arange(N_BUCKETS, dtype=jnp.int32)[:, None, None]).astype(F32)
    bias = jnp.einsum("bh,bqk->hqk", rel_bias.astype(F32), onehot, precision=lax.Precision.HIGHEST)
    valid = (dist >= 0) & (dist < WINDOW)
    return jnp.where(valid[None], bias, NEG_BIG)


def _sorting_network(n):
    def merge(lo, hi, r):
        step = r * 2
        if step < hi - lo:
            yield from merge(lo, hi, step)
            yield from merge(lo + r, hi, step)
            yield from ((i, i + r) for i in range(lo + r, hi - r, step))
        else:
            yield (lo, lo + r)

    def sort(lo, hi):
        if hi > lo:
            mid = lo + (hi - lo) // 2
            yield from sort(lo, mid)
            yield from sort(mid + 1, hi)
            yield from merge(lo, hi, 1)

    return list(sort(0, n - 1))


def _top16(vals, rid, big):
    n = vals.shape[0] // SUBLANES
    x = [vals[k * SUBLANES:(k + 1) * SUBLANES, :] for k in range(n)]
    ids = [rid[k * SUBLANES:(k + 1) * SUBLANES, :] for k in range(n)]
    for i, j in _sorting_network(n):
        a, b, ia, ib = x[i], x[j], ids[i], ids[j]
        first = (a > b) | ((a == b) & (ia < ib))
        x[i], x[j] = jnp.maximum(a, b), jnp.minimum(a, b)
        ids[i], ids[j] = jnp.where(first, ia, ib), jnp.where(first, ib, ia)
    tv, ti = [], []
    for r in range(PEER_TOPK):
        m = jnp.max(x[0], axis=0, keepdims=True)
        idx = jnp.min(jnp.where(x[0] == m, ids[0], big), axis=0, keepdims=True)
        tv.append(m)
        ti.append(idx)
        won = ids[0] == idx
        for k in range(min(n - 1, PEER_TOPK - 1 - r)):
            x[k] = jnp.where(won, x[k + 1], x[k])
            ids[k] = jnp.where(won, ids[k + 1], ids[k])
        if n - 1 < PEER_TOPK - 1 - r:
            x[n - 1] = jnp.where(won, -jnp.inf, x[n - 1])
    return jnp.concatenate(tv, axis=0), jnp.concatenate(ti, axis=0)


def _row_iota(n_rows):
    return lax.broadcasted_iota(jnp.int32, (n_rows, LANES), 0).astype(F32)


def _pair_candidates(v0, v1):
    assert PEER_TOPK == 2 * SUBLANES
    half = SUBLANES // 2
    r8 = _row_iota(SUBLANES)
    r16 = _row_iota(PEER_TOPK)
    vals = [v0[0:1, :] + v1]
    pos = [r16]
    for a in (1, 2, 3):
        vals.append(v0[a:a + 1, :] + v1[0:SUBLANES, :])
        pos.append(r8 + float(a * PEER_TOPK))
    low = r8 < float(half)
    v1dup = jnp.where(low, v1[0:SUBLANES, :], pltpu.roll(v1[0:SUBLANES, :], half, axis=0))
    bdup = jnp.where(low, r8, r8 - float(half))
    for a in (4, 6):
        vals.append(jnp.where(low, v0[a:a + 1, :], v0[a + 1:a + 2, :]) + v1dup)
        pos.append(jnp.where(low, float(a * PEER_TOPK), float((a + 1) * PEER_TOPK)) + bdup)
    vals.append(v0[SUBLANES:, :] + v1[0:1, :])
    pos.append((r8 + float(SUBLANES)) * float(PEER_TOPK))
    return jnp.concatenate(vals, axis=0), jnp.concatenate(pos, axis=0)


def _pick(table, sel):
    out = jnp.zeros_like(sel)
    for a in range(PEER_TOPK):
        out = jnp.where(sel == float(a), table[a:a + 1, :], out)
    return out


def _select_pairs(top0, top1):
    (v0, j0), (v1, j1) = top0, top1
    cand, cpos = _pair_candidates(v0, v1)
    ts_, pos = _top16(cand, cpos, float(PEER_TOPK * PEER_TOPK))
    pa = jnp.floor(pos * (1.0 / PEER_TOPK))
    pb = pos - pa * PEER_TOPK
    e = jnp.exp(ts_ - ts_[0:1, :])
    return _pick(j0, pa), _pick(j1, pb), e / jnp.sum(e, axis=0, keepdims=True)


def _gate_tiles(gd_ref, tg, i0b, i1b, gb, rid):
    zero = jnp.zeros_like(rid)
    one = jnp.ones_like(rid)
    for p in range(SUBLANES):
        tiles = []
        for t in (2 * p, 2 * p + 1):
            rt = jnp.where(rid == i0b[t:t + 1, :], gb[t:t + 1, :], zero)
            ct = jnp.where(rid == i1b[t:t + 1, :], one, zero)
            tiles.append(lax.dot_general(rt, ct, _NT, preferred_element_type=F32))
        lo_, hi_ = (pltpu.bitcast(t_.astype(BF16).astype(F32), jnp.uint32) for t_ in tiles)
        gd_ref[tg, pl.ds(p, N_KEYS, stride=SUBLANES), :] = (
            lax.shift_right_logical(lo_, jnp.uint32(16)) | (hi_ & jnp.uint32(0xFFFF0000)))


def _peer_kernel(x1_ref, h2_ref, q_ref, kh_ref, kl_ref, u_ref, v_ref, out_ref, gd_ref, rs_ref, tok_ref, *, tb, ec):
    ib = pl.program_id(0)
    c = pl.program_id(1)
    half_tok = tb // 2
    n_lg = half_tok // LANES
    n_pieces = ec // EXPERT_PIECE
    grp_per_piece = EXPERT_PIECE // N_KEYS
    w_slot = ib % 2
    r_slot = 1 - w_slot

    @pl.when(jnp.logical_and(ib > 0, c == 0))
    def _build_gate_matrix():
        rid = lax.broadcasted_iota(jnp.int32, (N_KEYS, N_SEL), 0).astype(F32).astype(BF16)

        def body(it, carry):
            for k in range(GATE_GROUPS_PER_ITER):
                tg = it * GATE_GROUPS_PER_ITER + k
                r0 = pl.multiple_of(tg * TOKENS_PER_GATE_GROUP, TOKENS_PER_GATE_GROUP)
                rows = pl.ds(r0, TOKENS_PER_GATE_GROUP)
                _gate_tiles(gd_ref, tg, tok_ref[r_slot, 0, rows, :].astype(BF16),
                            tok_ref[r_slot, 1, rows, :].astype(BF16),
                            (0.5 * tok_ref[r_slot, 2, rows, :]).astype(BF16), rid)
            return carry

        lax.fori_loop(0, tb // (TOKENS_PER_GATE_GROUP * GATE_GROUPS_PER_ITER), body, 0)
        out_ref[...] = x1_ref[...]

    hd = c // 2
    half = c % 2
    base = pl.multiple_of(c * (ec // N_KEYS * SUBLANES), SUBLANES)

    def scores(part):
        qh_, ql_ = _split_bf16(q_ref[:, part * PEER_HALF:(part + 1) * PEER_HALF])
        kh, kl = kh_ref[hd * 2 + part], kl_ref[hd * 2 + part]
        return (lax.dot_general(kh, qh_, _NT, preferred_element_type=F32)
                + lax.dot_general(kh, ql_, _NT, preferred_element_type=F32)
                + lax.dot_general(kl, qh_, _NT, preferred_element_type=F32))

    def weights(j):
        a = lax.dot_general(h2_ref[...], u_ref[j * EXPERT_PIECE:(j + 1) * EXPERT_PIECE, :], _NT,
                            preferred_element_type=F32)
        gd = jnp.concatenate(
            [pltpu.bitcast(gd_ref[:, pl.ds(base + (grp_per_piece * j + g) * SUBLANES, SUBLANES), :]
                           .reshape(tb // 2, N_KEYS), BF16) for g in range(grp_per_piece)], axis=1)
        return gd * (a * (1.0 + lax.erf(a * (2.0 ** -0.5)))).astype(BF16)

    def zero_after(words):
        u = pltpu.bitcast(words, jnp.uint32)
        z = lax.shift_right_logical(lax.shift_right_logical(u, jnp.uint32(16)), jnp.uint32(16))
        return jnp.max(z.astype(F32), axis=0, keepdims=True)

    def step(with_experts):
        tops = [[None] * n_lg, [None] * n_lg]
        w_pieces = []
        want = n_pieces if with_experts else 0
        for part in range(2):
            st = scores(part)
            for lg in range(n_lg):
                blk = st[:, lg * LANES:(lg + 1) * LANES]
                if with_experts and part == 1 and lg == n_lg - 1 and len(w_pieces) == n_pieces:
                    blk = blk + zero_after(w_pieces[-1][0:2 * SUBLANES, 0:LANES])
                tops[part][lg] = _top16(blk, _row_iota(N_KEYS), float(N_KEYS))
                w_pieces.extend(weights(j) for j in range(len(w_pieces), min(len(w_pieces) + 2, want)))
        w_pieces.extend(weights(j) for j in range(len(w_pieces), want))
        zero = 0.0
        if with_experts:
            hk = ec // 2
            acc = (jnp.dot(jnp.concatenate(w_pieces[:n_pieces // 2], axis=1), v_ref[0:hk, :], preferred_element_type=F32)
                   + jnp.dot(jnp.concatenate(w_pieces[n_pieces // 2:], axis=1), v_ref[hk:, :],
                             preferred_element_type=F32))
            zero = zero_after(acc[0:SUBLANES, 0:LANES])
        sel = [_select_pairs((tops[0][lg][0] + zero, tops[0][lg][1]), tops[1][lg]) for lg in range(n_lg)]
        rows = pl.ds(pl.multiple_of(hd * PEER_TOPK, PEER_TOPK), PEER_TOPK)
        for k in range(3):
            rs_ref[w_slot, half, k, rows, :] = jnp.concatenate([s_[k] for s_ in sel], axis=1)
        if with_experts:
            out_ref[...] += acc

    pl.when(ib == 0)(functools.partial(step, False))
    pl.when(ib > 0)(functools.partial(step, True))

    @pl.when(c == pl.num_programs(1) - 1)
    def _routing_to_token_major():
        for hf in range(2):
            for k in range(3):
                tok_ref[w_slot, k, hf * half_tok:(hf + 1) * half_tok, :] = rs_ref[w_slot, hf, k].T


def _peer_call(x1, h2, pq, sub_keys, expert_u, expert_v, tb, ec):
    t = x1.shape[0]
    nb = t // tb
    nc = N_EXPERTS // ec
    assert nc == 2 * PEER_HEADS and tb % (2 * LANES) == 0 and ec % EXPERT_PIECE == 0
    keys = sub_keys.astype(F32).reshape(PEER_HEADS * 2, N_KEYS, PEER_HALF)
    kh, kl = _split_bf16(keys)
    prev_row = lambda c_: pl.BlockSpec((tb, c_), lambda i, c: (jnp.maximum(i - 1, 0), 0))
    ahead_row = lambda c_: pl.BlockSpec((tb, c_), lambda i, c: (jnp.maximum(i - 1, 0), 0),
                                        pipeline_mode=pl.Buffered(2, use_lookahead=True))
    chunk = pl.BlockSpec((ec, D_MODEL), lambda i, c: (c, 0))
    q_spec = pl.BlockSpec((tb // 2, 2 * PEER_HALF), lambda i, c: (jnp.minimum(i, nb - 1) * 2 + c % 2, c // 2))
    full3 = pl.BlockSpec((PEER_HEADS * 2, N_KEYS, PEER_HALF), lambda i, c: (0, 0, 0))
    in_specs = [ahead_row(D_MODEL), ahead_row(D_MODEL), q_spec, full3, full3, chunk, chunk]
    operands = [x1, h2, pq, kh, kl, expert_u, expert_v]
    out_spec = prev_row(D_MODEL)
    out_shape = jax.ShapeDtypeStruct((t, D_MODEL), F32)
    scratch_shapes = [pltpu.VMEM((tb // TOKENS_PER_GATE_GROUP, N_KEYS * SUBLANES, N_KEYS), jnp.uint32),
                      pltpu.VMEM((2, 2, 3, N_SEL, tb // 2), F32),
                      pltpu.VMEM((2, 3, tb, N_SEL), F32)]
    def pipelined(*refs):
        *hbm, gd_ref, rs_ref, tok_ref = refs
        pltpu.emit_pipeline(
            functools.partial(_peer_kernel, tb=tb, ec=ec),
            grid=(nb + 1, nc),
            in_specs=in_specs,
            out_specs=[out_spec],
        )(*hbm, scratches=(gd_ref, rs_ref, tok_ref))

    any_spec = pl.BlockSpec(memory_space=pl.ANY)
    return pl.pallas_call(
        pipelined,
        in_specs=[any_spec] * len(operands),
        out_specs=any_spec,
        out_shape=out_shape,
        scratch_shapes=scratch_shapes,
        compiler_params=pltpu.CompilerParams(
            vmem_limit_bytes=_vmem_limit(in_specs, operands, [out_spec], [out_shape], scratch_shapes)),
        name="peer_block_experts",
    )(*operands)


def _tile_sizes(bsz, s):
    t = bsz * s
    tm = math.gcd(t, 512)
    ts = math.gcd(s, 512)
    tb = math.gcd(t, 512)
    return tm, ts, tb


def kernel(x, norm_mix, norm_ffn, w_in, b_gate, w_conv, q_norm, k_norm, sinks, rel_bias, w_conv_out, w_attn_out,
           w_out, w_query, sub_keys, expert_u, expert_v):
    bsz, s, d = x.shape
    assert d == D_MODEL and s % BLOCK == 0
    tm, ts, tb = _tile_sizes(bsz, s)
    biasm = _band_bias(rel_bias)
    x2 = x.reshape(bsz * s, d)
    for l in range(norm_mix.shape[0]):
        u16, v16, hc, bc, qn, kv, gc, ga = _proj_call(x2, norm_mix[l], w_in[l], b_gate[l], q_norm[l], k_norm[l],
                                                      expert_u[l], expert_v[l], tm)
        x1, h2, pq = _mixer_call(x2, hc, bc, qn, kv, gc, ga, w_conv[l], biasm, sinks[l], w_conv_out[l],
                                 w_attn_out[l], w_out[l], norm_ffn[l], w_query[l], bsz, s, ts)
        x2 = _peer_call(x1, h2, pq, sub_keys[l], u16, v16, tb, EXPERT_CHUNK)
    return x2.reshape(bsz, s, d)
```
